```python
import jax
import jax.numpy as jnp
from jax import lax

D_MODEL = 1024
BATCH = 4
SEQ = 8192
DEPTH = 2
DEC_BATCH = 32
DEC_SEQ = 1
PAST_LEN = 16384
PAGE_SIZE = 128

ATT_HEADS = 8
ATT_HD = 64
ATT_DIM = ATT_HEADS * ATT_HD
ROPE_DIM = ATT_HD // 4
ROPE_THETA = 500000.0
MOBA_BLOCK = 256
MOBA_TOPK = 3
Q_CHUNK = 128
RET_HEADS = 4
RET_DK = 64
RET_DV = 128
RET_QK_DIM = RET_HEADS * RET_DK
RET_V_DIM = RET_HEADS * RET_DV
RET_THETA = 10000.0
RET_CHUNK = 128
CONV_DIM = 512
CONV_W = 3
N_BRANCH = 3
BRANCH_DIM = 512
D_FF = -(-8 * D_MODEL // (3 * 256)) * 256
ALPHA = (2 * DEPTH) ** 0.25
BETA = (8 * DEPTH) ** -0.25
LN_EPS = 1e-5
IN_SIZES = (ATT_DIM, ATT_DIM, ATT_DIM, RET_QK_DIM, RET_QK_DIM, RET_V_DIM, RET_V_DIM, CONV_DIM, CONV_DIM, CONV_DIM, N_BRANCH * D_MODEL)
IN_DIM = sum(IN_SIZES)
VALUE_SLOTS = (2, 5, 9)

kernel_name = 'hybrid_moba_retention_conv_decoder_step'


def _layer_norm(x, g=None, b=None):
    xf = x.astype(jnp.float32)
    xc = xf - jnp.mean(xf, axis=-1, keepdims=True)
    y = xc * lax.rsqrt(jnp.mean(xc * xc, axis=-1, keepdims=True) + LN_EPS)
    if g is not None:
        y = y * g.astype(jnp.float32) + b.astype(jnp.float32)
    return y.astype(x.dtype)


def _rope(x, pos, rot_dim, theta):
    half = rot_dim // 2
    inv = theta ** (-2.0 * jnp.arange(half, dtype=jnp.float32) / rot_dim)
    ang = pos.astype(jnp.float32)[:, None] * inv[None, :]
    cos = jnp.cos(ang)[:, None, :].astype(x.dtype)
    sin = jnp.sin(ang)[:, None, :].astype(x.dtype)
    x1 = x[..., :half]
    x2 = x[..., half:rot_dim]
    return jnp.concatenate([x1 * cos - x2 * sin, x2 * cos + x1 * sin, x[..., rot_dim:]], axis=-1)


def _split_in(z):
    cuts = []
    acc = 0
    for s in IN_SIZES[:-1]:
        acc += s
        cuts.append(acc)
    return jnp.split(z, cuts, axis=-1)


def _moba_chunk(qc, qpos, b_own, kblk, vblk, kmean):
    bsz, nh = qc.shape[0], qc.shape[1]
    nb = kblk.shape[1]
    topk = min(MOBA_TOPK, nb)
    scale = ATT_HD ** -0.5
    gate = jnp.einsum('bhqd,bnhd->bhqn', qc, kmean).astype(jnp.float32)
    gate = jnp.where(jnp.arange(nb) < b_own, gate, -jnp.inf)
    _, idx = lax.top_k(gate, topk)
    bi = jnp.arange(bsz)[:, None, None]
    hi = jnp.arange(nh)[None, :, None]
    scores = []
    for r in range(topk):
        k_sel = kblk[bi, idx[..., r], :, hi]
        s = jnp.einsum('bhqd,bhqkd->bhqk', qc, k_sel).astype(jnp.float32) * scale
        scores.append(jnp.where(r < b_own, s, -jnp.inf))
    k_own = lax.dynamic_index_in_dim(kblk, b_own, axis=1, keepdims=False)
    v_own = lax.dynamic_index_in_dim(vblk, b_own, axis=1, keepdims=False)
    s_own = jnp.einsum('bhqd,bkhd->bhqk', qc, k_own).astype(jnp.float32) * scale
    kpos = b_own * MOBA_BLOCK + jnp.arange(MOBA_BLOCK)
    s_own = jnp.where(kpos[None, :] <= qpos[:, None], s_own, -jnp.inf)
    p = jax.nn.softmax(jnp.concatenate(scores + [s_own], axis=-1), axis=-1).astype(qc.dtype)
    out = jnp.einsum('bhqk,bkhd->bhqd', p[..., topk * MOBA_BLOCK:], v_own)
    for r in range(topk):
        v_sel = vblk[bi, idx[..., r], :, hi]
        out = out + jnp.einsum('bhqk,bhqkd->bhqd', p[..., r * MOBA_BLOCK:(r + 1) * MOBA_BLOCK], v_sel)
    return out


def _moba_attention(q, k_parts, v_parts, q_pos0):
    bsz, tq = q.shape[0], q.shape[1]
    length = sum(p.shape[1] for p in k_parts)
    n_blocks = -(-length // MOBA_BLOCK)
    pad = jnp.zeros((bsz, n_blocks * MOBA_BLOCK - length, ATT_HEADS, ATT_HD), q.dtype)
    kblk = jnp.concatenate(list(k_parts) + [pad], axis=1).reshape(bsz, n_blocks, MOBA_BLOCK, ATT_HEADS, ATT_HD)
    vblk = jnp.concatenate(list(v_parts) + [pad], axis=1).reshape(bsz, n_blocks, MOBA_BLOCK, ATT_HEADS, ATT_HD)
    kmean = jnp.mean(kblk.astype(jnp.float32), axis=2).astype(q.dtype)
    qh = jnp.swapaxes(q, 1, 2)
    chunk = min(tq, Q_CHUNK)
    n_chunks = tq // chunk

    def body(ci):
        start = ci * chunk
        qc = lax.dynamic_slice_in_dim(qh, start, chunk, axis=2)
        qpos = q_pos0 + start + jnp.arange(chunk)
        return _moba_chunk(qc, qpos, (q_pos0 + start) // MOBA_BLOCK, kblk, vblk, kmean)

    out = lax.map(body, jnp.arange(n_chunks))
    return jnp.transpose(out, (1, 0, 3, 2, 4)).reshape(bsz, tq, ATT_DIM)


def _retention(q, k, v, s0, chunk):
    bsz, t = q.shape[0], q.shape[1]
    n = t // chunk
    log_g = jnp.log(1.0 - 2.0 ** (-5.0 - jnp.arange(RET_HEADS, dtype=jnp.float32)))
    i = jnp.arange(chunk, dtype=jnp.float32)
    diff = i[:, None] - i[None, :]
    dmat = jnp.where(diff >= 0, jnp.exp(jnp.maximum(diff, 0.0)[None] * log_g[:, None, None]), 0.0)
    q_dec = jnp.exp((i + 1.0)[:, None] * log_g[None, :])
    k_dec = jnp.exp((chunk - 1.0 - i)[:, None] * log_g[None, :])
    c_dec = jnp.exp(chunk * log_g)

    def to_chunks(a):
        return jnp.moveaxis(a.astype(jnp.float32).reshape(bsz, n, chunk, a.shape[2], a.shape[3]), 1, 0)

    def step(s, xs):
        qc, kc, vc = xs
        inner = jnp.einsum('bihd,bjhd->bhij', qc, kc) * dmat[None]
        o = jnp.einsum('bhij,bjhe->bihe', inner, vc)
        o = o + jnp.einsum('bihd,bhde->bihe', qc * q_dec[None, :, :, None], s)
        s_new = c_dec[None, :, None, None] * s + jnp.einsum('bjhd,bjhe->bhde', kc * k_dec[None, :, :, None], vc)
        return s_new, o

    s_fin, o = lax.scan(step, s0.astype(jnp.float32), (to_chunks(q), to_chunks(k), to_chunks(v)))
    o = jnp.moveaxis(o, 0, 1).reshape(bsz, t, RET_HEADS, RET_DV)
    return o.astype(v.dtype), s_fin.astype(s0.dtype)


def _mixer(u, pos, pos0, conv_prev, ret_prev, k_past, v_past, w_in, conv_w, w_br, w_o):
    bsz, t, _ = u.shape
    aq, ak, av, rq, rk, rv, rg, cb, cc, cx, gt = _split_in(u @ w_in)
    aq = _rope(aq.reshape(bsz, t, ATT_HEADS, ATT_HD), pos, ROPE_DIM, ROPE_THETA)
    ak = _rope(ak.reshape(bsz, t, ATT_HEADS, ATT_HD), pos, ROPE_DIM, ROPE_THETA)
    av = av.reshape(bsz, t, ATT_HEADS, ATT_HD)
    k_parts = (ak,) if k_past is None else (k_past, ak)
    v_parts = (av,) if v_past is None else (v_past, av)
    a_out = _moba_attention(aq, k_parts, v_parts, pos0)
    rq = _rope(rq.reshape(bsz, t, RET_HEADS, RET_DK), pos, RET_DK, RET_THETA)
    rk = _rope(rk.reshape(bsz, t, RET_HEADS, RET_DK), pos, RET_DK, RET_THETA) * (RET_DK ** -0.5)
    r_o, r_state = _retention(rq, rk, rv.reshape(bsz, t, RET_HEADS, RET_DV), ret_prev, min(t, RET_CHUNK))
    r_out = _layer_norm(r_o).reshape(bsz, t, RET_V_DIM) * jax.nn.silu(rg)
    s = cc * cx
    padded = jnp.concatenate([conv_prev, s], axis=1)
    y = conv_w[0] * padded[:, 0:t]
    for j in range(1, CONV_W):
        y = y + conv_w[j] * padded[:, j:j + t]
    c_out = cb * y
    conv_state = padded[:, t:]
    branches = jnp.stack([a_out, r_out, c_out], axis=2)
    proj = jnp.einsum('btnc,ncd->btnd', branches, w_br)
    merged = jnp.sum(jax.nn.sigmoid(gt.reshape(bsz, t, N_BRANCH, D_MODEL)) * proj, axis=2)
    return merged @ w_o, ak, av, r_state, conv_state


def _swiglu(u, w_in, w_out):
    a, b = jnp.split(u @ w_in, 2, axis=-1)
    return (jax.nn.silu(a) * b) @ w_out


def _run(x, c, pos0, state_conv, state_ret, cache_k, cache_v, page_table,
         w_ada, b_ada, w_in, conv_w, w_br, w_o, ln1_g, ln1_b, w_ffn_in, w_ffn_out, ln2_g, ln2_b):
    bsz, t, _ = x.shape
    pos = pos0 + jnp.arange(t, dtype=jnp.int32)
    cond = jax.nn.silu(c)
    ks, vs, rss, css = [], [], [], []
    for l in range(DEPTH):
        ada = (cond @ w_ada[l] + b_ada[l])[:, None, :]
        sh1, sc1, g1, sh2, sc2, g2 = jnp.split(ada, 6, axis=-1)
        if cache_k is None:
            conv_prev = jnp.zeros((bsz, CONV_W - 1, CONV_DIM), x.dtype)
            ret_prev = jnp.zeros((bsz, RET_HEADS, RET_DK, RET_DV), x.dtype)
            k_past = None
            v_past = None
        else:
            conv_prev = state_conv[l]
            ret_prev = state_ret[l]
            k_past = cache_k[l][page_table].reshape(bsz, -1, ATT_HEADS, ATT_HD)
            v_past = cache_v[l][page_table].reshape(bsz, -1, ATT_HEADS, ATT_HD)
        u = x * (1.0 + sc1) + sh1
        m, k_new, v_new, r_new, c_new = _mixer(u, pos, pos0, conv_prev, ret_prev, k_past, v_past,
                                               w_in[l], conv_w[l], w_br[l], w_o[l])
        x = _layer_norm(ALPHA * x + g1 * m, ln1_g[l], ln1_b[l])
        u = x * (1.0 + sc2) + sh2
        x = _layer_norm(ALPHA * x + g2 * _swiglu(u, w_ffn_in[l], w_ffn_out[l]), ln2_g[l], ln2_b[l])
        ks.append(k_new)
        vs.append(v_new)
        rss.append(r_new)
        css.append(c_new)
    return x, jnp.stack(ks), jnp.stack(vs), jnp.stack(rss), jnp.stack(css)


def setup_inputs(seed: int = 0) -> dict:
    key = jax.random.key(seed)
    ks = jax.random.split(key, 24)
    f32 = jnp.float32
    n_pages = PAST_LEN // PAGE_SIZE
    n_used = DEC_BATCH * n_pages
    n_pool = n_used + n_used // 4
    page_table = jax.random.permutation(ks[0], n_pool)[:n_used].reshape(DEC_BATCH, n_pages).astype(jnp.int32)

    def nrm(k, shape, s=1.0):
        return jax.random.normal(k, shape, f32) * s

    col_scale = jnp.concatenate([jnp.full((sz,), BETA if i in VALUE_SLOTS else 1.0, f32) for i, sz in enumerate(IN_SIZES)])
    return {
        'x_prompt': nrm(ks[1], (BATCH, SEQ, D_MODEL)),
        'x_sample': nrm(ks[2], (DEC_BATCH, DEC_SEQ, D_MODEL)),
        'cache_k': nrm(ks[3], (DEPTH, n_pool, PAGE_SIZE, ATT_HEADS, ATT_HD)),
        'cache_v': nrm(ks[4], (DEPTH, n_pool, PAGE_SIZE, ATT_HEADS, ATT_HD)),
        'state_ret': nrm(ks[5], (DEPTH, DEC_BATCH, RET_HEADS, RET_DK, RET_DV)),
        'state_conv': nrm(ks[6], (DEPTH, DEC_BATCH, CONV_W - 1, CONV_DIM)),
        'page_table': page_table,
        'c_prompt': nrm(ks[7], (BATCH, D_MODEL)),
        'c_sample': nrm(ks[8], (DEC_BATCH, D_MODEL)),
        'w_ada': nrm(ks[9], (DEPTH, D_MODEL, 6 * D_MODEL), D_MODEL ** -0.5),
        'b_ada': nrm(ks[10], (DEPTH, 6 * D_MODEL), 0.02),
        'w_in': nrm(ks[11], (DEPTH, D_MODEL, IN_DIM), D_MODEL ** -0.5) * col_scale,
        'conv_w': nrm(ks[12], (DEPTH, CONV_W, CONV_DIM), CONV_W ** -0.5),
        'w_br': nrm(ks[13], (DEPTH, N_BRANCH, BRANCH_DIM, D_MODEL), BRANCH_DIM ** -0.5 * BETA),
        'w_o': nrm(ks[14], (DEPTH, D_MODEL, D_MODEL), D_MODEL ** -0.5 * BETA),
        'ln1_g': 1.0 + nrm(ks[15], (DEPTH, D_MODEL), 0.02),
        'ln1_b': nrm(ks[16], (DEPTH, D_MODEL), 0.02),
        'w_ffn_in': nrm(ks[17], (DEPTH, D_MODEL, 2 * D_FF), D_MODEL ** -0.5 * BETA),
        'w_ffn_out': nrm(ks[18], (DEPTH, D_FF, D_MODEL), D_FF ** -0.5 * BETA),
        'ln2_g': 1.0 + nrm(ks[19], (DEPTH, D_MODEL), 0.02),
        'ln2_b': nrm(ks[20], (DEPTH, D_MODEL), 0.02),
    }


def reference(x_prompt, x_sample, cache_k, cache_v, state_ret, state_conv, page_table, c_prompt, c_sample,
              w_ada, b_ada, w_in, conv_w, w_br, w_o, ln1_g, ln1_b, w_ffn_in, w_ffn_out, ln2_g, ln2_b):
    y_prompt, k_prompt, v_prompt, ret_prompt, conv_prompt = _run(
        x_prompt, c_prompt, 0, None, None, None, None, None,
        w_ada, b_ada, w_in, conv_w, w_br, w_o, ln1_g, ln1_b, w_ffn_in, w_ffn_out, ln2_g, ln2_b)
    y_sample, k_sample, v_sample, ret_sample, conv_sample = _run(
        x_sample, c_sample, PAST_LEN, state_conv, state_ret, cache_k, cache_v, page_table,
        w_ada, b_ada, w_in, conv_w, w_br, w_o, ln1_g, ln1_b, w_ffn_in, w_ffn_out, ln2_g, ln2_b)
    return (y_prompt, y_sample, k_prompt, v_prompt, k_sample, v_sample, ret_prompt, ret_sample, conv_prompt, conv_sample)
```

```python
import functools

import jax
import jax.numpy as jnp
from jax import lax
from jax.experimental import pallas as pl
from jax.experimental.pallas import tpu as pltpu

F32 = jnp.float32
BF16 = jnp.bfloat16

D_MODEL = 1024
ATT_HEADS = 8
ATT_HD = 64
ATT_DIM = ATT_HEADS * ATT_HD
ROPE_DIM = ATT_HD // 4
ROPE_THETA = 500000.0
MOBA_BLOCK = 256
MOBA_TOPK = 3
RET_HEADS = 4
RET_DK = 64
RET_DV = 128
RET_QK_DIM = RET_HEADS * RET_DK
RET_V_DIM = RET_HEADS * RET_DV
RET_THETA = 10000.0
RET_CHUNK = 128
CONV_DIM = 512
CONV_W = 3
N_BRANCH = 3
D_FF = 2816
LN_EPS = 1e-5
N_GATE = N_BRANCH * D_MODEL
N_MIX = 3 * ATT_DIM + 2 * RET_QK_DIM + 2 * RET_V_DIM + 3 * CONV_DIM
IN_DIM = N_MIX + N_GATE

OFF_GATE = 0
OFF_AQ = N_GATE
OFF_AK = OFF_AQ + ATT_DIM
OFF_AV = OFF_AK + ATT_DIM
OFF_RQ = OFF_AV + ATT_DIM
OFF_RK = OFF_RQ + RET_QK_DIM
OFF_RV = OFF_RK + RET_QK_DIM
OFF_RG = OFF_RV + RET_V_DIM
OFF_CB = OFF_RG + RET_V_DIM
OFF_CC = OFF_CB + CONV_DIM
OFF_CX = OFF_CC + CONV_DIM

LANES = 128
VMEM_LIMIT = 48 * 1024 * 1024
NEG_INF = float("-inf")


def _params(semantics, vmem=VMEM_LIMIT):
    return pltpu.CompilerParams(dimension_semantics=semantics, vmem_limit_bytes=vmem)


def _ln(y):
    mu = jnp.mean(y, axis=-1, keepdims=True)
    yc = y - mu
    return yc * lax.rsqrt(jnp.mean(yc * yc, axis=-1, keepdims=True) + LN_EPS)


def _sigmoid(x):
    return 1.0 / (1.0 + jnp.exp(-x))


def _silu(x):
    return x * _sigmoid(x)


def _mod_spec(mod, tm, ngrid):
    per_row = mod.shape[1] != 1
    rows = tm if per_row else 1
    if ngrid == 2:
        return pl.BlockSpec((None, rows, mod.shape[2]), lambda b, i: (b, i if per_row else 0, 0))
    return pl.BlockSpec((None, rows, mod.shape[2]), lambda b, i, j: (b, i if per_row else 0, 0))


def _ada_kernel(c_ref, w_ref, b_ref, o_ref):
    cond = _silu(c_ref[...]).astype(BF16)
    o_ref[...] = jnp.dot(cond, w_ref[...].astype(BF16), preferred_element_type=F32) + b_ref[...]


def _ada(c, w_ada, b_ada):
    depth, d, n = w_ada.shape
    rows = c.shape[0]
    tn = 1536
    return pl.pallas_call(
        _ada_kernel,
        grid=(depth, n // tn),
        in_specs=[pl.BlockSpec((rows, d), lambda l, j: (0, 0)),
                  pl.BlockSpec((None, d, tn), lambda l, j: (l, 0, j)),
                  pl.BlockSpec((None, 1, tn), lambda l, j: (l, 0, j))],
        out_specs=pl.BlockSpec((None, rows, tn), lambda l, j: (l, 0, j)),
        out_shape=jax.ShapeDtypeStruct((depth, rows, n), F32),
        compiler_params=_params(("arbitrary", "arbitrary")),
    )(c, w_ada, b_ada.reshape(depth, 1, n))


def _inproj_kernel(x_ref, sc_ref, sh_ref, w_ref, o_ref, u_ref):
    @pl.when(pl.program_id(2) == 0)
    def _():
        u_ref[...] = (x_ref[...] * (1.0 + sc_ref[...]) + sh_ref[...]).astype(BF16)

    o_ref[...] = jnp.dot(u_ref[...], w_ref[...], preferred_element_type=F32)


def _inproj(x, sc, sh, w):
    bsz, t, d = x.shape
    n = w.shape[1]
    tm = min(t, 512)
    tn = 1536
    return pl.pallas_call(
        _inproj_kernel,
        grid=(bsz, t // tm, n // tn),
        in_specs=[pl.BlockSpec((None, tm, d), lambda b, i, j: (b, i, 0)),
                  _mod_spec(sc, tm, 3), _mod_spec(sh, tm, 3),
                  pl.BlockSpec((d, tn), lambda b, i, j: (0, j))],
        out_specs=pl.BlockSpec((None, tm, tn), lambda b, i, j: (b, i, j)),
        out_shape=jax.ShapeDtypeStruct((bsz, t, n), F32),
        scratch_shapes=[pltpu.VMEM((tm, d), BF16)],
        compiler_params=_params(("arbitrary", "arbitrary", "arbitrary")),
    )(x, sc, sh, w)


def _rope_att(x, ca, cm, cp):
    return x * ca + pltpu.roll(x, LANES - ROPE_DIM // 2, 1) * cm + pltpu.roll(x, ROPE_DIM // 2, 1) * cp


def _attn_prep_kernel(q_ref, k_ref, v_ref, ca_ref, cm_ref, cp_ref, *out_refs, head_major):
    ca, cm, cp = ca_ref[...], cm_ref[...], cp_ref[...]
    scale = ATT_HD ** -0.5
    if head_major:
        qh_ref, kh_ref, vh_ref, ko_ref, vo_ref = out_refs
    else:
        qo_ref, ko_ref, vo_ref = out_refs
    vo_ref[...] = v_ref[...]
    for c in range(ATT_DIM // LANES):
        sl = slice(c * LANES, (c + 1) * LANES)
        q = _rope_att(q_ref[:, sl], ca, cm, cp) * scale
        k = _rope_att(k_ref[:, sl], ca, cm, cp)
        ko_ref[:, sl] = k
        if head_major:
            v = v_ref[:, sl]
            for hh in range(2):
                hs = slice(hh * ATT_HD, (hh + 1) * ATT_HD)
                qh_ref[2 * c + hh] = q[:, hs]
                kh_ref[2 * c + hh] = k[:, hs].astype(BF16)
                vh_ref[2 * c + hh] = v[:, hs].astype(BF16)
        else:
            qo_ref[:, sl] = q


def _attn_prep(z, tabs, head_major):
    bsz, t, _ = z.shape
    tm = min(t, 512)
    cq, ck, cv = OFF_AQ // ATT_DIM, OFF_AK // ATT_DIM, OFF_AV // ATT_DIM
    tok = pl.BlockSpec((None, tm, ATT_DIM), lambda b, i: (b, i, 0))
    hm = pl.BlockSpec((None, ATT_HEADS, tm, ATT_HD), lambda b, i: (b, 0, i, 0))
    tab = pl.BlockSpec((tm, LANES), lambda b, i: (i, 0))
    tok_shape = jax.ShapeDtypeStruct((bsz, t, ATT_DIM), F32)
    if head_major:
        out_specs = [hm, hm, hm, tok, tok]
        out_shape = [jax.ShapeDtypeStruct((bsz, ATT_HEADS, t, ATT_HD), F32),
                     jax.ShapeDtypeStruct((bsz, ATT_HEADS, t, ATT_HD), BF16),
                     jax.ShapeDtypeStruct((bsz, ATT_HEADS, t, ATT_HD), BF16), tok_shape, tok_shape]
    else:
        out_specs = [tok, tok, tok]
        out_shape = [tok_shape, tok_shape, tok_shape]
    return pl.pallas_call(
        functools.partial(_attn_prep_kernel, head_major=head_major),
        grid=(bsz, t // tm),
        in_specs=[pl.BlockSpec((None, tm, ATT_DIM), lambda b, i: (b, i, cq)),
                  pl.BlockSpec((None, tm, ATT_DIM), lambda b, i: (b, i, ck)),
                  pl.BlockSpec((None, tm, ATT_DIM), lambda b, i: (b, i, cv)),
                  tab, tab, tab],
        out_specs=out_specs,
        out_shape=out_shape,
        compiler_params=_params(("arbitrary", "arbitrary")),
    )(z, z, z, *tabs)


def _kmean_kernel(k_ref, o_ref, *, nblk):
    for j in range(nblk):
        o_ref[j:j + 1, :] = jnp.mean(k_ref[j * MOBA_BLOCK:(j + 1) * MOBA_BLOCK, :], axis=0, keepdims=True)


def _kmean(k_tok):
    bsz, t, _ = k_tok.shape
    rows = min(t, 8 * MOBA_BLOCK)
    nblk = rows // MOBA_BLOCK
    return pl.pallas_call(
        functools.partial(_kmean_kernel, nblk=nblk),
        grid=(bsz, t // rows),
        in_specs=[pl.BlockSpec((None, rows, ATT_DIM), lambda b, i: (b, i, 0))],
        out_specs=pl.BlockSpec((None, nblk, ATT_DIM), lambda b, i: (b, i, 0)),
        out_shape=jax.ShapeDtypeStruct((bsz, t // MOBA_BLOCK, ATT_DIM), F32),
        compiler_params=_params(("arbitrary", "arbitrary")),
    )(k_tok)


def _moba_kernel(q_ref, k_ref, v_ref, km_ref, o_ref, m_ref, l_ref, acc_ref, *, nb):
    i = pl.program_id(2)
    blk = MOBA_BLOCK
    row = lax.broadcasted_iota(jnp.int32, (blk, blk), 0)
    col = lax.broadcasted_iota(jnp.int32, (blk, blk), 1)
    bid = lax.broadcasted_iota(jnp.int32, (blk, nb), 1)
    nt = (((1,), (1,)), ((), ()))
    for hh in range(2):
        q = q_ref[hh]
        qb = q.astype(BF16)
        gate = lax.dot_general(q, km_ref[hh], nt, precision=lax.Precision.HIGHEST, preferred_element_type=F32)
        gate = jnp.where(bid < i, gate, NEG_INF)
        picks = []
        for r in range(MOBA_TOPK):
            mx = jnp.max(gate, axis=1, keepdims=True)
            ix = jnp.min(jnp.where(gate == mx, bid, nb), axis=1, keepdims=True)
            gate = jnp.where(bid == ix, NEG_INF, gate)
            picks.append(jnp.where(r < i, ix, -1))
        start = pl.multiple_of(i * blk, blk)
        s = lax.dot_general(qb, k_ref[hh, pl.ds(start, blk), :], nt, preferred_element_type=F32)
        s = jnp.where(col <= row, s, NEG_INF)
        m0 = jnp.max(s, axis=1, keepdims=True)
        p = jnp.exp(s - m0)
        m_ref[...] = m0
        l_ref[...] = jnp.sum(p, axis=1, keepdims=True)
        acc_ref[...] = jnp.dot(p.astype(BF16), v_ref[hh, pl.ds(start, blk), :], preferred_element_type=F32)

        def body(j, carry):
            off = pl.multiple_of(j * blk, blk)
            live = (picks[0] == j) | (picks[1] == j) | (picks[2] == j)
            sj = lax.dot_general(qb, k_ref[hh, pl.ds(off, blk), :], nt, preferred_element_type=F32)
            sj = jnp.where(live, sj, NEG_INF)
            m_old = m_ref[...]
            m_new = jnp.maximum(m_old, jnp.max(sj, axis=1, keepdims=True))
            alpha = jnp.exp(m_old - m_new)
            pj = jnp.exp(sj - m_new)
            l_ref[...] = alpha * l_ref[...] + jnp.sum(pj, axis=1, keepdims=True)
            acc_ref[...] = alpha * acc_ref[...] + jnp.dot(pj.astype(BF16), v_ref[hh, pl.ds(off, blk), :],
                                                          preferred_element_type=F32)
            m_ref[...] = m_new
            return carry

        lax.fori_loop(0, i, body, 0)
        o_ref[:, hh * ATT_HD:(hh + 1) * ATT_HD] = acc_ref[...] / l_ref[...]


def _moba(q_hm, k_hm, v_hm, kmean_hm):
    bsz, nh, t, hd = q_hm.shape
    nb = t // MOBA_BLOCK
    return pl.pallas_call(
        functools.partial(_moba_kernel, nb=nb),
        grid=(bsz, nh // 2, nb),
        in_specs=[pl.BlockSpec((None, 2, MOBA_BLOCK, hd), lambda b, h, i: (b, h, i, 0)),
                  pl.BlockSpec((None, 2, t, hd), lambda b, h, i: (b, h, 0, 0)),
                  pl.BlockSpec((None, 2, t, hd), lambda b, h, i: (b, h, 0, 0)),
                  pl.BlockSpec((None, 2, nb, hd), lambda b, h, i: (b, h, 0, 0))],
        out_specs=pl.BlockSpec((None, MOBA_BLOCK, 2 * hd), lambda b, h, i: (b, i, h)),
        out_shape=jax.ShapeDtypeStruct((bsz, t, nh * hd), F32),
        scratch_shapes=[pltpu.VMEM((MOBA_BLOCK, 1), F32), pltpu.VMEM((MOBA_BLOCK, 1), F32),
                        pltpu.VMEM((MOBA_BLOCK, hd), F32)],
        compiler_params=_params(("arbitrary", "arbitrary", "arbitrary")),
    )(q_hm, k_hm, v_hm, kmean_hm)


def _rope_ret(x, c, s):
    w = x.shape[1]
    d = lax.broadcasted_iota(jnp.int32, x.shape, 1) % RET_DK
    half = RET_DK // 2
    partner = jnp.where(d < half, pltpu.roll(x, w - half, 1), pltpu.roll(x, half, 1))
    return x * c + partner * s


def _ret_decay(h):
    return 1.0 - 2.0 ** (-5.0 - h)


def _retention_kernel(q_ref, k_ref, v_ref, g_ref, c_ref, s_ref, dm_ref, qd_ref, kd_ref, o_ref, st_ref, state, *, tm):
    ch = RET_CHUNK

    @pl.when(pl.program_id(1) == 0)
    def _():
        state[...] = jnp.zeros_like(state)

    q = _rope_ret(q_ref[...], c_ref[...], s_ref[...])
    k = _rope_ret(k_ref[...], c_ref[...], s_ref[...]) * (RET_DK ** -0.5)
    kt = k.T
    for c in range(tm // ch):
        rows = slice(c * ch, (c + 1) * ch)
        for h in range(RET_HEADS):
            qc = q[rows, h * RET_DK:(h + 1) * RET_DK]
            ktc = kt[h * RET_DK:(h + 1) * RET_DK, rows]
            vc = v_ref[rows, h * RET_DV:(h + 1) * RET_DV].astype(BF16)
            s_old = state[h]
            inner = jnp.dot(qc.astype(BF16), ktc.astype(BF16), preferred_element_type=F32) * dm_ref[h]
            o = jnp.dot(inner.astype(BF16), vc, preferred_element_type=F32)
            o = o + jnp.dot((qc * qd_ref[h]).astype(BF16), s_old.astype(BF16), preferred_element_type=F32)
            c_dec = _ret_decay(h) ** ch
            state[h] = c_dec * s_old + jnp.dot((ktc * kd_ref[h]).astype(BF16), vc, preferred_element_type=F32)
            gate = g_ref[rows, h * RET_DV:(h + 1) * RET_DV]
            o_ref[rows, h * RET_DV:(h + 1) * RET_DV] = _ln(o) * _silu(gate)
    st_ref[...] = state[...]


def _retention(z, tabs, consts):
    bsz, t, _ = z.shape
    tm = min(t, 512)
    dmat, qdec, kdec = consts
    const3 = lambda a: pl.BlockSpec(a.shape, lambda b, i: (0, 0, 0))
    tab = pl.BlockSpec((tm, RET_QK_DIM), lambda b, i: (i, 0))
    return pl.pallas_call(
        functools.partial(_retention_kernel, tm=tm),
        grid=(bsz, t // tm),
        in_specs=[pl.BlockSpec((None, tm, RET_QK_DIM), lambda b, i: (b, i, OFF_RQ // RET_QK_DIM)),
                  pl.BlockSpec((None, tm, RET_QK_DIM), lambda b, i: (b, i, OFF_RK // RET_QK_DIM)),
                  pl.BlockSpec((None, tm, RET_V_DIM), lambda b, i: (b, i, OFF_RV // RET_V_DIM)),
                  pl.BlockSpec((None, tm, RET_V_DIM), lambda b, i: (b, i, OFF_RG // RET_V_DIM)),
                  tab, tab, const3(dmat), const3(qdec), const3(kdec)],
        out_specs=[pl.BlockSpec((None, tm, RET_V_DIM), lambda b, i: (b, i, 0)),
                   pl.BlockSpec((None, RET_HEADS, RET_DK, RET_DV), lambda b, i: (b, 0, 0, 0))],
        out_shape=[jax.ShapeDtypeStruct((bsz, t, RET_V_DIM), F32),
                   jax.ShapeDtypeStruct((bsz, RET_HEADS, RET_DK, RET_DV), F32)],
        scratch_shapes=[pltpu.VMEM((RET_HEADS, RET_DK, RET_DV), F32)],
        compiler_params=_params(("arbitrary", "arbitrary")),
    )(z, z, z, z, *tabs, dmat, qdec, kdec)


def _ret_consts():
    ch = RET_CHUNK
    log_g = jnp.log(1.0 - 2.0 ** (-5.0 - jnp.arange(RET_HEADS, dtype=F32)))
    i = jnp.arange(ch, dtype=F32)
    diff = i[:, None] - i[None, :]
    dmat = jnp.where(diff >= 0, jnp.exp(jnp.maximum(diff, 0.0)[None] * log_g[:, None, None]), 0.0)
    qdec = jnp.exp((i + 1.0)[None, :] * log_g[:, None])[:, :, None]
    kdec = jnp.exp((ch - 1.0 - i)[None, :] * log_g[:, None])[:, None, :]
    return dmat, qdec, kdec


def _row_to_col(row):
    n = row.shape[1]
    eye = lax.broadcasted_iota(jnp.int32, (n, n), 0) == lax.broadcasted_iota(jnp.int32, (n, n), 1)
    return jnp.sum(jnp.where(eye, jnp.broadcast_to(row, (n, n)), 0.0), axis=1, keepdims=True)


def _ret_step_kernel(q_ref, k_ref, v_ref, g_ref, c_ref, s_ref, st_ref, o_ref, sn_ref):
    q = _rope_ret(q_ref[...], c_ref[...], s_ref[...])
    k = _rope_ret(k_ref[...], c_ref[...], s_ref[...]) * (RET_DK ** -0.5)
    for h in range(RET_HEADS):
        g = _ret_decay(h)
        qrow = q[:, h * RET_DK:(h + 1) * RET_DK]
        krow = k[:, h * RET_DK:(h + 1) * RET_DK]
        qcol = _row_to_col(qrow)
        kcol = _row_to_col(krow)
        v = v_ref[:, h * RET_DV:(h + 1) * RET_DV]
        s_old = st_ref[h]
        inner = jnp.sum(qrow * krow, axis=1, keepdims=True)
        o = inner * v + jnp.sum((qcol * g) * s_old, axis=0, keepdims=True)
        sn_ref[h] = g * s_old + kcol * v
        o_ref[:, h * RET_DV:(h + 1) * RET_DV] = _ln(o) * _silu(g_ref[:, h * RET_DV:(h + 1) * RET_DV])


def _ret_step(z, tabs, state):
    n = z.shape[0]
    tab = pl.BlockSpec((None, 1, RET_QK_DIM), lambda b: (b, 0, 0))
    st = pl.BlockSpec((None, RET_HEADS, RET_DK, RET_DV), lambda b: (b, 0, 0, 0))
    return pl.pallas_call(
        _ret_step_kernel,
        grid=(n,),
        in_specs=[pl.BlockSpec((None, 1, RET_QK_DIM), lambda b: (b, 0, OFF_RQ // RET_QK_DIM)),
                  pl.BlockSpec((None, 1, RET_QK_DIM), lambda b: (b, 0, OFF_RK // RET_QK_DIM)),
                  pl.BlockSpec((None, 1, RET_V_DIM), lambda b: (b, 0, OFF_RV // RET_V_DIM)),
                  pl.BlockSpec((None, 1, RET_V_DIM), lambda b: (b, 0, OFF_RG // RET_V_DIM)),
                  tab, tab, st],
        out_specs=[pl.BlockSpec((None, 1, RET_V_DIM), lambda b: (b, 0, 0)), st],
        out_shape=[jax.ShapeDtypeStruct((n, 1, RET_V_DIM), F32),
                   jax.ShapeDtypeStruct(state.shape, F32)],
        compiler_params=_params(("arbitrary",)),
    )(z, z, z, z, *tabs, state)


def _merge_kernel(*refs, tm, seq_mode, alpha):
    if seq_mode:
        (x_ref, g1_ref, a_ref, r_ref, cb_ref, cc_ref, cx_ref, gt0_ref, gt1_ref, gt2_ref, cw_ref, wbr_ref, wo_ref,
         lg_ref, lb_ref, o_ref, cs_ref, carry) = refs
    else:
        (x_ref, g1_ref, a_ref, r_ref, cb_ref, cc_ref, cx_ref, gt0_ref, gt1_ref, gt2_ref, cw_ref, wbr_ref, wo_ref,
         lg_ref, lb_ref, p0_ref, p1_ref, o_ref, cs_ref) = refs
    s = cc_ref[...] * cx_ref[...]
    if seq_mode:
        @pl.when(pl.program_id(1) == 0)
        def _():
            carry[...] = jnp.zeros_like(carry)

        rid = lax.broadcasted_iota(jnp.int32, s.shape, 0)
        prev0, prev1 = carry[0:1, :], carry[1:2, :]
        s1 = jnp.where(rid == 0, prev1, pltpu.roll(s, 1, 0))
        s2 = jnp.where(rid == 0, prev0, jnp.where(rid == 1, prev1, pltpu.roll(s, 2, 0)))
        carry[...] = s[tm - 2:tm, :]
        cs_ref[...] = s[tm - 2:tm, :]
    else:
        s2, s1 = p0_ref[...], p1_ref[...]
        cs_ref[...] = s
    y = cw_ref[0:1, :] * s2 + cw_ref[1:2, :] * s1 + cw_ref[2:3, :] * s
    c_out = cb_ref[...] * y
    merged = _sigmoid(gt0_ref[...]) * jnp.dot(a_ref[...].astype(BF16), wbr_ref[0], preferred_element_type=F32)
    merged = merged + _sigmoid(gt1_ref[...]) * jnp.dot(r_ref[...].astype(BF16), wbr_ref[1],
                                                       preferred_element_type=F32)
    merged = merged + _sigmoid(gt2_ref[...]) * jnp.dot(c_out.astype(BF16), wbr_ref[2], preferred_element_type=F32)
    m = jnp.dot(merged.astype(BF16), wo_ref[...], preferred_element_type=F32)
    o_ref[...] = _ln(alpha * x_ref[...] + g1_ref[...] * m) * lg_ref[...] + lb_ref[...]


def _merge(x, g1, a_out, r_out, z, conv_w, w_br, w_o, ln_g, ln_b, alpha, conv_prev=None):
    bsz, t, d = x.shape
    seq_mode = conv_prev is None
    tm = min(t, 256)
    tok512 = lambda cblk: pl.BlockSpec((None, tm, CONV_DIM), lambda b, i: (b, i, cblk))
    gate = lambda n: pl.BlockSpec((None, tm, d), lambda b, i: (b, i, OFF_GATE // d + n))
    full2 = lambda a: pl.BlockSpec(a.shape, lambda b, i: (0, 0))
    in_specs = [pl.BlockSpec((None, tm, d), lambda b, i: (b, i, 0)), _mod_spec(g1, tm, 2),
                tok512(0), tok512(0), tok512(OFF_CB // CONV_DIM), tok512(OFF_CC // CONV_DIM),
                tok512(OFF_CX // CONV_DIM), gate(0), gate(1), gate(2),
                full2(conv_w), pl.BlockSpec(w_br.shape, lambda b, i: (0, 0, 0)), full2(w_o),
                full2(ln_g), full2(ln_b)]
    args = [x, g1, a_out, r_out, z, z, z, z, z, z, conv_w, w_br, w_o, ln_g, ln_b]
    scratch = []
    if seq_mode:
        cs_spec = pl.BlockSpec((None, CONV_W - 1, CONV_DIM), lambda b, i: (b, 0, 0))
        cs_shape = jax.ShapeDtypeStruct((bsz, CONV_W - 1, CONV_DIM), F32)
        scratch = [pltpu.VMEM((CONV_W - 1, CONV_DIM), F32)]
    else:
        in_specs += [tok512(0), tok512(0)]
        args += list(conv_prev)
        cs_spec = tok512(0)
        cs_shape = jax.ShapeDtypeStruct((bsz, t, CONV_DIM), F32)
    return pl.pallas_call(
        functools.partial(_merge_kernel, tm=tm, seq_mode=seq_mode, alpha=alpha),
        grid=(bsz, t // tm),
        in_specs=in_specs,
        out_specs=[pl.BlockSpec((None, tm, d), lambda b, i: (b, i, 0)), cs_spec],
        out_shape=[jax.ShapeDtypeStruct((bsz, t, d), F32), cs_shape],
        scratch_shapes=scratch,
        compiler_params=_params(("arbitrary", "arbitrary")),
    )(*args)


def _ffn_kernel(x_ref, sc_ref, sh_ref, g2_ref, wa_ref, wb_ref, wo_ref, lg_ref, lb_ref, o_ref, u_ref, acc_ref, *,
                alpha):
    f = pl.program_id(2)

    @pl.when(f == 0)
    def _():
        u_ref[...] = (x_ref[...] * (1.0 + sc_ref[...]) + sh_ref[...]).astype(BF16)
        acc_ref[...] = jnp.zeros_like(acc_ref)

    u = u_ref[...]
    a = jnp.dot(u, wa_ref[...], preferred_element_type=F32)
    b = jnp.dot(u, wb_ref[...], preferred_element_type=F32)
    acc_ref[...] += jnp.dot((_silu(a) * b).astype(BF16), wo_ref[...], preferred_element_type=F32)

    @pl.when(f == pl.num_programs(2) - 1)
    def _():
        o_ref[...] = _ln(alpha * x_ref[...] + g2_ref[...] * acc_ref[...]) * lg_ref[...] + lb_ref[...]


def _ffn(x, sc, sh, g2, w_in, w_out, ln_g, ln_b, alpha):
    bsz, t, d = x.shape
    tm = min(t, 512)
    tf = D_FF // 2
    nf = D_FF // tf
    full2 = lambda a: pl.BlockSpec(a.shape, lambda b, i, f: (0, 0))
    return pl.pallas_call(
        functools.partial(_ffn_kernel, alpha=alpha),
        grid=(bsz, t // tm, nf),
        in_specs=[pl.BlockSpec((None, tm, d), lambda b, i, f: (b, i, 0)),
                  _mod_spec(sc, tm, 3), _mod_spec(sh, tm, 3), _mod_spec(g2, tm, 3),
                  pl.BlockSpec((d, tf), lambda b, i, f: (0, f)),
                  pl.BlockSpec((d, tf), lambda b, i, f: (0, nf + f)),
                  pl.BlockSpec((tf, d), lambda b, i, f: (f, 0)),
                  full2(ln_g), full2(ln_b)],
        out_specs=pl.BlockSpec((None, tm, d), lambda b, i, f: (b, i, 0)),
        out_shape=jax.ShapeDtypeStruct((bsz, t, d), F32),
        scratch_shapes=[pltpu.VMEM((tm, d), BF16), pltpu.VMEM((tm, d), F32)],
        compiler_params=_params(("arbitrary", "arbitrary", "arbitrary")),
    )(x, sc, sh, g2, w_in, w_in, w_out, ln_g, ln_b)


def _dec_scan_kernel(pt_ref, q_ref, ck_ref, o_ref, buf, ksum, sem, *, page_base, n_pages, cp):
    b = pl.program_id(0)
    n_chunks = n_pages // cp
    cb = cp // 2
    page_rows = MOBA_BLOCK // 2

    def copies(c, slot):
        out = []
        for p in range(cp):
            page = pt_ref[b * n_pages + c * cp + p] + page_base
            out.append(pltpu.make_async_copy(
                ck_ref.at[page], buf.at[slot, p // 2, pl.ds((p % 2) * page_rows, page_rows), :], sem.at[slot]))
        return out

    for cpy in copies(0, 0):
        cpy.start()

    def body(c, carry):
        slot = c % 2

        @pl.when(c + 1 < n_chunks)
        def _():
            for cpy in copies(c + 1, 1 - slot):
                cpy.start()

        for cpy in copies(c, slot):
            cpy.wait()
        ksum[pl.ds(pl.multiple_of(c * cb, cb), cb), :] = jnp.sum(buf[slot], axis=1)
        return carry

    lax.fori_loop(0, n_chunks, body, 0)

    nbp = n_pages // 2
    prod = (ksum[...] * (1.0 / MOBA_BLOCK)) * q_ref[...]
    bid = lax.broadcasted_iota(jnp.int32, (nbp, 1), 0)
    lane = lax.broadcasted_iota(jnp.int32, (1, LANES), 1)
    res = jnp.zeros((1, LANES), jnp.int32)
    for h in range(ATT_HEADS):
        gate = jnp.sum(prod[:, h * ATT_HD:(h + 1) * ATT_HD], axis=1, keepdims=True)
        for r in range(min(MOBA_TOPK, nbp)):
            mx = jnp.max(gate, axis=0, keepdims=True)
            ix = jnp.min(jnp.where(gate == mx, bid, nbp), axis=0, keepdims=True)
            gate = jnp.where(bid == ix, NEG_INF, gate)
            res = jnp.where(lane == h * MOBA_TOPK + r, ix, res)
    o_ref[...] = res


def _dec_scan(page_table, q_tok, cache_k_pages, page_base):
    n, n_pages = page_table.shape
    page_rows, width = cache_k_pages.shape[1], cache_k_pages.shape[2]
    assert page_rows * 2 == MOBA_BLOCK
    cp = min(16, n_pages)
    nbp = n_pages // 2
    grid_spec = pltpu.PrefetchScalarGridSpec(
        num_scalar_prefetch=1,
        grid=(n,),
        in_specs=[pl.BlockSpec((None, 1, width), lambda b, pt: (b, 0, 0)),
                  pl.BlockSpec(memory_space=pl.ANY)],
        out_specs=pl.BlockSpec((None, 1, LANES), lambda b, pt: (b, 0, 0)),
        scratch_shapes=[pltpu.VMEM((2, cp // 2, MOBA_BLOCK, width), F32),
                        pltpu.VMEM((nbp, width), F32),
                        pltpu.SemaphoreType.DMA((2,))],
    )
    return pl.pallas_call(
        functools.partial(_dec_scan_kernel, page_base=page_base, n_pages=n_pages, cp=cp),
        grid_spec=grid_spec,
        out_shape=jax.ShapeDtypeStruct((n, 1, LANES), jnp.int32),
        compiler_params=_params(("arbitrary",)),
    )(page_table.reshape(-1), q_tok, cache_k_pages)


def _dec_attend_kernel(pt_ref, pick_ref, q_ref, kn_ref, vn_ref, ck_ref, cv_ref, o_ref, kb, vb, sem, *,
                       page_base, n_pages, n_sel):
    b = pl.program_id(0)
    page_rows = MOBA_BLOCK // 2

    def copies():
        out = []
        for h in range(ATT_HEADS):
            for r in range(n_sel):
                blk = pick_ref[b * LANES + h * MOBA_TOPK + r]
                for p in range(2):
                    page = pt_ref[b * n_pages + 2 * blk + p] + page_base
                    dst = pl.ds(r * MOBA_BLOCK + p * page_rows, page_rows)
                    cols = pl.ds((h // 2) * LANES, LANES)
                    out.append(pltpu.make_async_copy(ck_ref.at[page, :, cols], kb.at[h, dst, :], sem.at[0]))
                    out.append(pltpu.make_async_copy(cv_ref.at[page, :, cols], vb.at[h, dst, :], sem.at[1]))
        return out

    cps = copies()
    for cpy in cps:
        cpy.start()
    for cpy in cps:
        cpy.wait()

    for h in range(ATT_HEADS):
        hs = slice(h * ATT_HD, (h + 1) * ATT_HD)
        q = q_ref[:, hs]
        pair = slice((h % 2) * ATT_HD, (h % 2 + 1) * ATT_HD)
        s = jnp.sum(kb[h, :, pair] * q, axis=1, keepdims=True)
        s_new = jnp.sum(kn_ref[:, hs] * q, axis=1, keepdims=True)
        m = jnp.maximum(jnp.max(s, axis=0, keepdims=True), s_new)
        p = jnp.exp(s - m)
        p_new = jnp.exp(s_new - m)
        den = jnp.sum(p, axis=0, keepdims=True) + p_new
        num = jnp.sum(p * vb[h, :, pair], axis=0, keepdims=True) + p_new * vn_ref[:, hs]
        o_ref[:, hs] = num / den


def _dec_attend(page_table, picks, q_tok, k_new, v_new, cache_k_pages, cache_v_pages, page_base):
    n, n_pages = page_table.shape
    width = cache_k_pages.shape[2]
    n_sel = min(MOBA_TOPK, n_pages // 2)
    tok = pl.BlockSpec((None, 1, width), lambda b, pt, pk: (b, 0, 0))
    grid_spec = pltpu.PrefetchScalarGridSpec(
        num_scalar_prefetch=2,
        grid=(n,),
        in_specs=[tok, tok, tok, pl.BlockSpec(memory_space=pl.ANY), pl.BlockSpec(memory_space=pl.ANY)],
        out_specs=tok,
        scratch_shapes=[pltpu.VMEM((ATT_HEADS, n_sel * MOBA_BLOCK, LANES), F32),
                        pltpu.VMEM((ATT_HEADS, n_sel * MOBA_BLOCK, LANES), F32),
                        pltpu.SemaphoreType.DMA((2,))],
    )
    return pl.pallas_call(
        functools.partial(_dec_attend_kernel, page_base=page_base, n_pages=n_pages, n_sel=n_sel),
        grid_spec=grid_spec,
        out_shape=jax.ShapeDtypeStruct((n, 1, width), F32),
        compiler_params=_params(("arbitrary",)),
    )(page_table.reshape(-1), picks.reshape(-1), q_tok, k_new, v_new, cache_k_pages, cache_v_pages)


def _att_rope_tables(pos):
    half = ROPE_DIM // 2
    inv = ROPE_THETA ** (-2.0 * jnp.arange(half, dtype=F32) / ROPE_DIM)
    ang = pos.astype(F32)[:, None] * inv[None, :]
    cos, sin = jnp.cos(ang), jnp.sin(ang)
    t = pos.shape[0]
    rest = ATT_HD - ROPE_DIM
    ca = jnp.concatenate([cos, cos, jnp.ones((t, rest), F32)], axis=1)
    cm = jnp.concatenate([-sin, jnp.zeros((t, half + rest), F32)], axis=1)
    cp = jnp.concatenate([jnp.zeros((t, half), F32), sin, jnp.zeros((t, rest), F32)], axis=1)
    rep = LANES // ATT_HD
    return tuple(jnp.tile(a, (1, rep)) for a in (ca, cm, cp))


def _ret_rope_tables(pos):
    half = RET_DK // 2
    inv = RET_THETA ** (-2.0 * jnp.arange(half, dtype=F32) / RET_DK)
    ang = pos.astype(F32)[:, None] * inv[None, :]
    cos, sin = jnp.cos(ang), jnp.sin(ang)
    c = jnp.concatenate([cos, cos], axis=1)
    s = jnp.concatenate([-sin, sin], axis=1)
    return jnp.tile(c, (1, RET_HEADS)), jnp.tile(s, (1, RET_HEADS))


def kernel(x_prompt, x_sample, cache_k, cache_v, state_ret, state_conv, page_table, c_prompt, c_sample, w_ada, b_ada,
           w_in, conv_w, w_br, w_o, ln1_g, ln1_b, w_ffn_in, w_ffn_out, ln2_g, ln2_b):
    depth = w_in.shape[0]
    alpha = (2 * depth) ** 0.25
    bsz, seq, d = x_prompt.shape
    nd = x_sample.shape[0]
    assert x_sample.shape[1] == 1, "one new token per decode sequence"
    n_pool, page_size = cache_k.shape[1], cache_k.shape[2]
    past_len = page_table.shape[1] * page_size

    ada = _ada(jnp.concatenate([c_prompt, c_sample], axis=0), w_ada, b_ada)
    w_in_b = jnp.concatenate([w_in[:, :, N_MIX:], w_in[:, :, :N_MIX]], axis=2).astype(BF16)
    w_br_b, w_o_b = w_br.astype(BF16), w_o.astype(BF16)
    w_fi_b, w_fo_b = w_ffn_in.astype(BF16), w_ffn_out.astype(BF16)
    ck_pages = cache_k.reshape(depth * n_pool, page_size, ATT_DIM)
    cv_pages = cache_v.reshape(depth * n_pool, page_size, ATT_DIM)

    pos_p = jnp.arange(seq, dtype=jnp.int32)
    pos_s = jnp.full((nd,), past_len, jnp.int32)
    att_tab_p, ret_tab_p = _att_rope_tables(pos_p), _ret_rope_tables(pos_p)
    att_tab_s = _att_rope_tables(pos_s)
    ret_tab_s = tuple(a.reshape(nd, 1, RET_QK_DIM) for a in _ret_rope_tables(pos_s))
    ret_consts = _ret_consts()

    xp = x_prompt
    xs = x_sample.reshape(1, nd, d)
    outs = {k: [] for k in ("kp", "vp", "ks", "vs", "rp", "rs", "cp", "cs")}
    for l in range(depth):
        mods_p = [m.reshape(bsz, 1, d) for m in jnp.split(ada[l, :bsz], 6, axis=-1)]
        mods_s = [m.reshape(1, nd, d) for m in jnp.split(ada[l, bsz:], 6, axis=-1)]
        lg1, lb1 = ln1_g[l].reshape(1, d), ln1_b[l].reshape(1, d)
        lg2, lb2 = ln2_g[l].reshape(1, d), ln2_b[l].reshape(1, d)

        sh1, sc1, g1, sh2, sc2, g2 = mods_p
        z = _inproj(xp, sc1, sh1, w_in_b[l])
        q_hm, k_hm, v_hm, k_tok, v_tok = _attn_prep(z, att_tab_p, head_major=True)
        kmean = _kmean(k_tok)
        kmean_hm = kmean.reshape(bsz, -1, ATT_HEADS, ATT_HD).transpose(0, 2, 1, 3)
        a_out = _moba(q_hm, k_hm, v_hm, kmean_hm)
        r_out, r_state = _retention(z, ret_tab_p, ret_consts)
        xp, c_state = _merge(xp, g1, a_out, r_out, z, conv_w[l], w_br_b[l], w_o_b[l], lg1, lb1, alpha)
        xp = _ffn(xp, sc2, sh2, g2, w_fi_b[l], w_fo_b[l], lg2, lb2, alpha)
        outs["kp"].append(k_tok.reshape(bsz, seq, ATT_HEADS, ATT_HD))
        outs["vp"].append(v_tok.reshape(bsz, seq, ATT_HEADS, ATT_HD))
        outs["rp"].append(r_state)
        outs["cp"].append(c_state)

        sh1, sc1, g1, sh2, sc2, g2 = mods_s
        z = _inproj(xs, sc1, sh1, w_in_b[l])
        q_tok, k_new, v_new = _attn_prep(z, att_tab_s, head_major=False)
        q_tok, k_new, v_new = (a.reshape(nd, 1, ATT_DIM) for a in (q_tok, k_new, v_new))
        picks = _dec_scan(page_table, q_tok, ck_pages, l * n_pool)
        a_out = _dec_attend(page_table, picks, q_tok, k_new, v_new, ck_pages, cv_pages, l * n_pool)
        r_out, r_state = _ret_step(z.reshape(nd, 1, IN_DIM), ret_tab_s, state_ret[l])
        prev = (state_conv[l][:, 0, :].reshape(1, nd, CONV_DIM), state_conv[l][:, 1, :].reshape(1, nd, CONV_DIM))
        xs, s_new = _merge(xs, g1, a_out.reshape(1, nd, ATT_DIM), r_out.reshape(1, nd, RET_V_DIM), z, conv_w[l],
                           w_br_b[l], w_o_b[l], lg1, lb1, alpha, conv_prev=prev)
        xs = _ffn(xs, sc2, sh2, g2, w_fi_b[l], w_fo_b[l], lg2, lb2, alpha)
        outs["ks"].append(k_new.reshape(nd, 1, ATT_HEADS, ATT_HD))
        outs["vs"].append(v_new.reshape(nd, 1, ATT_HEADS, ATT_HD))
        outs["rs"].append(r_state)
        outs["cs"].append(jnp.stack([state_conv[l][:, 1, :], s_new.reshape(nd, CONV_DIM)], axis=1))

    st = {k: jnp.stack(v) for k, v in outs.items()}
    return (xp, xs.reshape(nd, 1, d), st["kp"], st["vp"], st["ks"], st["vs"], st["rp"], st["rs"], st["cp"], st["cs"])
```

```python
import functools

import jax
import jax.numpy as jnp
from jax import lax
from jax.experimental import pallas as pl
from jax.experimental.pallas import tpu as pltpu

F32 = jnp.float32
BF16 = jnp.bfloat16

D_MODEL = 1024
ATT_HEADS = 8
ATT_HD = 64
ATT_DIM = ATT_HEADS * ATT_HD
ROPE_DIM = ATT_HD // 4
ROPE_THETA = 500000.0
MOBA_BLOCK = 256
MOBA_TOPK = 3
RET_HEADS = 4
RET_DK = 64
RET_DV = 128
RET_QK_DIM = RET_HEADS * RET_DK
RET_V_DIM = RET_HEADS * RET_DV
RET_THETA = 10000.0
RET_CHUNK = 128
CONV_DIM = 512
CONV_W = 3
N_BRANCH = 3
D_FF = 2816
LN_EPS = 1e-5
N_GATE = N_BRANCH * D_MODEL
N_MIX = 3 * ATT_DIM + 2 * RET_QK_DIM + 2 * RET_V_DIM + 3 * CONV_DIM
IN_DIM = N_MIX + N_GATE

OFF_GATE = 0
OFF_AQ = N_GATE
OFF_AK = OFF_AQ + ATT_DIM
OFF_AV = OFF_AK + ATT_DIM
OFF_RQ = OFF_AV + ATT_DIM
OFF_RK = OFF_RQ + RET_QK_DIM
OFF_RV = OFF_RK + RET_QK_DIM
OFF_RG = OFF_RV + RET_V_DIM
OFF_CB = OFF_RG + RET_V_DIM
OFF_CC = OFF_CB + CONV_DIM
OFF_CX = OFF_CC + CONV_DIM

LANES = 128
BF16_SUBLANES = 16
VT_ROWS = ATT_HD + BF16_SUBLANES
LOG2E = 1.4426950408889634
VMEM_LIMIT = 48 * 1024 * 1024
NEG_INF = float("-inf")


def _params(semantics, vmem=VMEM_LIMIT):
    return pltpu.CompilerParams(dimension_semantics=semantics, vmem_limit_bytes=vmem)


def _ln(y):
    mu = jnp.mean(y, axis=-1, keepdims=True)
    yc = y - mu
    return yc * lax.rsqrt(jnp.mean(yc * yc, axis=-1, keepdims=True) + LN_EPS)


def _sigmoid(x):
    return 1.0 / (1.0 + jnp.exp(-x))


def _silu(x):
    return x * _sigmoid(x)


def _mod_spec(mod, tm, ngrid):
    per_row = mod.shape[1] != 1
    rows = tm if per_row else 1
    if ngrid == 2:
        return pl.BlockSpec((None, rows, mod.shape[2]), lambda b, i: (b, i if per_row else 0, 0))
    return pl.BlockSpec((None, rows, mod.shape[2]), lambda b, i, j: (b, i if per_row else 0, 0))


def _ada_kernel(c_ref, w_ref, b_ref, o_ref):
    cond = _silu(c_ref[...]).astype(BF16)
    o_ref[...] = jnp.dot(cond, w_ref[...].astype(BF16), preferred_element_type=F32) + b_ref[...]


def _ada(c, w_ada, b_ada):
    depth, d, n = w_ada.shape
    rows = c.shape[0]
    tn = 1536
    return pl.pallas_call(
        _ada_kernel,
        grid=(depth, n // tn),
        in_specs=[pl.BlockSpec((rows, d), lambda l, j: (0, 0)),
                  pl.BlockSpec((None, d, tn), lambda l, j: (l, 0, j)),
                  pl.BlockSpec((None, 1, tn), lambda l, j: (l, 0, j))],
        out_specs=pl.BlockSpec((None, rows, tn), lambda l, j: (l, 0, j)),
        out_shape=jax.ShapeDtypeStruct((depth, rows, n), F32),
        compiler_params=_params(("arbitrary", "arbitrary")),
    )(c, w_ada, b_ada.reshape(depth, 1, n))


def _inproj_kernel(x_ref, sc_ref, sh_ref, w_ref, o_ref, u_ref):
    @pl.when(pl.program_id(2) == 0)
    def _():
        u_ref[...] = (x_ref[...] * (1.0 + sc_ref[...]) + sh_ref[...]).astype(BF16)

    o_ref[...] = jnp.dot(u_ref[...], w_ref[...], preferred_element_type=F32)


def _inproj(x, sc, sh, w):
    bsz, t, d = x.shape
    n = w.shape[1]
    tm = min(t, 1024)
    tn = 1536
    return pl.pallas_call(
        _inproj_kernel,
        grid=(bsz, t // tm, n // tn),
        in_specs=[pl.BlockSpec((None, tm, d), lambda b, i, j: (b, i, 0)),
                  _mod_spec(sc, tm, 3), _mod_spec(sh, tm, 3),
                  pl.BlockSpec((d, tn), lambda b, i, j: (0, j))],
        out_specs=pl.BlockSpec((None, tm, tn), lambda b, i, j: (b, i, j)),
        out_shape=jax.ShapeDtypeStruct((bsz, t, n), F32),
        scratch_shapes=[pltpu.VMEM((tm, d), BF16)],
        compiler_params=_params(("arbitrary", "arbitrary", "arbitrary")),
    )(x, sc, sh, w)


def _rope_att(x, ca, cm, cp):
    return x * ca + pltpu.roll(x, LANES - ROPE_DIM // 2, 1) * cm + pltpu.roll(x, ROPE_DIM // 2, 1) * cp


def _attn_prep_kernel(q_ref, k_ref, v_ref, ca_ref, cm_ref, cp_ref, *out_refs, head_major, tm):
    ca, cm, cp = ca_ref[...], cm_ref[...], cp_ref[...]
    scale = ATT_HD ** -0.5
    if head_major:
        qt_ref, kh_ref, vt_ref, ko_ref, vo_ref = out_refs
    else:
        qo_ref, ko_ref, vo_ref = out_refs
    vo_ref[...] = v_ref[...]
    for c in range(ATT_DIM // LANES):
        sl = slice(c * LANES, (c + 1) * LANES)
        q = _rope_att(q_ref[:, sl], ca, cm, cp) * scale
        k = _rope_att(k_ref[:, sl], ca, cm, cp)
        ko_ref[:, sl] = k
        if head_major:
            qt = q.T
            vt = v_ref[:, sl].T.astype(BF16)
            for hh in range(2):
                hs = slice(hh * ATT_HD, (hh + 1) * ATT_HD)
                kh_ref[2 * c + hh] = k[:, hs].astype(BF16)
                for j in range(tm // MOBA_BLOCK):
                    cols = slice(j * MOBA_BLOCK, (j + 1) * MOBA_BLOCK)
                    qt_ref[2 * c + hh, j] = qt[hs, cols]
                    vt_ref[2 * c + hh, j, 0:ATT_HD, :] = vt[hs, cols]
                    vt_ref[2 * c + hh, j, ATT_HD:VT_ROWS, :] = jnp.ones((VT_ROWS - ATT_HD, MOBA_BLOCK), BF16)
        else:
            qo_ref[:, sl] = q


def _attn_prep(z, tabs, head_major):
    bsz, t, _ = z.shape
    tm = min(t, 512)
    cq, ck, cv = OFF_AQ // ATT_DIM, OFF_AK // ATT_DIM, OFF_AV // ATT_DIM
    tok = pl.BlockSpec((None, tm, ATT_DIM), lambda b, i: (b, i, 0))
    tab = pl.BlockSpec((tm, LANES), lambda b, i: (i, 0))
    tok_shape = jax.ShapeDtypeStruct((bsz, t, ATT_DIM), F32)
    if head_major:
        nblk = tm // MOBA_BLOCK
        hm = pl.BlockSpec((None, ATT_HEADS, tm, ATT_HD), lambda b, i: (b, 0, i, 0))
        tr = pl.BlockSpec((None, ATT_HEADS, nblk, ATT_HD, MOBA_BLOCK), lambda b, i: (b, 0, i, 0, 0))
        trv = pl.BlockSpec((None, ATT_HEADS, nblk, VT_ROWS, MOBA_BLOCK), lambda b, i: (b, 0, i, 0, 0))
        nb = t // MOBA_BLOCK
        out_specs = [tr, hm, trv, tok, tok]
        out_shape = [jax.ShapeDtypeStruct((bsz, ATT_HEADS, nb, ATT_HD, MOBA_BLOCK), F32),
                     jax.ShapeDtypeStruct((bsz, ATT_HEADS, t, ATT_HD), BF16),
                     jax.ShapeDtypeStruct((bsz, ATT_HEADS, nb, VT_ROWS, MOBA_BLOCK), BF16), tok_shape, tok_shape]
    else:
        out_specs = [tok, tok, tok]
        out_shape = [tok_shape, tok_shape, tok_shape]
    return pl.pallas_call(
        functools.partial(_attn_prep_kernel, head_major=head_major, tm=tm),
        grid=(bsz, t // tm),
        in_specs=[pl.BlockSpec((None, tm, ATT_DIM), lambda b, i: (b, i, cq)),
                  pl.BlockSpec((None, tm, ATT_DIM), lambda b, i: (b, i, ck)),
                  pl.BlockSpec((None, tm, ATT_DIM), lambda b, i: (b, i, cv)),
                  tab, tab, tab],
        out_specs=out_specs,
        out_shape=out_shape,
        compiler_params=_params(("arbitrary", "arbitrary")),
    )(z, z, z, *tabs)


def _kmean_kernel(k_ref, o_ref, *, nblk):
    for j in range(nblk):
        o_ref[j:j + 1, :] = jnp.mean(k_ref[j * MOBA_BLOCK:(j + 1) * MOBA_BLOCK, :], axis=0, keepdims=True)


def _kmean(k_tok):
    bsz, t, _ = k_tok.shape
    rows = min(t, 8 * MOBA_BLOCK)
    nblk = rows // MOBA_BLOCK
    return pl.pallas_call(
        functools.partial(_kmean_kernel, nblk=nblk),
        grid=(bsz, t // rows),
        in_specs=[pl.BlockSpec((None, rows, ATT_DIM), lambda b, i: (b, i, 0))],
        out_specs=pl.BlockSpec((None, nblk, ATT_DIM), lambda b, i: (b, i, 0)),
        out_shape=jax.ShapeDtypeStruct((bsz, t // MOBA_BLOCK, ATT_DIM), F32),
        compiler_params=_params(("arbitrary", "arbitrary")),
    )(k_tok)


def _moba_kernel(qt_ref, k_ref, vt_ref, km_ref, o_ref, s_scr, *, nb):
    i = pl.program_id(2)
    blk = MOBA_BLOCK
    key = lax.broadcasted_iota(jnp.int32, (blk, blk), 0)
    qry = lax.broadcasted_iota(jnp.int32, (blk, blk), 1)
    bid = lax.broadcasted_iota(jnp.int32, (nb, blk), 0)
    qtbs, pickss, init = [], [], []

    def scores(j):
        off = pl.multiple_of(j * blk, blk)
        return [jnp.dot(k_ref[hh, pl.ds(off, blk), :], qtbs[hh], preferred_element_type=F32) for hh in range(2)]

    for hh in range(2):
        qt = qt_ref[hh]
        gate = jnp.dot(km_ref[hh], qt, precision=lax.Precision.HIGHEST, preferred_element_type=F32)
        gate = jnp.where(bid < i, gate, NEG_INF)
        picks = []
        for r in range(MOBA_TOPK):
            mx = jnp.max(gate, axis=0, keepdims=True)
            ix = jnp.min(jnp.where(gate == mx, bid, nb), axis=0, keepdims=True)
            gate = jnp.where(bid == ix, NEG_INF, gate)
            picks.append(jnp.where(r < i, ix, -1))
        pickss.append(picks)
        qtbs.append((qt * LOG2E).astype(BF16))

    for hh, s in enumerate(scores(i)):
        s = jnp.where(key <= qry, s, NEG_INF)
        m0 = jnp.max(s, axis=0, keepdims=True)
        p = jnp.exp2(s - m0)
        init.append((m0, jnp.dot(vt_ref[hh, i], p.astype(BF16), preferred_element_type=F32)))

    for hh, s in enumerate(scores(0)):
        s_scr[hh] = s

    def step(j, carry):
        nxt = scores(j + 1)
        new = []
        for hh in range(2):
            m, acc = carry[hh]
            picks = pickss[hh]
            live = (picks[0] == j) | (picks[1] == j) | (picks[2] == j)
            sj = s_scr[hh]
            m_new = jnp.maximum(m, jnp.where(live, jnp.max(sj, axis=0, keepdims=True), NEG_INF))
            alpha = jnp.exp2(m - m_new)
            pj = jnp.exp2(sj - jnp.where(live, m_new, jnp.inf))
            acc = alpha * acc + jnp.dot(vt_ref[hh, j], pj.astype(BF16), preferred_element_type=F32)
            new.append((m_new, acc))
        for hh in range(2):
            s_scr[hh] = nxt[hh]
        return tuple(new)

    fin = lax.fori_loop(0, i, step, tuple(init))
    outs = [acc[0:ATT_HD] / acc[ATT_HD:ATT_HD + 1] for (_, acc) in fin]
    o_ref[...] = jnp.concatenate(outs, axis=0).T


def _moba(qt, k_hm, vt, kmean_hm):
    bsz, nh, nb, hd, blk = qt.shape
    t = nb * blk
    return pl.pallas_call(
        functools.partial(_moba_kernel, nb=nb),
        grid=(bsz, nh // 2, nb),
        in_specs=[pl.BlockSpec((None, 2, None, hd, blk), lambda b, h, i: (b, h, i, 0, 0)),
                  pl.BlockSpec((None, 2, t, hd), lambda b, h, i: (b, h, 0, 0)),
                  pl.BlockSpec((None, 2, nb, vt.shape[3], blk), lambda b, h, i: (b, h, 0, 0, 0)),
                  pl.BlockSpec((None, 2, nb, hd), lambda b, h, i: (b, h, 0, 0))],
        out_specs=pl.BlockSpec((None, blk, 2 * hd), lambda b, h, i: (b, i, h)),
        out_shape=jax.ShapeDtypeStruct((bsz, t, nh * hd), F32),
        scratch_shapes=[pltpu.VMEM((2, blk, blk), F32)],
        compiler_params=_params(("arbitrary", "arbitrary", "arbitrary")),
    )(qt, k_hm, vt, kmean_hm)


def _rope_ret(x, c, s):
    w = x.shape[1]
    d = lax.broadcasted_iota(jnp.int32, x.shape, 1) % RET_DK
    half = RET_DK // 2
    partner = jnp.where(d < half, pltpu.roll(x, w - half, 1), pltpu.roll(x, half, 1))
    return x * c + partner * s


def _ret_decay(h):
    return 1.0 - 2.0 ** (-5.0 - h)


def _retention_kernel(q_ref, k_ref, v_ref, g_ref, c_ref, s_ref, dm_ref, qd_ref, kd_ref, o_ref, st_ref, state, *, tm):
    ch = RET_CHUNK

    @pl.when(pl.program_id(1) == 0)
    def _():
        state[...] = jnp.zeros_like(state)

    q = _rope_ret(q_ref[...], c_ref[...], s_ref[...])
    k = _rope_ret(k_ref[...], c_ref[...], s_ref[...]) * (RET_DK ** -0.5)
    kt = k.T
    for c in range(tm // ch):
        rows = slice(c * ch, (c + 1) * ch)
        for h in range(RET_HEADS):
            qc = q[rows, h * RET_DK:(h + 1) * RET_DK]
            ktc = kt[h * RET_DK:(h + 1) * RET_DK, rows]
            vc = v_ref[rows, h * RET_DV:(h + 1) * RET_DV].astype(BF16)
            s_old = state[h]
            inner = jnp.dot(qc.astype(BF16), ktc.astype(BF16), preferred_element_type=F32) * dm_ref[h]
            o = jnp.dot(inner.astype(BF16), vc, preferred_element_type=F32)
            o = o + jnp.dot((qc * qd_ref[h]).astype(BF16), s_old.astype(BF16), preferred_element_type=F32)
            c_dec = _ret_decay(h) ** ch
            state[h] = c_dec * s_old + jnp.dot((ktc * kd_ref[h]).astype(BF16), vc, preferred_element_type=F32)
            gate = g_ref[rows, h * RET_DV:(h + 1) * RET_DV]
            o_ref[rows, h * RET_DV:(h + 1) * RET_DV] = _ln(o) * _silu(gate)
    st_ref[...] = state[...]


def _retention(z, tabs, consts):
    bsz, t, _ = z.shape
    tm = min(t, 512)
    dmat, qdec, kdec = consts
    const3 = lambda a: pl.BlockSpec(a.shape, lambda b, i: (0, 0, 0))
    tab = pl.BlockSpec((tm, RET_QK_DIM), lambda b, i: (i, 0))
    return pl.pallas_call(
        functools.partial(_retention_kernel, tm=tm),
        grid=(bsz, t // tm),
        in_specs=[pl.BlockSpec((None, tm, RET_QK_DIM), lambda b, i: (b, i, OFF_RQ // RET_QK_DIM)),
                  pl.BlockSpec((None, tm, RET_QK_DIM), lambda b, i: (b, i, OFF_RK // RET_QK_DIM)),
                  pl.BlockSpec((None, tm, RET_V_DIM), lambda b, i: (b, i, OFF_RV // RET_V_DIM)),
                  pl.BlockSpec((None, tm, RET_V_DIM), lambda b, i: (b, i, OFF_RG // RET_V_DIM)),
                  tab, tab, const3(dmat), const3(qdec), const3(kdec)],
        out_specs=[pl.BlockSpec((None, tm, RET_V_DIM), lambda b, i: (b, i, 0)),
                   pl.BlockSpec((None, RET_HEADS, RET_DK, RET_DV), lambda b, i: (b, 0, 0, 0))],
        out_shape=[jax.ShapeDtypeStruct((bsz, t, RET_V_DIM), F32),
                   jax.ShapeDtypeStruct((bsz, RET_HEADS, RET_DK, RET_DV), F32)],
        scratch_shapes=[pltpu.VMEM((RET_HEADS, RET_DK, RET_DV), F32)],
        compiler_params=_params(("arbitrary", "arbitrary")),
    )(z, z, z, z, *tabs, dmat, qdec, kdec)


def _ret_consts():
    ch = RET_CHUNK
    log_g = jnp.log(1.0 - 2.0 ** (-5.0 - jnp.arange(RET_HEADS, dtype=F32)))
    i = jnp.arange(ch, dtype=F32)
    diff = i[:, None] - i[None, :]
    dmat = jnp.where(diff >= 0, jnp.exp(jnp.maximum(diff, 0.0)[None] * log_g[:, None, None]), 0.0)
    qdec = jnp.exp((i + 1.0)[None, :] * log_g[:, None])[:, :, None]
    kdec = jnp.exp((ch - 1.0 - i)[None, :] * log_g[:, None])[:, None, :]
    return dmat, qdec, kdec


def _row_to_col(row):
    n = row.shape[1]
    eye = lax.broadcasted_iota(jnp.int32, (n, n), 0) == lax.broadcasted_iota(jnp.int32, (n, n), 1)
    return jnp.sum(jnp.where(eye, jnp.broadcast_to(row, (n, n)), 0.0), axis=1, keepdims=True)


def _ret_step_kernel(q_ref, k_ref, v_ref, g_ref, c_ref, s_ref, st_ref, o_ref, sn_ref):
    q = _rope_ret(q_ref[...], c_ref[...], s_ref[...])
    k = _rope_ret(k_ref[...], c_ref[...], s_ref[...]) * (RET_DK ** -0.5)
    for h in range(RET_HEADS):
        g = _ret_decay(h)
        qrow = q[:, h * RET_DK:(h + 1) * RET_DK]
        krow = k[:, h * RET_DK:(h + 1) * RET_DK]
        qcol = _row_to_col(qrow)
        kcol = _row_to_col(krow)
        v = v_ref[:, h * RET_DV:(h + 1) * RET_DV]
        s_old = st_ref[h]
        inner = jnp.sum(qrow * krow, axis=1, keepdims=True)
        o = inner * v + jnp.sum((qcol * g) * s_old, axis=0, keepdims=True)
        sn_ref[h] = g * s_old + kcol * v
        o_ref[:, h * RET_DV:(h + 1) * RET_DV] = _ln(o) * _silu(g_ref[:, h * RET_DV:(h + 1) * RET_DV])


def _ret_step(z, tabs, state):
    n = z.shape[0]
    tab = pl.BlockSpec((None, 1, RET_QK_DIM), lambda b: (b, 0, 0))
    st = pl.BlockSpec((None, RET_HEADS, RET_DK, RET_DV), lambda b: (b, 0, 0, 0))
    return pl.pallas_call(
        _ret_step_kernel,
        grid=(n,),
        in_specs=[pl.BlockSpec((None, 1, RET_QK_DIM), lambda b: (b, 0, OFF_RQ // RET_QK_DIM)),
                  pl.BlockSpec((None, 1, RET_QK_DIM), lambda b: (b, 0, OFF_RK // RET_QK_DIM)),
                  pl.BlockSpec((None, 1, RET_V_DIM), lambda b: (b, 0, OFF_RV // RET_V_DIM)),
                  pl.BlockSpec((None, 1, RET_V_DIM), lambda b: (b, 0, OFF_RG // RET_V_DIM)),
                  tab, tab, st],
        out_specs=[pl.BlockSpec((None, 1, RET_V_DIM), lambda b: (b, 0, 0)), st],
        out_shape=[jax.ShapeDtypeStruct((n, 1, RET_V_DIM), F32),
                   jax.ShapeDtypeStruct(state.shape, F32)],
        compiler_params=_params(("arbitrary",)),
    )(z, z, z, z, *tabs, state)


def _merge_kernel(*refs, tm, seq_mode, alpha):
    if seq_mode:
        (x_ref, g1_ref, a_ref, r_ref, cb_ref, cc_ref, cx_ref, gt0_ref, gt1_ref, gt2_ref, cw_ref, wbr_ref, wo_ref,
         lg_ref, lb_ref, o_ref, cs_ref, carry) = refs
    else:
        (x_ref, g1_ref, a_ref, r_ref, cb_ref, cc_ref, cx_ref, gt0_ref, gt1_ref, gt2_ref, cw_ref, wbr_ref, wo_ref,
         lg_ref, lb_ref, p0_ref, p1_ref, o_ref, cs_ref) = refs
    s = cc_ref[...] * cx_ref[...]
    if seq_mode:
        @pl.when(pl.program_id(1) == 0)
        def _():
            carry[...] = jnp.zeros_like(carry)

        rid = lax.broadcasted_iota(jnp.int32, s.shape, 0)
        prev0, prev1 = carry[0:1, :], carry[1:2, :]
        s1 = jnp.where(rid == 0, prev1, pltpu.roll(s, 1, 0))
        s2 = jnp.where(rid == 0, prev0, jnp.where(rid == 1, prev1, pltpu.roll(s, 2, 0)))
        carry[...] = s[tm - 2:tm, :]
        cs_ref[...] = s[tm - 2:tm, :]
    else:
        s2, s1 = p0_ref[...], p1_ref[...]
        cs_ref[...] = s
    y = cw_ref[0:1, :] * s2 + cw_ref[1:2, :] * s1 + cw_ref[2:3, :] * s
    c_out = cb_ref[...] * y
    merged = _sigmoid(gt0_ref[...]) * jnp.dot(a_ref[...].astype(BF16), wbr_ref[0], preferred_element_type=F32)
    merged = merged + _sigmoid(gt1_ref[...]) * jnp.dot(r_ref[...].astype(BF16), wbr_ref[1],
                                                       preferred_element_type=F32)
    merged = merged + _sigmoid(gt2_ref[...]) * jnp.dot(c_out.astype(BF16), wbr_ref[2], preferred_element_type=F32)
    m = jnp.dot(merged.astype(BF16), wo_ref[...], preferred_element_type=F32)
    o_ref[...] = _ln(alpha * x_ref[...] + g1_ref[...] * m) * lg_ref[...] + lb_ref[...]


def _merge(x, g1, a_out, r_out, z, conv_w, w_br, w_o, ln_g, ln_b, alpha, conv_prev=None):
    bsz, t, d = x.shape
    seq_mode = conv_prev is None
    tm = min(t, 256)
    tok512 = lambda cblk: pl.BlockSpec((None, tm, CONV_DIM), lambda b, i: (b, i, cblk))
    gate = lambda n: pl.BlockSpec((None, tm, d), lambda b, i: (b, i, OFF_GATE // d + n))
    full2 = lambda a: pl.BlockSpec(a.shape, lambda b, i: (0, 0))
    in_specs = [pl.BlockSpec((None, tm, d), lambda b, i: (b, i, 0)), _mod_spec(g1, tm, 2),
                tok512(0), tok512(0), tok512(OFF_CB // CONV_DIM), tok512(OFF_CC // CONV_DIM),
                tok512(OFF_CX // CONV_DIM), gate(0), gate(1), gate(2),
                full2(conv_w), pl.BlockSpec(w_br.shape, lambda b, i: (0, 0, 0)), full2(w_o),
                full2(ln_g), full2(ln_b)]
    args = [x, g1, a_out, r_out, z, z, z, z, z, z, conv_w, w_br, w_o, ln_g, ln_b]
    scratch = []
    if seq_mode:
        cs_spec = pl.BlockSpec((None, CONV_W - 1, CONV_DIM), lambda b, i: (b, 0, 0))
        cs_shape = jax.ShapeDtypeStruct((bsz, CONV_W - 1, CONV_DIM), F32)
        scratch = [pltpu.VMEM((CONV_W - 1, CONV_DIM), F32)]
    else:
        in_specs += [tok512(0), tok512(0)]
        args += list(conv_prev)
        cs_spec = tok512(0)
        cs_shape = jax.ShapeDtypeStruct((bsz, t, CONV_DIM), F32)
    return pl.pallas_call(
        functools.partial(_merge_kernel, tm=tm, seq_mode=seq_mode, alpha=alpha),
        grid=(bsz, t // tm),
        in_specs=in_specs,
        out_specs=[pl.BlockSpec((None, tm, d), lambda b, i: (b, i, 0)), cs_spec],
        out_shape=[jax.ShapeDtypeStruct((bsz, t, d), F32), cs_shape],
        scratch_shapes=scratch,
        compiler_params=_params(("arbitrary", "arbitrary")),
    )(*args)


def _ffn_kernel(x_ref, sc_ref, sh_ref, g2_ref, wa_ref, wb_ref, wo_ref, lg_ref, lb_ref, o_ref, u_ref, acc_ref, *,
                alpha):
    f = pl.program_id(2)

    @pl.when(f == 0)
    def _():
        u_ref[...] = (x_ref[...] * (1.0 + sc_ref[...]) + sh_ref[...]).astype(BF16)
        acc_ref[...] = jnp.zeros_like(acc_ref)

    u = u_ref[...]
    a = jnp.dot(u, wa_ref[...], preferred_element_type=F32)
    b = jnp.dot(u, wb_ref[...], preferred_element_type=F32)
    acc_ref[...] += jnp.dot((_silu(a) * b).astype(BF16), wo_ref[...], preferred_element_type=F32)

    @pl.when(f == pl.num_programs(2) - 1)
    def _():
        o_ref[...] = _ln(alpha * x_ref[...] + g2_ref[...] * acc_ref[...]) * lg_ref[...] + lb_ref[...]


def _ffn(x, sc, sh, g2, w_in, w_out, ln_g, ln_b, alpha):
    bsz, t, d = x.shape
    tm = min(t, 512)
    tf = D_FF // 2
    nf = D_FF // tf
    full2 = lambda a: pl.BlockSpec(a.shape, lambda b, i, f: (0, 0))
    return pl.pallas_call(
        functools.partial(_ffn_kernel, alpha=alpha),
        grid=(bsz, t // tm, nf),
        in_specs=[pl.BlockSpec((None, tm, d), lambda b, i, f: (b, i, 0)),
                  _mod_spec(sc, tm, 3), _mod_spec(sh, tm, 3), _mod_spec(g2, tm, 3),
                  pl.BlockSpec((d, tf), lambda b, i, f: (0, f)),
                  pl.BlockSpec((d, tf), lambda b, i, f: (0, nf + f)),
                  pl.BlockSpec((tf, d), lambda b, i, f: (f, 0)),
                  full2(ln_g), full2(ln_b)],
        out_specs=pl.BlockSpec((None, tm, d), lambda b, i, f: (b, i, 0)),
        out_shape=jax.ShapeDtypeStruct((bsz, t, d), F32),
        scratch_shapes=[pltpu.VMEM((tm, d), BF16), pltpu.VMEM((tm, d), F32)],
        compiler_params=_params(("arbitrary", "arbitrary", "arbitrary")),
    )(x, sc, sh, g2, w_in, w_in, w_out, ln_g, ln_b)


def _dec_scan_kernel(pt_ref, q_ref, ck_ref, o_ref, buf, ksum, sem, *, layer, n_pages, cp, page_size):
    b = pl.program_id(0)
    n_chunks = n_pages // cp
    cb = cp // 2

    def copies(c, slot):
        return [pltpu.make_async_copy(ck_ref.at[layer, pt_ref[b * n_pages + c * cp + p]], buf.at[slot, p],
                                      sem.at[slot]) for p in range(cp)]

    for cpy in copies(0, 0):
        cpy.start()

    def body(c, carry):
        slot = c % 2

        @pl.when(c + 1 < n_chunks)
        def _():
            for cpy in copies(c + 1, 1 - slot):
                cpy.start()

        for cpy in copies(c, slot):
            cpy.wait()
        for jb in range(cb):
            pages = buf[slot, 2 * jb] + buf[slot, 2 * jb + 1]
            ksum[c * cb + jb] = jnp.sum(pages.reshape(page_size, ATT_HEADS, ATT_HD), axis=0)
        return carry

    lax.fori_loop(0, n_chunks, body, 0)

    nbp = n_pages // 2
    gate = jnp.sum((ksum[...] * (1.0 / MOBA_BLOCK)) * q_ref[...], axis=-1, keepdims=True)
    bid = lax.broadcasted_iota(jnp.int32, (nbp, ATT_HEADS, 1), 0)
    lane = lax.broadcasted_iota(jnp.int32, (ATT_HEADS, LANES), 1)
    res = jnp.zeros((ATT_HEADS, LANES), jnp.int32)
    for r in range(min(MOBA_TOPK, nbp)):
        mx = jnp.max(gate, axis=0, keepdims=True)
        ix = jnp.min(jnp.where(gate == mx, bid, nbp), axis=0, keepdims=True)
        gate = jnp.where(bid == ix, NEG_INF, gate)
        res = jnp.where(lane == r, ix[0], res)
    o_ref[...] = res


def _dec_scan(page_table, q_hd, cache_k_rows, layer, page_size):
    n, n_pages = page_table.shape
    assert page_size * 2 == MOBA_BLOCK
    cp = min(8, n_pages)
    nbp = n_pages // 2
    grid_spec = pltpu.PrefetchScalarGridSpec(
        num_scalar_prefetch=1,
        grid=(n,),
        in_specs=[pl.BlockSpec((None, ATT_HEADS, ATT_HD), lambda b, pt: (b, 0, 0)),
                  pl.BlockSpec(memory_space=pl.ANY)],
        out_specs=pl.BlockSpec((None, ATT_HEADS, LANES), lambda b, pt: (b, 0, 0)),
        scratch_shapes=[pltpu.VMEM((2, cp, page_size * ATT_HEADS, ATT_HD), F32),
                        pltpu.VMEM((nbp, ATT_HEADS, ATT_HD), F32),
                        pltpu.SemaphoreType.DMA((2,))],
    )
    return pl.pallas_call(
        functools.partial(_dec_scan_kernel, layer=layer, n_pages=n_pages, cp=cp, page_size=page_size),
        grid_spec=grid_spec,
        out_shape=jax.ShapeDtypeStruct((n, ATT_HEADS, LANES), jnp.int32),
        compiler_params=_params(("arbitrary",)),
    )(page_table.reshape(-1), q_hd, cache_k_rows)


def _dec_attend_kernel(pt_ref, pick_ref, q_ref, kn_ref, vn_ref, ck_ref, cv_ref, o_ref, kb, vb, sem, *,
                       layer, n_pages, n_sel, page_size):
    b = pl.program_id(0)
    page_rows = page_size * ATT_HEADS
    n_keys = n_sel * MOBA_BLOCK

    def copies(h, slot):
        out = []
        for r in range(n_sel):
            blk = pick_ref[(b * ATT_HEADS + h) * n_sel + r]
            for p in range(2):
                page = pt_ref[b * n_pages + 2 * blk + p]
                dst = pl.ds((2 * r + p) * page_rows, page_rows)
                out.append(pltpu.make_async_copy(ck_ref.at[layer, page], kb.at[slot, dst, :], sem.at[2 * slot]))
                out.append(pltpu.make_async_copy(cv_ref.at[layer, page], vb.at[slot, dst, :], sem.at[2 * slot + 1]))
        return out

    for cpy in copies(0, 0):
        cpy.start()
    for h in range(ATT_HEADS):
        slot = h % 2
        if h + 1 < ATT_HEADS:
            for cpy in copies(h + 1, 1 - slot):
                cpy.start()
        for cpy in copies(h, slot):
            cpy.wait()
        q = q_ref[h:h + 1, :]
        k = kb[slot, pl.ds(h, n_keys, stride=ATT_HEADS), :]
        v = vb[slot, pl.ds(h, n_keys, stride=ATT_HEADS), :]
        s = jnp.sum(k * q, axis=1, keepdims=True)
        s_new = jnp.sum(kn_ref[h:h + 1, :] * q, axis=1, keepdims=True)
        m = jnp.maximum(jnp.max(s, axis=0, keepdims=True), s_new)
        p = jnp.exp(s - m)
        p_new = jnp.exp(s_new - m)
        den = jnp.sum(p, axis=0, keepdims=True) + p_new
        num = jnp.sum(p * v, axis=0, keepdims=True) + p_new * vn_ref[h:h + 1, :]
        o_ref[h:h + 1, :] = num / den


def _dec_attend(page_table, picks, q_hd, k_new, v_new, cache_k_rows, cache_v_rows, layer, page_size):
    n, n_pages = page_table.shape
    n_sel = min(MOBA_TOPK, n_pages // 2)
    tok = pl.BlockSpec((None, ATT_HEADS, ATT_HD), lambda b, pt, pk: (b, 0, 0))
    buf_rows = n_sel * 2 * page_size * ATT_HEADS
    grid_spec = pltpu.PrefetchScalarGridSpec(
        num_scalar_prefetch=2,
        grid=(n,),
        in_specs=[tok, tok, tok, pl.BlockSpec(memory_space=pl.ANY), pl.BlockSpec(memory_space=pl.ANY)],
        out_specs=tok,
        scratch_shapes=[pltpu.VMEM((2, buf_rows, ATT_HD), F32),
                        pltpu.VMEM((2, buf_rows, ATT_HD), F32),
                        pltpu.SemaphoreType.DMA((4,))],
    )
    return pl.pallas_call(
        functools.partial(_dec_attend_kernel, layer=layer, n_pages=n_pages, n_sel=n_sel, page_size=page_size),
        grid_spec=grid_spec,
        out_shape=jax.ShapeDtypeStruct((n, ATT_HEADS, ATT_HD), F32),
        compiler_params=_params(("arbitrary",)),
    )(page_table.reshape(-1), picks, q_hd, k_new, v_new, cache_k_rows, cache_v_rows)


def _att_rope_tables(pos):
    half = ROPE_DIM // 2
    inv = ROPE_THETA ** (-2.0 * jnp.arange(half, dtype=F32) / ROPE_DIM)
    ang = pos.astype(F32)[:, None] * inv[None, :]
    cos, sin = jnp.cos(ang), jnp.sin(ang)
    t = pos.shape[0]
    rest = ATT_HD - ROPE_DIM
    ca = jnp.concatenate([cos, cos, jnp.ones((t, rest), F32)], axis=1)
    cm = jnp.concatenate([-sin, jnp.zeros((t, half + rest), F32)], axis=1)
    cp = jnp.concatenate([jnp.zeros((t, half), F32), sin, jnp.zeros((t, rest), F32)], axis=1)
    rep = LANES // ATT_HD
    return tuple(jnp.tile(a, (1, rep)) for a in (ca, cm, cp))


def _ret_rope_tables(pos):
    half = RET_DK // 2
    inv = RET_THETA ** (-2.0 * jnp.arange(half, dtype=F32) / RET_DK)
    ang = pos.astype(F32)[:, None] * inv[None, :]
    cos, sin = jnp.cos(ang), jnp.sin(ang)
    c = jnp.concatenate([cos, cos], axis=1)
    s = jnp.concatenate([-sin, sin], axis=1)
    return jnp.tile(c, (1, RET_HEADS)), jnp.tile(s, (1, RET_HEADS))


def kernel(x_prompt, x_sample, cache_k, cache_v, state_ret, state_conv, page_table, c_prompt, c_sample, w_ada, b_ada,
           w_in, conv_w, w_br, w_o, ln1_g, ln1_b, w_ffn_in, w_ffn_out, ln2_g, ln2_b):
    depth = w_in.shape[0]
    alpha = (2 * depth) ** 0.25
    bsz, seq, d = x_prompt.shape
    nd = x_sample.shape[0]
    assert x_sample.shape[1] == 1, "one new token per decode sequence"
    n_pool, page_size = cache_k.shape[1], cache_k.shape[2]
    past_len = page_table.shape[1] * page_size

    ada = _ada(jnp.concatenate([c_prompt, c_sample], axis=0), w_ada, b_ada)
    w_in_b = jnp.concatenate([w_in[:, :, N_MIX:], w_in[:, :, :N_MIX]], axis=2).astype(BF16)
    w_br_b, w_o_b = w_br.astype(BF16), w_o.astype(BF16)
    w_fi_b, w_fo_b = w_ffn_in.astype(BF16), w_ffn_out.astype(BF16)
    ck_rows = cache_k.reshape(depth, n_pool, page_size * ATT_HEADS, ATT_HD)
    cv_rows = cache_v.reshape(depth, n_pool, page_size * ATT_HEADS, ATT_HD)

    pos_p = jnp.arange(seq, dtype=jnp.int32)
    pos_s = jnp.full((nd,), past_len, jnp.int32)
    att_tab_p, ret_tab_p = _att_rope_tables(pos_p), _ret_rope_tables(pos_p)
    att_tab_s = _att_rope_tables(pos_s)
    ret_tab_s = tuple(a.reshape(nd, 1, RET_QK_DIM) for a in _ret_rope_tables(pos_s))
    ret_consts = _ret_consts()

    xp = x_prompt
    xs = x_sample.reshape(1, nd, d)
    outs = {k: [] for k in ("kp", "vp", "ks", "vs", "rp", "rs", "cp", "cs")}
    for l in range(depth):
        mods_p = [m.reshape(bsz, 1, d) for m in jnp.split(ada[l, :bsz], 6, axis=-1)]
        mods_s = [m.reshape(1, nd, d) for m in jnp.split(ada[l, bsz:], 6, axis=-1)]
        lg1, lb1 = ln1_g[l].reshape(1, d), ln1_b[l].reshape(1, d)
        lg2, lb2 = ln2_g[l].reshape(1, d), ln2_b[l].reshape(1, d)

        sh1, sc1, g1, sh2, sc2, g2 = mods_p
        z = _inproj(xp, sc1, sh1, w_in_b[l])
        q_t, k_hm, v_t, k_tok, v_tok = _attn_prep(z, att_tab_p, head_major=True)
        kmean = _kmean(k_tok)
        kmean_hm = kmean.reshape(bsz, -1, ATT_HEADS, ATT_HD).transpose(0, 2, 1, 3)
        a_out = _moba(q_t, k_hm, v_t, kmean_hm)
        r_out, r_state = _retention(z, ret_tab_p, ret_consts)
        xp, c_state = _merge(xp, g1, a_out, r_out, z, conv_w[l], w_br_b[l], w_o_b[l], lg1, lb1, alpha)
        xp = _ffn(xp, sc2, sh2, g2, w_fi_b[l], w_fo_b[l], lg2, lb2, alpha)
        outs["kp"].append(k_tok.reshape(bsz, seq, ATT_HEADS, ATT_HD))
        outs["vp"].append(v_tok.reshape(bsz, seq, ATT_HEADS, ATT_HD))
        outs["rp"].append(r_state)
        outs["cp"].append(c_state)

        sh1, sc1, g1, sh2, sc2, g2 = mods_s
        z = _inproj(xs, sc1, sh1, w_in_b[l])
        q_tok, k_new, v_new = _attn_prep(z, att_tab_s, head_major=False)
        q_tok, k_new, v_new = (a.reshape(nd, ATT_HEADS, ATT_HD) for a in (q_tok, k_new, v_new))
        picks = _dec_scan(page_table, q_tok, ck_rows, l, page_size)
        n_sel = min(MOBA_TOPK, page_table.shape[1] // 2)
        a_out = _dec_attend(page_table, picks[:, :, :n_sel].reshape(-1), q_tok, k_new, v_new, ck_rows, cv_rows, l,
                            page_size)
        r_out, r_state = _ret_step(z.reshape(nd, 1, IN_DIM), ret_tab_s, state_ret[l])
        prev = (state_conv[l][:, 0, :].reshape(1, nd, CONV_DIM), state_conv[l][:, 1, :].reshape(1, nd, CONV_DIM))
        xs, s_new = _merge(xs, g1, a_out.reshape(1, nd, ATT_DIM), r_out.reshape(1, nd, RET_V_DIM), z, conv_w[l],
                           w_br_b[l], w_o_b[l], lg1, lb1, alpha, conv_prev=prev)
        xs = _ffn(xs, sc2, sh2, g2, w_fi_b[l], w_fo_b[l], lg2, lb2, alpha)
        outs["ks"].append(k_new.reshape(nd, 1, ATT_HEADS, ATT_HD))
        outs["vs"].append(v_new.reshape(nd, 1, ATT_HEADS, ATT_HD))
        outs["rs"].append(r_state)
        outs["cs"].append(jnp.stack([state_conv[l][:, 1, :], s_new.reshape(nd, CONV_DIM)], axis=1))

    st = {k: jnp.stack(v) for k, v in outs.items()}
    return (xp, xs.reshape(nd, 1, d), st["kp"], st["vp"], st["ks"], st["vs"], st["rp"], st["rs"], st["cp"], st["cs"])
```

```python
import functools

import jax
import jax.numpy as jnp
from jax import lax
from jax.experimental import pallas as pl
from jax.experimental.pallas import tpu as pltpu

F32 = jnp.float32
BF16 = jnp.bfloat16

D_MODEL = 1024
ATT_HEADS = 8
ATT_HD = 64
ATT_DIM = ATT_HEADS * ATT_HD
ROPE_DIM = ATT_HD // 4
ROPE_THETA = 500000.0
MOBA_BLOCK = 256
MOBA_TOPK = 3
RET_HEADS = 4
RET_DK = 64
RET_DV = 128
RET_QK_DIM = RET_HEADS * RET_DK
RET_V_DIM = RET_HEADS * RET_DV
RET_THETA = 10000.0
RET_CHUNK = 128
CONV_DIM = 512
CONV_W = 3
N_BRANCH = 3
D_FF = 2816
LN_EPS = 1e-5
N_GATE = N_BRANCH * D_MODEL
N_MIX = 3 * ATT_DIM + 2 * RET_QK_DIM + 2 * RET_V_DIM + 3 * CONV_DIM
IN_DIM = N_MIX + N_GATE

OFF_GATE = 0
OFF_AQ = N_GATE
OFF_AK = OFF_AQ + ATT_DIM
OFF_AV = OFF_AK + ATT_DIM
OFF_RQ = OFF_AV + ATT_DIM
OFF_RK = OFF_RQ + RET_QK_DIM
OFF_RV = OFF_RK + RET_QK_DIM
OFF_RG = OFF_RV + RET_V_DIM
OFF_CB = OFF_RG + RET_V_DIM
OFF_CC = OFF_CB + CONV_DIM
OFF_CX = OFF_CC + CONV_DIM

LANES = 128
BF16_SUBLANES = 16
VT_ROWS = ATT_HD + BF16_SUBLANES
LOG2E = 1.4426950408889634
VMEM_LIMIT = 48 * 1024 * 1024
NEG_INF = float("-inf")


def _params(semantics, vmem=VMEM_LIMIT):
    return pltpu.CompilerParams(dimension_semantics=semantics, vmem_limit_bytes=vmem)


def _ln(y):
    mu = jnp.mean(y, axis=-1, keepdims=True)
    yc = y - mu
    return yc * lax.rsqrt(jnp.mean(yc * yc, axis=-1, keepdims=True) + LN_EPS)


def _sigmoid(x):
    return 1.0 / (1.0 + jnp.exp(-x))


def _silu(x):
    return x * _sigmoid(x)


def _mod_spec(mod, tm, ngrid):
    per_row = mod.shape[1] != 1
    rows = tm if per_row else 1
    if ngrid == 2:
        return pl.BlockSpec((None, rows, mod.shape[2]), lambda b, i: (b, i if per_row else 0, 0))
    return pl.BlockSpec((None, rows, mod.shape[2]), lambda b, i, j: (b, i if per_row else 0, 0))


def _ada_kernel(c_ref, w_ref, b_ref, o_ref):
    cond = _silu(c_ref[...]).astype(BF16)
    o_ref[...] = jnp.dot(cond, w_ref[...].astype(BF16), preferred_element_type=F32) + b_ref[...]


def _ada(c, w_ada, b_ada):
    depth, d, n = w_ada.shape
    rows = c.shape[0]
    tn = 1536
    return pl.pallas_call(
        _ada_kernel,
        grid=(depth, n // tn),
        in_specs=[pl.BlockSpec((rows, d), lambda l, j: (0, 0)),
                  pl.BlockSpec((None, d, tn), lambda l, j: (l, 0, j)),
                  pl.BlockSpec((None, 1, tn), lambda l, j: (l, 0, j))],
        out_specs=pl.BlockSpec((None, rows, tn), lambda l, j: (l, 0, j)),
        out_shape=jax.ShapeDtypeStruct((depth, rows, n), F32),
        compiler_params=_params(("arbitrary", "arbitrary")),
    )(c, w_ada, b_ada.reshape(depth, 1, n))


def _inproj_kernel(x_ref, sc_ref, sh_ref, w_ref, o_ref, u_ref):
    @pl.when(pl.program_id(2) == 0)
    def _():
        u_ref[...] = (x_ref[...] * (1.0 + sc_ref[...]) + sh_ref[...]).astype(BF16)

    o_ref[...] = jnp.dot(u_ref[...], w_ref[...], preferred_element_type=F32)


def _inproj(x, sc, sh, w):
    bsz, t, d = x.shape
    n = w.shape[1]
    tm = min(t, 1024)
    tn = 1536
    return pl.pallas_call(
        _inproj_kernel,
        grid=(bsz, t // tm, n // tn),
        in_specs=[pl.BlockSpec((None, tm, d), lambda b, i, j: (b, i, 0)),
                  _mod_spec(sc, tm, 3), _mod_spec(sh, tm, 3),
                  pl.BlockSpec((d, tn), lambda b, i, j: (0, j))],
        out_specs=pl.BlockSpec((None, tm, tn), lambda b, i, j: (b, i, j)),
        out_shape=jax.ShapeDtypeStruct((bsz, t, n), F32),
        scratch_shapes=[pltpu.VMEM((tm, d), BF16)],
        compiler_params=_params(("arbitrary", "arbitrary", "arbitrary")),
    )(x, sc, sh, w)


def _rope_att(x, ca, cm, cp):
    return x * ca + pltpu.roll(x, LANES - ROPE_DIM // 2, 1) * cm + pltpu.roll(x, ROPE_DIM // 2, 1) * cp


def _attn_prep_kernel(q_ref, k_ref, v_ref, ca_ref, cm_ref, cp_ref, *out_refs, head_major, tm):
    ca, cm, cp = ca_ref[...], cm_ref[...], cp_ref[...]
    scale = ATT_HD ** -0.5
    if head_major:
        qt_ref, kh_ref, vt_ref, ko_ref, vo_ref = out_refs
    else:
        qo_ref, ko_ref, vo_ref = out_refs
    vo_ref[...] = v_ref[...]
    for c in range(ATT_DIM // LANES):
        sl = slice(c * LANES, (c + 1) * LANES)
        q = _rope_att(q_ref[:, sl], ca, cm, cp) * scale
        k = _rope_att(k_ref[:, sl], ca, cm, cp)
        ko_ref[:, sl] = k
        if head_major:
            qt = q.T
            vt = v_ref[:, sl].T.astype(BF16)
            for hh in range(2):
                hs = slice(hh * ATT_HD, (hh + 1) * ATT_HD)
                kh_ref[2 * c + hh] = k[:, hs].astype(BF16)
                for j in range(tm // MOBA_BLOCK):
                    cols = slice(j * MOBA_BLOCK, (j + 1) * MOBA_BLOCK)
                    qt_ref[2 * c + hh, j] = qt[hs, cols]
                    vt_ref[2 * c + hh, j, 0:ATT_HD, :] = vt[hs, cols]
                    vt_ref[2 * c + hh, j, ATT_HD:VT_ROWS, :] = jnp.ones((VT_ROWS - ATT_HD, MOBA_BLOCK), BF16)
        else:
            qo_ref[:, sl] = q


def _attn_prep(z, tabs, head_major):
    bsz, t, _ = z.shape
    tm = min(t, 512)
    cq, ck, cv = OFF_AQ // ATT_DIM, OFF_AK // ATT_DIM, OFF_AV // ATT_DIM
    tok = pl.BlockSpec((None, tm, ATT_DIM), lambda b, i: (b, i, 0))
    tab = pl.BlockSpec((tm, LANES), lambda b, i: (i, 0))
    tok_shape = jax.ShapeDtypeStruct((bsz, t, ATT_DIM), F32)
    if head_major:
        nblk = tm // MOBA_BLOCK
        hm = pl.BlockSpec((None, ATT_HEADS, tm, ATT_HD), lambda b, i: (b, 0, i, 0))
        tr = pl.BlockSpec((None, ATT_HEADS, nblk, ATT_HD, MOBA_BLOCK), lambda b, i: (b, 0, i, 0, 0))
        trv = pl.BlockSpec((None, ATT_HEADS, nblk, VT_ROWS, MOBA_BLOCK), lambda b, i: (b, 0, i, 0, 0))
        nb = t // MOBA_BLOCK
        out_specs = [tr, hm, trv, tok, tok]
        out_shape = [jax.ShapeDtypeStruct((bsz, ATT_HEADS, nb, ATT_HD, MOBA_BLOCK), F32),
                     jax.ShapeDtypeStruct((bsz, ATT_HEADS, t, ATT_HD), BF16),
                     jax.ShapeDtypeStruct((bsz, ATT_HEADS, nb, VT_ROWS, MOBA_BLOCK), BF16), tok_shape, tok_shape]
    else:
        out_specs = [tok, tok, tok]
        out_shape = [tok_shape, tok_shape, tok_shape]
    return pl.pallas_call(
        functools.partial(_attn_prep_kernel, head_major=head_major, tm=tm),
        grid=(bsz, t // tm),
        in_specs=[pl.BlockSpec((None, tm, ATT_DIM), lambda b, i: (b, i, cq)),
                  pl.BlockSpec((None, tm, ATT_DIM), lambda b, i: (b, i, ck)),
                  pl.BlockSpec((None, tm, ATT_DIM), lambda b, i: (b, i, cv)),
                  tab, tab, tab],
        out_specs=out_specs,
        out_shape=out_shape,
        compiler_params=_params(("arbitrary", "arbitrary")),
    )(z, z, z, *tabs)


def _kmean_kernel(k_ref, o_ref, *, nblk):
    for j in range(nblk):
        o_ref[j:j + 1, :] = jnp.mean(k_ref[j * MOBA_BLOCK:(j + 1) * MOBA_BLOCK, :], axis=0, keepdims=True)


def _kmean(k_tok):
    bsz, t, _ = k_tok.shape
    rows = min(t, 8 * MOBA_BLOCK)
    nblk = rows // MOBA_BLOCK
    return pl.pallas_call(
        functools.partial(_kmean_kernel, nblk=nblk),
        grid=(bsz, t // rows),
        in_specs=[pl.BlockSpec((None, rows, ATT_DIM), lambda b, i: (b, i, 0))],
        out_specs=pl.BlockSpec((None, nblk, ATT_DIM), lambda b, i: (b, i, 0)),
        out_shape=jax.ShapeDtypeStruct((bsz, t // MOBA_BLOCK, ATT_DIM), F32),
        compiler_params=_params(("arbitrary", "arbitrary")),
    )(k_tok)


def _moba_kernel(qt_ref, k_ref, vt_ref, km_ref, o_ref, s_scr, *, nb):
    i = pl.program_id(2)
    blk = MOBA_BLOCK
    key = lax.broadcasted_iota(jnp.int32, (blk, blk), 0)
    qry = lax.broadcasted_iota(jnp.int32, (blk, blk), 1)
    bid = lax.broadcasted_iota(jnp.int32, (nb, blk), 0)
    qtbs, pickss, init = [], [], []

    def scores(j):
        off = pl.multiple_of(j * blk, blk)
        return [jnp.dot(k_ref[hh, pl.ds(off, blk), :], qtbs[hh], preferred_element_type=F32) for hh in range(2)]

    for hh in range(2):
        qt = qt_ref[hh]
        gate = jnp.dot(km_ref[hh], qt, precision=lax.Precision.HIGHEST, preferred_element_type=F32)
        gate = jnp.where(bid < i, gate, NEG_INF)
        picks = []
        for r in range(MOBA_TOPK):
            mx = jnp.max(gate, axis=0, keepdims=True)
            ix = jnp.min(jnp.where(gate == mx, bid, nb), axis=0, keepdims=True)
            gate = jnp.where(bid == ix, NEG_INF, gate)
            picks.append(jnp.where(r < i, ix, -1))
        pickss.append(picks)
        qtbs.append((qt * LOG2E).astype(BF16))

    for hh, s in enumerate(scores(i)):
        s = jnp.where(key <= qry, s, NEG_INF)
        m0 = jnp.max(s, axis=0, keepdims=True)
        p = jnp.exp2(s - m0)
        init.append((m0, jnp.dot(vt_ref[hh, i], p.astype(BF16), preferred_element_type=F32)))

    for hh, s in enumerate(scores(0)):
        s_scr[0, hh] = s

    def step(j, carry, cur):
        nxt = scores(jnp.minimum(j + 1, nb - 1))
        new = []
        for hh in range(2):
            m, acc = carry[hh]
            picks = pickss[hh]
            live = (picks[0] == j) | (picks[1] == j) | (picks[2] == j)
            sj = s_scr[cur, hh]
            m_new = jnp.maximum(m, jnp.where(live, jnp.max(sj, axis=0, keepdims=True), NEG_INF))
            alpha = jnp.exp2(m - m_new)
            pj = jnp.exp2(sj - jnp.where(live, m_new, jnp.inf))
            acc = alpha * acc + jnp.dot(vt_ref[hh, j], pj.astype(BF16), preferred_element_type=F32)
            new.append((m_new, acc))
        for hh in range(2):
            s_scr[1 - cur, hh] = nxt[hh]
        return tuple(new)

    def pair(t, carry):
        return step(2 * t + 1, step(2 * t, carry, 0), 1)

    fin = lax.fori_loop(0, (i + 1) // 2, pair, tuple(init))
    outs = [acc[0:ATT_HD] / acc[ATT_HD:ATT_HD + 1] for (_, acc) in fin]
    o_ref[...] = jnp.concatenate(outs, axis=0).T


def _moba(qt, k_hm, vt, kmean_hm):
    bsz, nh, nb, hd, blk = qt.shape
    t = nb * blk
    return pl.pallas_call(
        functools.partial(_moba_kernel, nb=nb),
        grid=(bsz, nh // 2, nb),
        in_specs=[pl.BlockSpec((None, 2, None, hd, blk), lambda b, h, i: (b, h, i, 0, 0)),
                  pl.BlockSpec((None, 2, t, hd), lambda b, h, i: (b, h, 0, 0)),
                  pl.BlockSpec((None, 2, nb, vt.shape[3], blk), lambda b, h, i: (b, h, 0, 0, 0)),
                  pl.BlockSpec((None, 2, nb, hd), lambda b, h, i: (b, h, 0, 0))],
        out_specs=pl.BlockSpec((None, blk, 2 * hd), lambda b, h, i: (b, i, h)),
        out_shape=jax.ShapeDtypeStruct((bsz, t, nh * hd), F32),
        scratch_shapes=[pltpu.VMEM((2, 2, blk, blk), F32)],
        compiler_params=_params(("arbitrary", "arbitrary", "arbitrary")),
    )(qt, k_hm, vt, kmean_hm)


def _rope_ret(x, c, s):
    w = x.shape[1]
    d = lax.broadcasted_iota(jnp.int32, x.shape, 1) % RET_DK
    half = RET_DK // 2
    partner = jnp.where(d < half, pltpu.roll(x, w - half, 1), pltpu.roll(x, half, 1))
    return x * c + partner * s


def _ret_decay(h):
    return 1.0 - 2.0 ** (-5.0 - h)


def _retention_kernel(q_ref, k_ref, v_ref, g_ref, c_ref, s_ref, dm_ref, qd_ref, kd_ref, o_ref, st_ref, state, *, tm):
    ch = RET_CHUNK

    @pl.when(pl.program_id(1) == 0)
    def _():
        state[...] = jnp.zeros_like(state)

    q = _rope_ret(q_ref[...], c_ref[...], s_ref[...])
    k = _rope_ret(k_ref[...], c_ref[...], s_ref[...]) * (RET_DK ** -0.5)
    kt = k.T
    for c in range(tm // ch):
        rows = slice(c * ch, (c + 1) * ch)
        for h in range(RET_HEADS):
            qc = q[rows, h * RET_DK:(h + 1) * RET_DK]
            ktc = kt[h * RET_DK:(h + 1) * RET_DK, rows]
            vc = v_ref[rows, h * RET_DV:(h + 1) * RET_DV].astype(BF16)
            s_old = state[h]
            inner = jnp.dot(qc.astype(BF16), ktc.astype(BF16), preferred_element_type=F32) * dm_ref[h]
            o = jnp.dot(inner.astype(BF16), vc, preferred_element_type=F32)
            o = o + jnp.dot((qc * qd_ref[h]).astype(BF16), s_old.astype(BF16), preferred_element_type=F32)
            c_dec = _ret_decay(h) ** ch
            state[h] = c_dec * s_old + jnp.dot((ktc * kd_ref[h]).astype(BF16), vc, preferred_element_type=F32)
            gate = g_ref[rows, h * RET_DV:(h + 1) * RET_DV]
            o_ref[rows, h * RET_DV:(h + 1) * RET_DV] = _ln(o) * _silu(gate)
    st_ref[...] = state[...]


def _retention(z, tabs, consts):
    bsz, t, _ = z.shape
    tm = min(t, 512)
    dmat, qdec, kdec = consts
    const3 = lambda a: pl.BlockSpec(a.shape, lambda b, i: (0, 0, 0))
    tab = pl.BlockSpec((tm, RET_QK_DIM), lambda b, i: (i, 0))
    return pl.pallas_call(
        functools.partial(_retention_kernel, tm=tm),
        grid=(bsz, t // tm),
        in_specs=[pl.BlockSpec((None, tm, RET_QK_DIM), lambda b, i: (b, i, OFF_RQ // RET_QK_DIM)),
                  pl.BlockSpec((None, tm, RET_QK_DIM), lambda b, i: (b, i, OFF_RK // RET_QK_DIM)),
                  pl.BlockSpec((None, tm, RET_V_DIM), lambda b, i: (b, i, OFF_RV // RET_V_DIM)),
                  pl.BlockSpec((None, tm, RET_V_DIM), lambda b, i: (b, i, OFF_RG // RET_V_DIM)),
                  tab, tab, const3(dmat), const3(qdec), const3(kdec)],
        out_specs=[pl.BlockSpec((None, tm, RET_V_DIM), lambda b, i: (b, i, 0)),
                   pl.BlockSpec((None, RET_HEADS, RET_DK, RET_DV), lambda b, i: (b, 0, 0, 0))],
        out_shape=[jax.ShapeDtypeStruct((bsz, t, RET_V_DIM), F32),
                   jax.ShapeDtypeStruct((bsz, RET_HEADS, RET_DK, RET_DV), F32)],
        scratch_shapes=[pltpu.VMEM((RET_HEADS, RET_DK, RET_DV), F32)],
        compiler_params=_params(("arbitrary", "arbitrary")),
    )(z, z, z, z, *tabs, dmat, qdec, kdec)


def _ret_consts():
    ch = RET_CHUNK
    log_g = jnp.log(1.0 - 2.0 ** (-5.0 - jnp.arange(RET_HEADS, dtype=F32)))
    i = jnp.arange(ch, dtype=F32)
    diff = i[:, None] - i[None, :]
    dmat = jnp.where(diff >= 0, jnp.exp(jnp.maximum(diff, 0.0)[None] * log_g[:, None, None]), 0.0)
    qdec = jnp.exp((i + 1.0)[None, :] * log_g[:, None])[:, :, None]
    kdec = jnp.exp((ch - 1.0 - i)[None, :] * log_g[:, None])[:, None, :]
    return dmat, qdec, kdec


def _row_to_col(row):
    n = row.shape[1]
    eye = lax.broadcasted_iota(jnp.int32, (n, n), 0) == lax.broadcasted_iota(jnp.int32, (n, n), 1)
    return jnp.sum(jnp.where(eye, jnp.broadcast_to(row, (n, n)), 0.0), axis=1, keepdims=True)


def _ret_step_kernel(q_ref, k_ref, v_ref, g_ref, c_ref, s_ref, st_ref, o_ref, sn_ref):
    q = _rope_ret(q_ref[...], c_ref[...], s_ref[...])
    k = _rope_ret(k_ref[...], c_ref[...], s_ref[...]) * (RET_DK ** -0.5)
    for h in range(RET_HEADS):
        g = _ret_decay(h)
        qrow = q[:, h * RET_DK:(h + 1) * RET_DK]
        krow = k[:, h * RET_DK:(h + 1) * RET_DK]
        qcol = _row_to_col(qrow)
        kcol = _row_to_col(krow)
        v = v_ref[:, h * RET_DV:(h + 1) * RET_DV]
        s_old = st_ref[h]
        inner = jnp.sum(qrow * krow, axis=1, keepdims=True)
        o = inner * v + jnp.sum((qcol * g) * s_old, axis=0, keepdims=True)
        sn_ref[h] = g * s_old + kcol * v
        o_ref[:, h * RET_DV:(h + 1) * RET_DV] = _ln(o) * _silu(g_ref[:, h * RET_DV:(h + 1) * RET_DV])


def _ret_step(z, tabs, state):
    n = z.shape[0]
    tab = pl.BlockSpec((None, 1, RET_QK_DIM), lambda b: (b, 0, 0))
    st = pl.BlockSpec((None, RET_HEADS, RET_DK, RET_DV), lambda b: (b, 0, 0, 0))
    return pl.pallas_call(
        _ret_step_kernel,
        grid=(n,),
        in_specs=[pl.BlockSpec((None, 1, RET_QK_DIM), lambda b: (b, 0, OFF_RQ // RET_QK_DIM)),
                  pl.BlockSpec((None, 1, RET_QK_DIM), lambda b: (b, 0, OFF_RK // RET_QK_DIM)),
                  pl.BlockSpec((None, 1, RET_V_DIM), lambda b: (b, 0, OFF_RV // RET_V_DIM)),
                  pl.BlockSpec((None, 1, RET_V_DIM), lambda b: (b, 0, OFF_RG // RET_V_DIM)),
                  tab, tab, st],
        out_specs=[pl.BlockSpec((None, 1, RET_V_DIM), lambda b: (b, 0, 0)), st],
        out_shape=[jax.ShapeDtypeStruct((n, 1, RET_V_DIM), F32),
                   jax.ShapeDtypeStruct(state.shape, F32)],
        compiler_params=_params(("arbitrary",)),
    )(z, z, z, z, *tabs, state)


def _merge_kernel(*refs, tm, seq_mode, alpha):
    if seq_mode:
        (x_ref, g1_ref, a_ref, r_ref, cb_ref, cc_ref, cx_ref, gt0_ref, gt1_ref, gt2_ref, cw_ref, wbr_ref, wo_ref,
         lg_ref, lb_ref, o_ref, cs_ref, carry) = refs
    else:
        (x_ref, g1_ref, a_ref, r_ref, cb_ref, cc_ref, cx_ref, gt0_ref, gt1_ref, gt2_ref, cw_ref, wbr_ref, wo_ref,
         lg_ref, lb_ref, p0_ref, p1_ref, o_ref, cs_ref) = refs
    s = cc_ref[...] * cx_ref[...]
    if seq_mode:
        @pl.when(pl.program_id(1) == 0)
        def _():
            carry[...] = jnp.zeros_like(carry)

        rid = lax.broadcasted_iota(jnp.int32, s.shape, 0)
        prev0, prev1 = carry[0:1, :], carry[1:2, :]
        s1 = jnp.where(rid == 0, prev1, pltpu.roll(s, 1, 0))
        s2 = jnp.where(rid == 0, prev0, jnp.where(rid == 1, prev1, pltpu.roll(s, 2, 0)))
        carry[...] = s[tm - 2:tm, :]
        cs_ref[...] = s[tm - 2:tm, :]
    else:
        s2, s1 = p0_ref[...], p1_ref[...]
        cs_ref[...] = s
    y = cw_ref[0:1, :] * s2 + cw_ref[1:2, :] * s1 + cw_ref[2:3, :] * s
    c_out = cb_ref[...] * y
    merged = _sigmoid(gt0_ref[...]) * jnp.dot(a_ref[...].astype(BF16), wbr_ref[0], preferred_element_type=F32)
    merged = merged + _sigmoid(gt1_ref[...]) * jnp.dot(r_ref[...].astype(BF16), wbr_ref[1],
                                                       preferred_element_type=F32)
    merged = merged + _sigmoid(gt2_ref[...]) * jnp.dot(c_out.astype(BF16), wbr_ref[2], preferred_element_type=F32)
    m = jnp.dot(merged.astype(BF16), wo_ref[...], preferred_element_type=F32)
    o_ref[...] = _ln(alpha * x_ref[...] + g1_ref[...] * m) * lg_ref[...] + lb_ref[...]


def _merge(x, g1, a_out, r_out, z, conv_w, w_br, w_o, ln_g, ln_b, alpha, conv_prev=None):
    bsz, t, d = x.shape
    seq_mode = conv_prev is None
    tm = min(t, 256)
    tok512 = lambda cblk: pl.BlockSpec((None, tm, CONV_DIM), lambda b, i: (b, i, cblk))
    gate = lambda n: pl.BlockSpec((None, tm, d), lambda b, i: (b, i, OFF_GATE // d + n))
    full2 = lambda a: pl.BlockSpec(a.shape, lambda b, i: (0, 0))
    in_specs = [pl.BlockSpec((None, tm, d), lambda b, i: (b, i, 0)), _mod_spec(g1, tm, 2),
                tok512(0), tok512(0), tok512(OFF_CB // CONV_DIM), tok512(OFF_CC // CONV_DIM),
                tok512(OFF_CX // CONV_DIM), gate(0), gate(1), gate(2),
                full2(conv_w), pl.BlockSpec(w_br.shape, lambda b, i: (0, 0, 0)), full2(w_o),
                full2(ln_g), full2(ln_b)]
    args = [x, g1, a_out, r_out, z, z, z, z, z, z, conv_w, w_br, w_o, ln_g, ln_b]
    scratch = []
    if seq_mode:
        cs_spec = pl.BlockSpec((None, CONV_W - 1, CONV_DIM), lambda b, i: (b, 0, 0))
        cs_shape = jax.ShapeDtypeStruct((bsz, CONV_W - 1, CONV_DIM), F32)
        scratch = [pltpu.VMEM((CONV_W - 1, CONV_DIM), F32)]
    else:
        in_specs += [tok512(0), tok512(0)]
        args += list(conv_prev)
        cs_spec = tok512(0)
        cs_shape = jax.ShapeDtypeStruct((bsz, t, CONV_DIM), F32)
    return pl.pallas_call(
        functools.partial(_merge_kernel, tm=tm, seq_mode=seq_mode, alpha=alpha),
        grid=(bsz, t // tm),
        in_specs=in_specs,
        out_specs=[pl.BlockSpec((None, tm, d), lambda b, i: (b, i, 0)), cs_spec],
        out_shape=[jax.ShapeDtypeStruct((bsz, t, d), F32), cs_shape],
        scratch_shapes=scratch,
        compiler_params=_params(("arbitrary", "arbitrary")),
    )(*args)


def _ffn_kernel(x_ref, sc_ref, sh_ref, g2_ref, wa_ref, wb_ref, wo_ref, lg_ref, lb_ref, o_ref, u_ref, acc_ref, *,
                alpha):
    f = pl.program_id(2)

    @pl.when(f == 0)
    def _():
        u_ref[...] = (x_ref[...] * (1.0 + sc_ref[...]) + sh_ref[...]).astype(BF16)
        acc_ref[...] = jnp.zeros_like(acc_ref)

    u = u_ref[...]
    a = jnp.dot(u, wa_ref[...], preferred_element_type=F32)
    b = jnp.dot(u, wb_ref[...], preferred_element_type=F32)
    acc_ref[...] += jnp.dot((_silu(a) * b).astype(BF16), wo_ref[...], preferred_element_type=F32)

    @pl.when(f == pl.num_programs(2) - 1)
    def _():
        o_ref[...] = _ln(alpha * x_ref[...] + g2_ref[...] * acc_ref[...]) * lg_ref[...] + lb_ref[...]


def _ffn(x, sc, sh, g2, w_in, w_out, ln_g, ln_b, alpha):
    bsz, t, d = x.shape
    tm = min(t, 512)
    tf = D_FF // 2
    nf = D_FF // tf
    full2 = lambda a: pl.BlockSpec(a.shape, lambda b, i, f: (0, 0))
    return pl.pallas_call(
        functools.partial(_ffn_kernel, alpha=alpha),
        grid=(bsz, t // tm, nf),
        in_specs=[pl.BlockSpec((None, tm, d), lambda b, i, f: (b, i, 0)),
                  _mod_spec(sc, tm, 3), _mod_spec(sh, tm, 3), _mod_spec(g2, tm, 3),
                  pl.BlockSpec((d, tf), lambda b, i, f: (0, f)),
                  pl.BlockSpec((d, tf), lambda b, i, f: (0, nf + f)),
                  pl.BlockSpec((tf, d), lambda b, i, f: (f, 0)),
                  full2(ln_g), full2(ln_b)],
        out_specs=pl.BlockSpec((None, tm, d), lambda b, i, f: (b, i, 0)),
        out_shape=jax.ShapeDtypeStruct((bsz, t, d), F32),
        scratch_shapes=[pltpu.VMEM((tm, d), BF16), pltpu.VMEM((tm, d), F32)],
        compiler_params=_params(("arbitrary", "arbitrary", "arbitrary")),
    )(x, sc, sh, g2, w_in, w_in, w_out, ln_g, ln_b)


def _dec_scan_kernel(pt_ref, q_ref, ck_ref, o_ref, buf, ksum, sem, *, layer, n_pages, cp):
    b = pl.program_id(0)
    n_chunks = n_pages // cp
    cb = cp // 2

    def copies(c, slot):
        return [pltpu.make_async_copy(ck_ref.at[layer, pt_ref[b * n_pages + c * cp + p]], buf.at[slot, p],
                                      sem.at[slot]) for p in range(cp)]

    for cpy in copies(0, 0):
        cpy.start()

    def body(c, carry):
        slot = c % 2

        @pl.when(c + 1 < n_chunks)
        def _():
            for cpy in copies(c + 1, 1 - slot):
                cpy.start()

        for cpy in copies(c, slot):
            cpy.wait()
        for jb in range(cb):
            pages = buf[slot, 2 * jb] + buf[slot, 2 * jb + 1]
            ksum[c * cb + jb] = jnp.sum(pages, axis=0)
        return carry

    lax.fori_loop(0, n_chunks, body, 0)

    nbp = n_pages // 2
    gate = jnp.sum((ksum[...] * (1.0 / MOBA_BLOCK)) * q_ref[...], axis=-1, keepdims=True)
    bid = lax.broadcasted_iota(jnp.int32, (nbp, ATT_HEADS, 1), 0)
    lane = lax.broadcasted_iota(jnp.int32, (ATT_HEADS, LANES), 1)
    res = jnp.zeros((ATT_HEADS, LANES), jnp.int32)
    for r in range(min(MOBA_TOPK, nbp)):
        mx = jnp.max(gate, axis=0, keepdims=True)
        ix = jnp.min(jnp.where(gate == mx, bid, nbp), axis=0, keepdims=True)
        gate = jnp.where(bid == ix, NEG_INF, gate)
        res = jnp.where(lane == r, ix[0], res)
    o_ref[...] = res


def _dec_scan(page_table, q_hd, cache_k, layer):
    n, n_pages = page_table.shape
    page_size = cache_k.shape[2]
    assert page_size * 2 == MOBA_BLOCK
    cp = min(8, n_pages)
    nbp = n_pages // 2
    grid_spec = pltpu.PrefetchScalarGridSpec(
        num_scalar_prefetch=1,
        grid=(n,),
        in_specs=[pl.BlockSpec((None, ATT_HEADS, ATT_HD), lambda b, pt: (b, 0, 0)),
                  pl.BlockSpec(memory_space=pl.ANY)],
        out_specs=pl.BlockSpec((None, ATT_HEADS, LANES), lambda b, pt: (b, 0, 0)),
        scratch_shapes=[pltpu.VMEM((2, cp, page_size, ATT_HEADS, ATT_HD), F32),
                        pltpu.VMEM((nbp, ATT_HEADS, ATT_HD), F32),
                        pltpu.SemaphoreType.DMA((2,))],
    )
    return pl.pallas_call(
        functools.partial(_dec_scan_kernel, layer=layer, n_pages=n_pages, cp=cp),
        grid_spec=grid_spec,
        out_shape=jax.ShapeDtypeStruct((n, ATT_HEADS, LANES), jnp.int32),
        compiler_params=_params(("arbitrary",)),
    )(page_table.reshape(-1), q_hd, cache_k)


def _dec_attend_kernel(pt_ref, pick_ref, q_ref, kn_ref, vn_ref, ck_ref, cv_ref, o_ref, kb, vb, sem, *,
                       layer, n_pages, n_sel, page_size):
    b = pl.program_id(0)

    def copies(h, slot):
        out = []
        for r in range(n_sel):
            blk = pick_ref[(b * ATT_HEADS + h) * n_sel + r]
            for p in range(2):
                page = pt_ref[b * n_pages + 2 * blk + p]
                dst = pl.ds((2 * r + p) * page_size, page_size)
                out.append(pltpu.make_async_copy(ck_ref.at[layer, page], kb.at[slot, dst], sem.at[2 * slot]))
                out.append(pltpu.make_async_copy(cv_ref.at[layer, page], vb.at[slot, dst], sem.at[2 * slot + 1]))
        return out

    q = q_ref[...]
    s_new = jnp.sum(kn_ref[...] * q, axis=-1, keepdims=True)
    for cpy in copies(0, 0):
        cpy.start()
    for h in range(ATT_HEADS):
        slot = h % 2
        if h + 1 < ATT_HEADS:
            for cpy in copies(h + 1, 1 - slot):
                cpy.start()
        for cpy in copies(h, slot):
            cpy.wait()
        s = jnp.sum(kb[slot] * q, axis=-1, keepdims=True)
        m = jnp.maximum(jnp.max(s, axis=0), s_new)
        p = jnp.exp(s - m)
        p_new = jnp.exp(s_new - m)
        den = jnp.sum(p, axis=0) + p_new
        num = jnp.sum(p * vb[slot], axis=0) + p_new * vn_ref[...]
        o_ref[h:h + 1, :] = (num / den)[h:h + 1, :]


def _dec_attend(page_table, picks, q_hd, k_new, v_new, cache_k, cache_v, layer):
    n, n_pages = page_table.shape
    page_size = cache_k.shape[2]
    n_sel = min(MOBA_TOPK, n_pages // 2)
    tok = pl.BlockSpec((None, ATT_HEADS, ATT_HD), lambda b, pt, pk: (b, 0, 0))
    n_keys = n_sel * 2 * page_size
    grid_spec = pltpu.PrefetchScalarGridSpec(
        num_scalar_prefetch=2,
        grid=(n,),
        in_specs=[tok, tok, tok, pl.BlockSpec(memory_space=pl.ANY), pl.BlockSpec(memory_space=pl.ANY)],
        out_specs=tok,
        scratch_shapes=[pltpu.VMEM((2, n_keys, ATT_HEADS, ATT_HD), F32),
                        pltpu.VMEM((2, n_keys, ATT_HEADS, ATT_HD), F32),
                        pltpu.SemaphoreType.DMA((4,))],
    )
    return pl.pallas_call(
        functools.partial(_dec_attend_kernel, layer=layer, n_pages=n_pages, n_sel=n_sel, page_size=page_size),
        grid_spec=grid_spec,
        out_shape=jax.ShapeDtypeStruct((n, ATT_HEADS, ATT_HD), F32),
        compiler_params=_params(("arbitrary",)),
    )(page_table.reshape(-1), picks, q_hd, k_new, v_new, cache_k, cache_v)


def _att_rope_tables(pos):
    half = ROPE_DIM // 2
    inv = ROPE_THETA ** (-2.0 * jnp.arange(half, dtype=F32) / ROPE_DIM)
    ang = pos.astype(F32)[:, None] * inv[None, :]
    cos, sin = jnp.cos(ang), jnp.sin(ang)
    t = pos.shape[0]
    rest = ATT_HD - ROPE_DIM
    ca = jnp.concatenate([cos, cos, jnp.ones((t, rest), F32)], axis=1)
    cm = jnp.concatenate([-sin, jnp.zeros((t, half + rest), F32)], axis=1)
    cp = jnp.concatenate([jnp.zeros((t, half), F32), sin, jnp.zeros((t, rest), F32)], axis=1)
    rep = LANES // ATT_HD
    return tuple(jnp.tile(a, (1, rep)) for a in (ca, cm, cp))


def _ret_rope_tables(pos):
    half = RET_DK // 2
    inv = RET_THETA ** (-2.0 * jnp.arange(half, dtype=F32) / RET_DK)
    ang = pos.astype(F32)[:, None] * inv[None, :]
    cos, sin = jnp.cos(ang), jnp.sin(ang)
    c = jnp.concatenate([cos, cos], axis=1)
    s = jnp.concatenate([-sin, sin], axis=1)
    return jnp.tile(c, (1, RET_HEADS)), jnp.tile(s, (1, RET_HEADS))


def kernel(x_prompt, x_sample, cache_k, cache_v, state_ret, state_conv, page_table, c_prompt, c_sample, w_ada, b_ada,
           w_in, conv_w, w_br, w_o, ln1_g, ln1_b, w_ffn_in, w_ffn_out, ln2_g, ln2_b):
    depth = w_in.shape[0]
    alpha = (2 * depth) ** 0.25
    bsz, seq, d = x_prompt.shape
    nd = x_sample.shape[0]
    assert x_sample.shape[1] == 1, "one new token per decode sequence"
    n_pool, page_size = cache_k.shape[1], cache_k.shape[2]
    past_len = page_table.shape[1] * page_size

    ada = _ada(jnp.concatenate([c_prompt, c_sample], axis=0), w_ada, b_ada)
    w_in_b = jnp.concatenate([w_in[:, :, N_MIX:], w_in[:, :, :N_MIX]], axis=2).astype(BF16)
    w_br_b, w_o_b = w_br.astype(BF16), w_o.astype(BF16)
    w_fi_b, w_fo_b = w_ffn_in.astype(BF16), w_ffn_out.astype(BF16)

    pos_p = jnp.arange(seq, dtype=jnp.int32)
    pos_s = jnp.full((nd,), past_len, jnp.int32)
    att_tab_p, ret_tab_p = _att_rope_tables(pos_p), _ret_rope_tables(pos_p)
    att_tab_s = _att_rope_tables(pos_s)
    ret_tab_s = tuple(a.reshape(nd, 1, RET_QK_DIM) for a in _ret_rope_tables(pos_s))
    ret_consts = _ret_consts()

    xp = x_prompt
    xs = x_sample.reshape(1, nd, d)
    outs = {k: [] for k in ("kp", "vp", "ks", "vs", "rp", "rs", "cp", "cs")}
    for l in range(depth):
        mods_p = [m.reshape(bsz, 1, d) for m in jnp.split(ada[l, :bsz], 6, axis=-1)]
        mods_s = [m.reshape(1, nd, d) for m in jnp.split(ada[l, bsz:], 6, axis=-1)]
        lg1, lb1 = ln1_g[l].reshape(1, d), ln1_b[l].reshape(1, d)
        lg2, lb2 = ln2_g[l].reshape(1, d), ln2_b[l].reshape(1, d)

        sh1, sc1, g1, sh2, sc2, g2 = mods_p
        z = _inproj(xp, sc1, sh1, w_in_b[l])
        q_t, k_hm, v_t, k_tok, v_tok = _attn_prep(z, att_tab_p, head_major=True)
        kmean = _kmean(k_tok)
        kmean_hm = kmean.reshape(bsz, -1, ATT_HEADS, ATT_HD).transpose(0, 2, 1, 3)
        a_out = _moba(q_t, k_hm, v_t, kmean_hm)
        r_out, r_state = _retention(z, ret_tab_p, ret_consts)
        xp, c_state = _merge(xp, g1, a_out, r_out, z, conv_w[l], w_br_b[l], w_o_b[l], lg1, lb1, alpha)
        xp = _ffn(xp, sc2, sh2, g2, w_fi_b[l], w_fo_b[l], lg2, lb2, alpha)
        outs["kp"].append(k_tok.reshape(bsz, seq, ATT_HEADS, ATT_HD))
        outs["vp"].append(v_tok.reshape(bsz, seq, ATT_HEADS, ATT_HD))
        outs["rp"].append(r_state)
        outs["cp"].append(c_state)

        sh1, sc1, g1, sh2, sc2, g2 = mods_s
        z = _inproj(xs, sc1, sh1, w_in_b[l])
        q_tok, k_new, v_new = _attn_prep(z, att_tab_s, head_major=False)
        q_tok, k_new, v_new = (a.reshape(nd, ATT_HEADS, ATT_HD) for a in (q_tok, k_new, v_new))
        picks = _dec_scan(page_table, q_tok, cache_k, l)
        n_sel = min(MOBA_TOPK, page_table.shape[1] // 2)
        a_out = _dec_attend(page_table, picks[:, :, :n_sel].reshape(-1), q_tok, k_new, v_new, cache_k, cache_v, l)
        r_out, r_state = _ret_step(z.reshape(nd, 1, IN_DIM), ret_tab_s, state_ret[l])
        prev = (state_conv[l][:, 0, :].reshape(1, nd, CONV_DIM), state_conv[l][:, 1, :].reshape(1, nd, CONV_DIM))
        xs, s_new = _merge(xs, g1, a_out.reshape(1, nd, ATT_DIM), r_out.reshape(1, nd, RET_V_DIM), z, conv_w[l],
                           w_br_b[l], w_o_b[l], lg1, lb1, alpha, conv_prev=prev)
        xs = _ffn(xs, sc2, sh2, g2, w_fi_b[l], w_fo_b[l], lg2, lb2, alpha)
        outs["ks"].append(k_new.reshape(nd, 1, ATT_HEADS, ATT_HD))
        outs["vs"].append(v_new.reshape(nd, 1, ATT_HEADS, ATT_HD))
        outs["rs"].append(r_state)
        outs["cs"].append(jnp.stack([state_conv[l][:, 1, :], s_new.reshape(nd, CONV_DIM)], axis=1))

    st = {k: jnp.stack(v) for k, v in outs.items()}
    return (xp, xs.reshape(nd, 1, d), st["kp"], st["vp"], st["ks"], st["vs"], st["rp"], st["rs"], st["cp"], st["cs"])
```

```python
import functools

import jax
import jax.numpy as jnp
from jax import lax
from jax.experimental import pallas as pl
from jax.experimental.pallas import tpu as pltpu

F32 = jnp.float32
BF16 = jnp.bfloat16

D_MODEL = 1024
ATT_HEADS = 8
ATT_HD = 64
ATT_DIM = ATT_HEADS * ATT_HD
ROPE_DIM = ATT_HD // 4
ROPE_THETA = 500000.0
MOBA_BLOCK = 256
MOBA_TOPK = 3
RET_HEADS = 4
RET_DK = 64
RET_DV = 128
RET_QK_DIM = RET_HEADS * RET_DK
RET_V_DIM = RET_HEADS * RET_DV
RET_THETA = 10000.0
RET_CHUNK = 128
CONV_DIM = 512
CONV_W = 3
N_BRANCH = 3
D_FF = 2816
LN_EPS = 1e-5
N_GATE = N_BRANCH * D_MODEL
N_MIX = 3 * ATT_DIM + 2 * RET_QK_DIM + 2 * RET_V_DIM + 3 * CONV_DIM
IN_DIM = N_MIX + N_GATE

OFF_GATE = 0
OFF_AQ = N_GATE
OFF_AK = OFF_AQ + ATT_DIM
OFF_AV = OFF_AK + ATT_DIM
OFF_RQ = OFF_AV + ATT_DIM
OFF_RK = OFF_RQ + RET_QK_DIM
OFF_RV = OFF_RK + RET_QK_DIM
OFF_RG = OFF_RV + RET_V_DIM
OFF_CB = OFF_RG + RET_V_DIM
OFF_CC = OFF_CB + CONV_DIM
OFF_CX = OFF_CC + CONV_DIM

LANES = 128
BF16_SUBLANES = 16
VT_ROWS = ATT_HD + BF16_SUBLANES
LOG2E = 1.4426950408889634
VMEM_LIMIT = 48 * 1024 * 1024
NEG_INF = float("-inf")


def _params(semantics, vmem=VMEM_LIMIT):
    return pltpu.CompilerParams(dimension_semantics=semantics, vmem_limit_bytes=vmem)


def _ln(y):
    mu = jnp.mean(y, axis=-1, keepdims=True)
    yc = y - mu
    return yc * lax.rsqrt(jnp.mean(yc * yc, axis=-1, keepdims=True) + LN_EPS)


def _sigmoid(x):
    return 1.0 / (1.0 + jnp.exp(-x))


def _silu(x):
    return x * _sigmoid(x)


def _mod_spec(mod, tm, ngrid):
    per_row = mod.shape[1] != 1
    rows = tm if per_row else 1
    if ngrid == 2:
        return pl.BlockSpec((None, rows, mod.shape[2]), lambda b, i: (b, i if per_row else 0, 0))
    return pl.BlockSpec((None, rows, mod.shape[2]), lambda b, i, j: (b, i if per_row else 0, 0))


def _ada_kernel(c_ref, w_ref, b_ref, o_ref):
    cond = _silu(c_ref[...]).astype(BF16)
    o_ref[...] = jnp.dot(cond, w_ref[...].astype(BF16), preferred_element_type=F32) + b_ref[...]


def _ada(c, w_ada, b_ada):
    depth, d, n = w_ada.shape
    rows = c.shape[0]
    tn = 1536
    return pl.pallas_call(
        _ada_kernel,
        grid=(depth, n // tn),
        in_specs=[pl.BlockSpec((rows, d), lambda l, j: (0, 0)),
                  pl.BlockSpec((None, d, tn), lambda l, j: (l, 0, j)),
                  pl.BlockSpec((None, 1, tn), lambda l, j: (l, 0, j))],
        out_specs=pl.BlockSpec((None, rows, tn), lambda l, j: (l, 0, j)),
        out_shape=jax.ShapeDtypeStruct((depth, rows, n), F32),
        compiler_params=_params(("arbitrary", "arbitrary")),
    )(c, w_ada, b_ada.reshape(depth, 1, n))


def _inproj_kernel(x_ref, sc_ref, sh_ref, w_ref, o_ref, u_ref):
    @pl.when(pl.program_id(2) == 0)
    def _():
        u_ref[...] = (x_ref[...] * (1.0 + sc_ref[...]) + sh_ref[...]).astype(BF16)

    o_ref[...] = jnp.dot(u_ref[...], w_ref[...], preferred_element_type=F32)


def _inproj(x, sc, sh, w):
    bsz, t, d = x.shape
    n = w.shape[1]
    tm = min(t, 1024)
    tn = 1536
    return pl.pallas_call(
        _inproj_kernel,
        grid=(bsz, t // tm, n // tn),
        in_specs=[pl.BlockSpec((None, tm, d), lambda b, i, j: (b, i, 0)),
                  _mod_spec(sc, tm, 3), _mod_spec(sh, tm, 3),
                  pl.BlockSpec((d, tn), lambda b, i, j: (0, j))],
        out_specs=pl.BlockSpec((None, tm, tn), lambda b, i, j: (b, i, j)),
        out_shape=jax.ShapeDtypeStruct((bsz, t, n), F32),
        scratch_shapes=[pltpu.VMEM((tm, d), BF16)],
        compiler_params=_params(("arbitrary", "arbitrary", "arbitrary")),
    )(x, sc, sh, w)


def _rope_att(x, ca, cm, cp):
    return x * ca + pltpu.roll(x, LANES - ROPE_DIM // 2, 1) * cm + pltpu.roll(x, ROPE_DIM // 2, 1) * cp


def _attn_prep_kernel(q_ref, k_ref, v_ref, ca_ref, cm_ref, cp_ref, *out_refs, head_major, tm):
    ca, cm, cp = ca_ref[...], cm_ref[...], cp_ref[...]
    scale = ATT_HD ** -0.5
    if head_major:
        qt_ref, kh_ref, vt_ref, ko_ref, vo_ref = out_refs
    else:
        qo_ref, ko_ref, vo_ref = out_refs
    vo_ref[...] = v_ref[...]
    for c in range(ATT_DIM // LANES):
        sl = slice(c * LANES, (c + 1) * LANES)
        q = _rope_att(q_ref[:, sl], ca, cm, cp) * scale
        k = _rope_att(k_ref[:, sl], ca, cm, cp)
        ko_ref[:, sl] = k
        if head_major:
            qt = q.T
            vt = v_ref[:, sl].T.astype(BF16)
            for hh in range(2):
                hs = slice(hh * ATT_HD, (hh + 1) * ATT_HD)
                kh_ref[2 * c + hh] = k[:, hs].astype(BF16)
                for j in range(tm // MOBA_BLOCK):
                    cols = slice(j * MOBA_BLOCK, (j + 1) * MOBA_BLOCK)
                    qt_ref[2 * c + hh, j] = qt[hs, cols]
                    vt_ref[2 * c + hh, j, 0:ATT_HD, :] = vt[hs, cols]
                    vt_ref[2 * c + hh, j, ATT_HD:VT_ROWS, :] = jnp.ones((VT_ROWS - ATT_HD, MOBA_BLOCK), BF16)
        else:
            qo_ref[:, sl] = q


def _attn_prep(z, tabs, head_major):
    bsz, t, _ = z.shape
    tm = min(t, 512)
    cq, ck, cv = OFF_AQ // ATT_DIM, OFF_AK // ATT_DIM, OFF_AV // ATT_DIM
    tok = pl.BlockSpec((None, tm, ATT_DIM), lambda b, i: (b, i, 0))
    tab = pl.BlockSpec((tm, LANES), lambda b, i: (i, 0))
    tok_shape = jax.ShapeDtypeStruct((bsz, t, ATT_DIM), F32)
    if head_major:
        nblk = tm // MOBA_BLOCK
        hm = pl.BlockSpec((None, ATT_HEADS, tm, ATT_HD), lambda b, i: (b, 0, i, 0))
        tr = pl.BlockSpec((None, ATT_HEADS, nblk, ATT_HD, MOBA_BLOCK), lambda b, i: (b, 0, i, 0, 0))
        trv = pl.BlockSpec((None, ATT_HEADS, nblk, VT_ROWS, MOBA_BLOCK), lambda b, i: (b, 0, i, 0, 0))
        nb = t // MOBA_BLOCK
        out_specs = [tr, hm, trv, tok, tok]
        out_shape = [jax.ShapeDtypeStruct((bsz, ATT_HEADS, nb, ATT_HD, MOBA_BLOCK), F32),
                     jax.ShapeDtypeStruct((bsz, ATT_HEADS, t, ATT_HD), BF16),
                     jax.ShapeDtypeStruct((bsz, ATT_HEADS, nb, VT_ROWS, MOBA_BLOCK), BF16), tok_shape, tok_shape]
    else:
        out_specs = [tok, tok, tok]
        out_shape = [tok_shape, tok_shape, tok_shape]
    return pl.pallas_call(
        functools.partial(_attn_prep_kernel, head_major=head_major, tm=tm),
        grid=(bsz, t // tm),
        in_specs=[pl.BlockSpec((None, tm, ATT_DIM), lambda b, i: (b, i, cq)),
                  pl.BlockSpec((None, tm, ATT_DIM), lambda b, i: (b, i, ck)),
                  pl.BlockSpec((None, tm, ATT_DIM), lambda b, i: (b, i, cv)),
                  tab, tab, tab],
        out_specs=out_specs,
        out_shape=out_shape,
        compiler_params=_params(("arbitrary", "arbitrary")),
    )(z, z, z, *tabs)


def _kmean_kernel(k_ref, o_ref, *, nblk):
    for j in range(nblk):
        o_ref[j:j + 1, :] = jnp.mean(k_ref[j * MOBA_BLOCK:(j + 1) * MOBA_BLOCK, :], axis=0, keepdims=True)


def _kmean(k_tok):
    bsz, t, _ = k_tok.shape
    rows = min(t, 8 * MOBA_BLOCK)
    nblk = rows // MOBA_BLOCK
    return pl.pallas_call(
        functools.partial(_kmean_kernel, nblk=nblk),
        grid=(bsz, t // rows),
        in_specs=[pl.BlockSpec((None, rows, ATT_DIM), lambda b, i: (b, i, 0))],
        out_specs=pl.BlockSpec((None, nblk, ATT_DIM), lambda b, i: (b, i, 0)),
        out_shape=jax.ShapeDtypeStruct((bsz, t // MOBA_BLOCK, ATT_DIM), F32),
        compiler_params=_params(("arbitrary", "arbitrary")),
    )(k_tok)


def _moba_kernel(qt_ref, k_ref, vt_ref, km_ref, o_ref, s_scr, *, nb):
    i = pl.program_id(2)
    blk = MOBA_BLOCK
    key = lax.broadcasted_iota(jnp.int32, (blk, blk), 0)
    qry = lax.broadcasted_iota(jnp.int32, (blk, blk), 1)
    bid = lax.broadcasted_iota(jnp.int32, (nb, blk), 0)
    qtbs, pickss, init = [], [], []

    def scores(j):
        off = pl.multiple_of(j * blk, blk)
        return [jnp.dot(k_ref[hh, pl.ds(off, blk), :], qtbs[hh], preferred_element_type=F32) for hh in range(2)]

    for hh in range(2):
        qt = qt_ref[hh]
        gate = jnp.dot(km_ref[hh], qt, precision=lax.Precision.HIGHEST, preferred_element_type=F32)
        gate = jnp.where(bid < i, gate, NEG_INF)
        picks = []
        for r in range(MOBA_TOPK):
            mx = jnp.max(gate, axis=0, keepdims=True)
            ix = jnp.min(jnp.where(gate == mx, bid, nb), axis=0, keepdims=True)
            gate = jnp.where(bid == ix, NEG_INF, gate)
            picks.append(jnp.where(r < i, ix, -1))
        pickss.append(picks)
        qtbs.append((qt * LOG2E).astype(BF16))

    for hh, s in enumerate(scores(i)):
        s = jnp.where(key <= qry, s, NEG_INF)
        m0 = jnp.max(s, axis=0, keepdims=True)
        p = jnp.exp2(s - m0)
        init.append((m0, jnp.dot(vt_ref[hh, i], p.astype(BF16), preferred_element_type=F32)))

    for hh, s in enumerate(scores(0)):
        s_scr[0, hh] = s

    def step(j, carry, cur):
        nxt = scores(jnp.minimum(j + 1, nb - 1))
        new = []
        for hh in range(2):
            m, acc = carry[hh]
            picks = pickss[hh]
            live = (picks[0] == j) | (picks[1] == j) | (picks[2] == j)
            sj = s_scr[cur, hh]
            m_new = jnp.maximum(m, jnp.where(live, jnp.max(sj, axis=0, keepdims=True), NEG_INF))
            alpha = jnp.exp2(m - m_new)
            pj = jnp.exp2(sj - jnp.where(live, m_new, jnp.inf))
            acc = alpha * acc + jnp.dot(vt_ref[hh, j], pj.astype(BF16), preferred_element_type=F32)
            new.append((m_new, acc))
        for hh in range(2):
            s_scr[1 - cur, hh] = nxt[hh]
        return tuple(new)

    def pair(t, carry):
        return step(2 * t + 1, step(2 * t, carry, 0), 1)

    fin = lax.fori_loop(0, (i + 1) // 2, pair, tuple(init))
    outs = [acc[0:ATT_HD] / acc[ATT_HD:ATT_HD + 1] for (_, acc) in fin]
    o_ref[...] = jnp.concatenate(outs, axis=0).T


def _moba(qt, k_hm, vt, kmean_hm):
    bsz, nh, nb, hd, blk = qt.shape
    t = nb * blk
    return pl.pallas_call(
        functools.partial(_moba_kernel, nb=nb),
        grid=(bsz, nh // 2, nb),
        in_specs=[pl.BlockSpec((None, 2, None, hd, blk), lambda b, h, i: (b, h, i, 0, 0)),
                  pl.BlockSpec((None, 2, t, hd), lambda b, h, i: (b, h, 0, 0)),
                  pl.BlockSpec((None, 2, nb, vt.shape[3], blk), lambda b, h, i: (b, h, 0, 0, 0)),
                  pl.BlockSpec((None, 2, nb, hd), lambda b, h, i: (b, h, 0, 0))],
        out_specs=pl.BlockSpec((None, blk, 2 * hd), lambda b, h, i: (b, i, h)),
        out_shape=jax.ShapeDtypeStruct((bsz, t, nh * hd), F32),
        scratch_shapes=[pltpu.VMEM((2, 2, blk, blk), F32)],
        compiler_params=_params(("arbitrary", "arbitrary", "arbitrary")),
    )(qt, k_hm, vt, kmean_hm)


def _rope_ret(x, c, s):
    w = x.shape[1]
    d = lax.broadcasted_iota(jnp.int32, x.shape, 1) % RET_DK
    half = RET_DK // 2
    partner = jnp.where(d < half, pltpu.roll(x, w - half, 1), pltpu.roll(x, half, 1))
    return x * c + partner * s


def _ret_decay(h):
    return 1.0 - 2.0 ** (-5.0 - h)


def _retention_kernel(q_ref, k_ref, v_ref, g_ref, c_ref, s_ref, dm_ref, qd_ref, kd_ref, o_ref, st_ref, state, *, tm):
    ch = RET_CHUNK

    @pl.when(pl.program_id(1) == 0)
    def _():
        state[...] = jnp.zeros_like(state)

    q = _rope_ret(q_ref[...], c_ref[...], s_ref[...])
    k = _rope_ret(k_ref[...], c_ref[...], s_ref[...]) * (RET_DK ** -0.5)
    kt = k.T
    for c in range(tm // ch):
        rows = slice(c * ch, (c + 1) * ch)
        for h in range(RET_HEADS):
            qc = q[rows, h * RET_DK:(h + 1) * RET_DK]
            ktc = kt[h * RET_DK:(h + 1) * RET_DK, rows]
            vc = v_ref[rows, h * RET_DV:(h + 1) * RET_DV].astype(BF16)
            s_old = state[h]
            inner = jnp.dot(qc.astype(BF16), ktc.astype(BF16), preferred_element_type=F32) * dm_ref[h]
            o = jnp.dot(inner.astype(BF16), vc, preferred_element_type=F32)
            o = o + jnp.dot((qc * qd_ref[h]).astype(BF16), s_old.astype(BF16), preferred_element_type=F32)
            c_dec = _ret_decay(h) ** ch
            state[h] = c_dec * s_old + jnp.dot((ktc * kd_ref[h]).astype(BF16), vc, preferred_element_type=F32)
            gate = g_ref[rows, h * RET_DV:(h + 1) * RET_DV]
            o_ref[rows, h * RET_DV:(h + 1) * RET_DV] = _ln(o) * _silu(gate)
    st_ref[...] = state[...]


def _retention(z, tabs, consts):
    bsz, t, _ = z.shape
    tm = min(t, 512)
    dmat, qdec, kdec = consts
    const3 = lambda a: pl.BlockSpec(a.shape, lambda b, i: (0, 0, 0))
    tab = pl.BlockSpec((tm, RET_QK_DIM), lambda b, i: (i, 0))
    return pl.pallas_call(
        functools.partial(_retention_kernel, tm=tm),
        grid=(bsz, t // tm),
        in_specs=[pl.BlockSpec((None, tm, RET_QK_DIM), lambda b, i: (b, i, OFF_RQ // RET_QK_DIM)),
                  pl.BlockSpec((None, tm, RET_QK_DIM), lambda b, i: (b, i, OFF_RK // RET_QK_DIM)),
                  pl.BlockSpec((None, tm, RET_V_DIM), lambda b, i: (b, i, OFF_RV // RET_V_DIM)),
                  pl.BlockSpec((None, tm, RET_V_DIM), lambda b, i: (b, i, OFF_RG // RET_V_DIM)),
                  tab, tab, const3(dmat), const3(qdec), const3(kdec)],
        out_specs=[pl.BlockSpec((None, tm, RET_V_DIM), lambda b, i: (b, i, 0)),
                   pl.BlockSpec((None, RET_HEADS, RET_DK, RET_DV), lambda b, i: (b, 0, 0, 0))],
        out_shape=[jax.ShapeDtypeStruct((bsz, t, RET_V_DIM), F32),
                   jax.ShapeDtypeStruct((bsz, RET_HEADS, RET_DK, RET_DV), F32)],
        scratch_shapes=[pltpu.VMEM((RET_HEADS, RET_DK, RET_DV), F32)],
        compiler_params=_params(("arbitrary", "arbitrary")),
    )(z, z, z, z, *tabs, dmat, qdec, kdec)


def _ret_consts():
    ch = RET_CHUNK
    log_g = jnp.log(1.0 - 2.0 ** (-5.0 - jnp.arange(RET_HEADS, dtype=F32)))
    i = jnp.arange(ch, dtype=F32)
    diff = i[:, None] - i[None, :]
    dmat = jnp.where(diff >= 0, jnp.exp(jnp.maximum(diff, 0.0)[None] * log_g[:, None, None]), 0.0)
    qdec = jnp.exp((i + 1.0)[None, :] * log_g[:, None])[:, :, None]
    kdec = jnp.exp((ch - 1.0 - i)[None, :] * log_g[:, None])[:, None, :]
    return dmat, qdec, kdec


def _row_to_col(row):
    n = row.shape[1]
    eye = lax.broadcasted_iota(jnp.int32, (n, n), 0) == lax.broadcasted_iota(jnp.int32, (n, n), 1)
    return jnp.sum(jnp.where(eye, jnp.broadcast_to(row, (n, n)), 0.0), axis=1, keepdims=True)


def _ret_step_kernel(q_ref, k_ref, v_ref, g_ref, c_ref, s_ref, st_ref, o_ref, sn_ref):
    q = _rope_ret(q_ref[...], c_ref[...], s_ref[...])
    k = _rope_ret(k_ref[...], c_ref[...], s_ref[...]) * (RET_DK ** -0.5)
    for h in range(RET_HEADS):
        g = _ret_decay(h)
        qrow = q[:, h * RET_DK:(h + 1) * RET_DK]
        krow = k[:, h * RET_DK:(h + 1) * RET_DK]
        qcol = _row_to_col(qrow)
        kcol = _row_to_col(krow)
        v = v_ref[:, h * RET_DV:(h + 1) * RET_DV]
        s_old = st_ref[h]
        inner = jnp.sum(qrow * krow, axis=1, keepdims=True)
        o = inner * v + jnp.sum((qcol * g) * s_old, axis=0, keepdims=True)
        sn_ref[h] = g * s_old + kcol * v
        o_ref[:, h * RET_DV:(h + 1) * RET_DV] = _ln(o) * _silu(g_ref[:, h * RET_DV:(h + 1) * RET_DV])


def _ret_step(z, tabs, state):
    n = z.shape[0]
    tab = pl.BlockSpec((None, 1, RET_QK_DIM), lambda b: (b, 0, 0))
    st = pl.BlockSpec((None, RET_HEADS, RET_DK, RET_DV), lambda b: (b, 0, 0, 0))
    return pl.pallas_call(
        _ret_step_kernel,
        grid=(n,),
        in_specs=[pl.BlockSpec((None, 1, RET_QK_DIM), lambda b: (b, 0, OFF_RQ // RET_QK_DIM)),
                  pl.BlockSpec((None, 1, RET_QK_DIM), lambda b: (b, 0, OFF_RK // RET_QK_DIM)),
                  pl.BlockSpec((None, 1, RET_V_DIM), lambda b: (b, 0, OFF_RV // RET_V_DIM)),
                  pl.BlockSpec((None, 1, RET_V_DIM), lambda b: (b, 0, OFF_RG // RET_V_DIM)),
                  tab, tab, st],
        out_specs=[pl.BlockSpec((None, 1, RET_V_DIM), lambda b: (b, 0, 0)), st],
        out_shape=[jax.ShapeDtypeStruct((n, 1, RET_V_DIM), F32),
                   jax.ShapeDtypeStruct(state.shape, F32)],
        compiler_params=_params(("arbitrary",)),
    )(z, z, z, z, *tabs, state)


def _merge_kernel(*refs, tm, seq_mode, alpha):
    if seq_mode:
        (x_ref, g1_ref, a_ref, r_ref, cb_ref, cc_ref, cx_ref, gt0_ref, gt1_ref, gt2_ref, cw_ref, wbr_ref, wo_ref,
         lg_ref, lb_ref, o_ref, cs_ref, carry) = refs
    else:
        (x_ref, g1_ref, a_ref, r_ref, cb_ref, cc_ref, cx_ref, gt0_ref, gt1_ref, gt2_ref, cw_ref, wbr_ref, wo_ref,
         lg_ref, lb_ref, p0_ref, p1_ref, o_ref, cs_ref) = refs
    s = cc_ref[...] * cx_ref[...]
    if seq_mode:
        @pl.when(pl.program_id(1) == 0)
        def _():
            carry[...] = jnp.zeros_like(carry)

        rid = lax.broadcasted_iota(jnp.int32, s.shape, 0)
        prev0, prev1 = carry[0:1, :], carry[1:2, :]
        s1 = jnp.where(rid == 0, prev1, pltpu.roll(s, 1, 0))
        s2 = jnp.where(rid == 0, prev0, jnp.where(rid == 1, prev1, pltpu.roll(s, 2, 0)))
        carry[...] = s[tm - 2:tm, :]
        cs_ref[...] = s[tm - 2:tm, :]
    else:
        s2, s1 = p0_ref[...], p1_ref[...]
        cs_ref[...] = s
    y = cw_ref[0:1, :] * s2 + cw_ref[1:2, :] * s1 + cw_ref[2:3, :] * s
    c_out = cb_ref[...] * y
    merged = _sigmoid(gt0_ref[...]) * jnp.dot(a_ref[...].astype(BF16), wbr_ref[0], preferred_element_type=F32)
    merged = merged + _sigmoid(gt1_ref[...]) * jnp.dot(r_ref[...].astype(BF16), wbr_ref[1],
                                                       preferred_element_type=F32)
    merged = merged + _sigmoid(gt2_ref[...]) * jnp.dot(c_out.astype(BF16), wbr_ref[2], preferred_element_type=F32)
    m = jnp.dot(merged.astype(BF16), wo_ref[...], preferred_element_type=F32)
    o_ref[...] = _ln(alpha * x_ref[...] + g1_ref[...] * m) * lg_ref[...] + lb_ref[...]


def _merge(x, g1, a_out, r_out, z, conv_w, w_br, w_o, ln_g, ln_b, alpha, conv_prev=None):
    bsz, t, d = x.shape
    seq_mode = conv_prev is None
    tm = min(t, 512)
    tok512 = lambda cblk: pl.BlockSpec((None, tm, CONV_DIM), lambda b, i: (b, i, cblk))
    gate = lambda n: pl.BlockSpec((None, tm, d), lambda b, i: (b, i, OFF_GATE // d + n))
    full2 = lambda a: pl.BlockSpec(a.shape, lambda b, i: (0, 0))
    in_specs = [pl.BlockSpec((None, tm, d), lambda b, i: (b, i, 0)), _mod_spec(g1, tm, 2),
                tok512(0), tok512(0), tok512(OFF_CB // CONV_DIM), tok512(OFF_CC // CONV_DIM),
                tok512(OFF_CX // CONV_DIM), gate(0), gate(1), gate(2),
                full2(conv_w), pl.BlockSpec(w_br.shape, lambda b, i: (0, 0, 0)), full2(w_o),
                full2(ln_g), full2(ln_b)]
    args = [x, g1, a_out, r_out, z, z, z, z, z, z, conv_w, w_br, w_o, ln_g, ln_b]
    scratch = []
    if seq_mode:
        cs_spec = pl.BlockSpec((None, CONV_W - 1, CONV_DIM), lambda b, i: (b, 0, 0))
        cs_shape = jax.ShapeDtypeStruct((bsz, CONV_W - 1, CONV_DIM), F32)
        scratch = [pltpu.VMEM((CONV_W - 1, CONV_DIM), F32)]
    else:
        in_specs += [tok512(0), tok512(0)]
        args += list(conv_prev)
        cs_spec = tok512(0)
        cs_shape = jax.ShapeDtypeStruct((bsz, t, CONV_DIM), F32)
    return pl.pallas_call(
        functools.partial(_merge_kernel, tm=tm, seq_mode=seq_mode, alpha=alpha),
        grid=(bsz, t // tm),
        in_specs=in_specs,
        out_specs=[pl.BlockSpec((None, tm, d), lambda b, i: (b, i, 0)), cs_spec],
        out_shape=[jax.ShapeDtypeStruct((bsz, t, d), F32), cs_shape],
        scratch_shapes=scratch,
        compiler_params=_params(("arbitrary", "arbitrary")),
    )(*args)


def _ffn_kernel(x_ref, sc_ref, sh_ref, g2_ref, wa_ref, wb_ref, wo_ref, lg_ref, lb_ref, o_ref, u_ref, acc_ref, *,
                alpha):
    f = pl.program_id(2)

    @pl.when(f == 0)
    def _():
        u_ref[...] = (x_ref[...] * (1.0 + sc_ref[...]) + sh_ref[...]).astype(BF16)
        acc_ref[...] = jnp.zeros_like(acc_ref)

    u = u_ref[...]
    a = jnp.dot(u, wa_ref[...], preferred_element_type=F32)
    b = jnp.dot(u, wb_ref[...], preferred_element_type=F32)
    acc_ref[...] += jnp.dot((_silu(a) * b).astype(BF16), wo_ref[...], preferred_element_type=F32)

    @pl.when(f == pl.num_programs(2) - 1)
    def _():
        o_ref[...] = _ln(alpha * x_ref[...] + g2_ref[...] * acc_ref[...]) * lg_ref[...] + lb_ref[...]


def _ffn(x, sc, sh, g2, w_in, w_out, ln_g, ln_b, alpha):
    bsz, t, d = x.shape
    tm = min(t, 512)
    tf = D_FF // 2
    nf = D_FF // tf
    full2 = lambda a: pl.BlockSpec(a.shape, lambda b, i, f: (0, 0))
    return pl.pallas_call(
        functools.partial(_ffn_kernel, alpha=alpha),
        grid=(bsz, t // tm, nf),
        in_specs=[pl.BlockSpec((None, tm, d), lambda b, i, f: (b, i, 0)),
                  _mod_spec(sc, tm, 3), _mod_spec(sh, tm, 3), _mod_spec(g2, tm, 3),
                  pl.BlockSpec((d, tf), lambda b, i, f: (0, f)),
                  pl.BlockSpec((d, tf), lambda b, i, f: (0, nf + f)),
                  pl.BlockSpec((tf, d), lambda b, i, f: (f, 0)),
                  full2(ln_g), full2(ln_b)],
        out_specs=pl.BlockSpec((None, tm, d), lambda b, i, f: (b, i, 0)),
        out_shape=jax.ShapeDtypeStruct((bsz, t, d), F32),
        scratch_shapes=[pltpu.VMEM((tm, d), BF16), pltpu.VMEM((tm, d), F32)],
        compiler_params=_params(("arbitrary", "arbitrary", "arbitrary")),
    )(x, sc, sh, g2, w_in, w_in, w_out, ln_g, ln_b)


def _col_to_row(col):
    n = col.shape[0]
    eye = lax.broadcasted_iota(jnp.int32, (n, n), 0) == lax.broadcasted_iota(jnp.int32, (n, n), 1)
    return jnp.sum(jnp.where(eye, jnp.broadcast_to(col, (n, n)), 0.0), axis=0, keepdims=True)


def _dec_scan_kernel(pt_ref, q_ref, ck_ref, o_ref, buf, sem, *, layer, n_pages, cp):
    b = pl.program_id(0)
    n_chunks = n_pages // cp
    cb = cp // 2
    nbp = n_pages // 2

    def copies(c, slot):
        return [pltpu.make_async_copy(ck_ref.at[layer, pt_ref[b * n_pages + c * cp + p]], buf.at[slot, p],
                                      sem.at[slot]) for p in range(cp)]

    for cpy in copies(0, 0):
        cpy.start()

    qcols = [_row_to_col(q_ref[h:h + 1, :]) for h in range(ATT_HEADS)]
    head = lax.broadcasted_iota(jnp.int32, (ATT_HEADS, LANES), 0)
    lane = lax.broadcasted_iota(jnp.int32, (ATT_HEADS, LANES), 1)

    def body(c, gates):
        slot = c % 2

        @pl.when(c + 1 < n_chunks)
        def _():
            for cpy in copies(c + 1, 1 - slot):
                cpy.start()

        for cpy in copies(c, slot):
            cpy.wait()
        for jb in range(cb):
            per_token = jnp.zeros((ATT_HEADS, LANES), F32)
            for h in range(ATT_HEADS):
                kt = buf[slot, 2 * jb, h] + buf[slot, 2 * jb + 1, h]
                per_token = jnp.where(head == h, jnp.sum(kt * qcols[h], axis=0, keepdims=True), per_token)
            g = jnp.sum(per_token, axis=1, keepdims=True) * (1.0 / MOBA_BLOCK)
            gates = jnp.where(lane == c * cb + jb, g, gates)
        return gates

    gates = lax.fori_loop(0, n_chunks, body, jnp.full((ATT_HEADS, LANES), NEG_INF, F32))
    res = jnp.zeros((ATT_HEADS, LANES), jnp.int32)
    for r in range(min(MOBA_TOPK, nbp)):
        mx = jnp.max(gates, axis=1, keepdims=True)
        ix = jnp.min(jnp.where(gates == mx, lane, LANES), axis=1, keepdims=True)
        gates = jnp.where(lane == ix, NEG_INF, gates)
        res = jnp.where(lane == r, ix, res)
    o_ref[...] = res


def _dec_scan(page_table, q_hd, cache_kt, layer):
    n, n_pages = page_table.shape
    page_size = cache_kt.shape[4]
    assert page_size * 2 == MOBA_BLOCK and page_size == LANES and n_pages // 2 <= LANES
    cp = min(8, n_pages)
    grid_spec = pltpu.PrefetchScalarGridSpec(
        num_scalar_prefetch=1,
        grid=(n,),
        in_specs=[pl.BlockSpec((None, ATT_HEADS, ATT_HD), lambda b, pt: (b, 0, 0)),
                  pl.BlockSpec(memory_space=pl.ANY)],
        out_specs=pl.BlockSpec((None, ATT_HEADS, LANES), lambda b, pt: (b, 0, 0)),
        scratch_shapes=[pltpu.VMEM((2, cp, ATT_HEADS, ATT_HD, page_size), F32),
                        pltpu.SemaphoreType.DMA((2,))],
    )
    return pl.pallas_call(
        functools.partial(_dec_scan_kernel, layer=layer, n_pages=n_pages, cp=cp),
        grid_spec=grid_spec,
        out_shape=jax.ShapeDtypeStruct((n, ATT_HEADS, LANES), jnp.int32),
        compiler_params=_params(("arbitrary",)),
    )(page_table.reshape(-1), q_hd, cache_kt)


def _dec_attend_kernel(pt_ref, pick_ref, q_ref, kn_ref, vn_ref, ck_ref, cv_ref, o_ref, kb, vb, sem, *,
                       layer, n_pages, n_sel):
    b = pl.program_id(0)
    n_tiles = 2 * n_sel

    def copies(h, slot):
        out = []
        for r in range(n_sel):
            blk = pick_ref[(b * ATT_HEADS + h) * n_sel + r]
            for p in range(2):
                page = pt_ref[b * n_pages + 2 * blk + p]
                out.append(pltpu.make_async_copy(ck_ref.at[layer, page, h], kb.at[slot, 2 * r + p], sem.at[2 * slot]))
                out.append(pltpu.make_async_copy(cv_ref.at[layer, page, h], vb.at[slot, 2 * r + p],
                                                 sem.at[2 * slot + 1]))
        return out

    for cpy in copies(0, 0):
        cpy.start()
    for h in range(ATT_HEADS):
        slot = h % 2
        if h + 1 < ATT_HEADS:
            for cpy in copies(h + 1, 1 - slot):
                cpy.start()
        qrow = q_ref[h:h + 1, :]
        qcol = _row_to_col(qrow)
        s_new = jnp.sum(kn_ref[h:h + 1, :] * qrow, axis=1, keepdims=True)
        vcol_new = _row_to_col(vn_ref[h:h + 1, :])
        for cpy in copies(h, slot):
            cpy.wait()
        s = [jnp.sum(kb[slot, t] * qcol, axis=0, keepdims=True) for t in range(n_tiles)]
        m = s_new
        for st in s:
            m = jnp.maximum(m, jnp.max(st, axis=1, keepdims=True))
        p_new = jnp.exp(s_new - m)
        den = p_new
        num = p_new * vcol_new
        for t, st in enumerate(s):
            p = jnp.exp(st - m)
            den = den + jnp.sum(p, axis=1, keepdims=True)
            num = num + jnp.sum(vb[slot, t] * p, axis=1, keepdims=True)
        o_ref[h:h + 1, :] = _col_to_row(num / den)


def _dec_attend(page_table, picks, q_hd, k_new, v_new, cache_kt, cache_vt, layer):
    n, n_pages = page_table.shape
    page_size = cache_kt.shape[4]
    n_sel = min(MOBA_TOPK, n_pages // 2)
    tok = pl.BlockSpec((None, ATT_HEADS, ATT_HD), lambda b, pt, pk: (b, 0, 0))
    grid_spec = pltpu.PrefetchScalarGridSpec(
        num_scalar_prefetch=2,
        grid=(n,),
        in_specs=[tok, tok, tok, pl.BlockSpec(memory_space=pl.ANY), pl.BlockSpec(memory_space=pl.ANY)],
        out_specs=tok,
        scratch_shapes=[pltpu.VMEM((2, 2 * n_sel, ATT_HD, page_size), F32),
                        pltpu.VMEM((2, 2 * n_sel, ATT_HD, page_size), F32),
                        pltpu.SemaphoreType.DMA((4,))],
    )
    return pl.pallas_call(
        functools.partial(_dec_attend_kernel, layer=layer, n_pages=n_pages, n_sel=n_sel),
        grid_spec=grid_spec,
        out_shape=jax.ShapeDtypeStruct((n, ATT_HEADS, ATT_HD), F32),
        compiler_params=_params(("arbitrary",)),
    )(page_table.reshape(-1), picks, q_hd, k_new, v_new, cache_kt, cache_vt)


def _att_rope_tables(pos):
    half = ROPE_DIM // 2
    inv = ROPE_THETA ** (-2.0 * jnp.arange(half, dtype=F32) / ROPE_DIM)
    ang = pos.astype(F32)[:, None] * inv[None, :]
    cos, sin = jnp.cos(ang), jnp.sin(ang)
    t = pos.shape[0]
    rest = ATT_HD - ROPE_DIM
    ca = jnp.concatenate([cos, cos, jnp.ones((t, rest), F32)], axis=1)
    cm = jnp.concatenate([-sin, jnp.zeros((t, half + rest), F32)], axis=1)
    cp = jnp.concatenate([jnp.zeros((t, half), F32), sin, jnp.zeros((t, rest), F32)], axis=1)
    rep = LANES // ATT_HD
    return tuple(jnp.tile(a, (1, rep)) for a in (ca, cm, cp))


def _ret_rope_tables(pos):
    half = RET_DK // 2
    inv = RET_THETA ** (-2.0 * jnp.arange(half, dtype=F32) / RET_DK)
    ang = pos.astype(F32)[:, None] * inv[None, :]
    cos, sin = jnp.cos(ang), jnp.sin(ang)
    c = jnp.concatenate([cos, cos], axis=1)
    s = jnp.concatenate([-sin, sin], axis=1)
    return jnp.tile(c, (1, RET_HEADS)), jnp.tile(s, (1, RET_HEADS))


def kernel(x_prompt, x_sample, cache_k, cache_v, state_ret, state_conv, page_table, c_prompt, c_sample, w_ada, b_ada,
           w_in, conv_w, w_br, w_o, ln1_g, ln1_b, w_ffn_in, w_ffn_out, ln2_g, ln2_b):
    depth = w_in.shape[0]
    alpha = (2 * depth) ** 0.25
    bsz, seq, d = x_prompt.shape
    nd = x_sample.shape[0]
    assert x_sample.shape[1] == 1, "one new token per decode sequence"
    n_pool, page_size = cache_k.shape[1], cache_k.shape[2]
    past_len = page_table.shape[1] * page_size

    ada = _ada(jnp.concatenate([c_prompt, c_sample], axis=0), w_ada, b_ada)
    w_in_b = jnp.concatenate([w_in[:, :, N_MIX:], w_in[:, :, :N_MIX]], axis=2).astype(BF16)
    w_br_b, w_o_b = w_br.astype(BF16), w_o.astype(BF16)
    w_fi_b, w_fo_b = w_ffn_in.astype(BF16), w_ffn_out.astype(BF16)
    cache_kt = jnp.transpose(cache_k, (0, 1, 3, 4, 2))
    cache_vt = jnp.transpose(cache_v, (0, 1, 3, 4, 2))

    pos_p = jnp.arange(seq, dtype=jnp.int32)
    pos_s = jnp.full((nd,), past_len, jnp.int32)
    att_tab_p, ret_tab_p = _att_rope_tables(pos_p), _ret_rope_tables(pos_p)
    att_tab_s = _att_rope_tables(pos_s)
    ret_tab_s = tuple(a.reshape(nd, 1, RET_QK_DIM) for a in _ret_rope_tables(pos_s))
    ret_consts = _ret_consts()

    xp = x_prompt
    xs = x_sample.reshape(1, nd, d)
    outs = {k: [] for k in ("kp", "vp", "ks", "vs", "rp", "rs", "cp", "cs")}
    for l in range(depth):
        mods_p = [m.reshape(bsz, 1, d) for m in jnp.split(ada[l, :bsz], 6, axis=-1)]
        mods_s = [m.reshape(1, nd, d) for m in jnp.split(ada[l, bsz:], 6, axis=-1)]
        lg1, lb1 = ln1_g[l].reshape(1, d), ln1_b[l].reshape(1, d)
        lg2, lb2 = ln2_g[l].reshape(1, d), ln2_b[l].reshape(1, d)

        sh1, sc1, g1, sh2, sc2, g2 = mods_p
        z = _inproj(xp, sc1, sh1, w_in_b[l])
        q_t, k_hm, v_t, k_tok, v_tok = _attn_prep(z, att_tab_p, head_major=True)
        kmean = _kmean(k_tok)
        kmean_hm = kmean.reshape(bsz, -1, ATT_HEADS, ATT_HD).transpose(0, 2, 1, 3)
        a_out = _moba(q_t, k_hm, v_t, kmean_hm)
        r_out, r_state = _retention(z, ret_tab_p, ret_consts)
        xp, c_state = _merge(xp, g1, a_out, r_out, z, conv_w[l], w_br_b[l], w_o_b[l], lg1, lb1, alpha)
        xp = _ffn(xp, sc2, sh2, g2, w_fi_b[l], w_fo_b[l], lg2, lb2, alpha)
        outs["kp"].append(k_tok.reshape(bsz, seq, ATT_HEADS, ATT_HD))
        outs["vp"].append(v_tok.reshape(bsz, seq, ATT_HEADS, ATT_HD))
        outs["rp"].append(r_state)
        outs["cp"].append(c_state)

        sh1, sc1, g1, sh2, sc2, g2 = mods_s
        z = _inproj(xs, sc1, sh1, w_in_b[l])
        q_tok, k_new, v_new = _attn_prep(z, att_tab_s, head_major=False)
        q_tok, k_new, v_new = (a.reshape(nd, ATT_HEADS, ATT_HD) for a in (q_tok, k_new, v_new))
        picks = _dec_scan(page_table, q_tok, cache_kt, l)
        n_sel = min(MOBA_TOPK, page_table.shape[1] // 2)
        a_out = _dec_attend(page_table, picks[:, :, :n_sel].reshape(-1), q_tok, k_new, v_new, cache_kt, cache_vt, l)
        r_out, r_state = _ret_step(z.reshape(nd, 1, IN_DIM), ret_tab_s, state_ret[l])
        prev = (state_conv[l][:, 0, :].reshape(1, nd, CONV_DIM), state_conv[l][:, 1, :].reshape(1, nd, CONV_DIM))
        xs, s_new = _merge(xs, g1, a_out.reshape(1, nd, ATT_DIM), r_out.reshape(1, nd, RET_V_DIM), z, conv_w[l],
                           w_br_b[l], w_o_b[l], lg1, lb1, alpha, conv_prev=prev)
        xs = _ffn(xs, sc2, sh2, g2, w_fi_b[l], w_fo_b[l], lg2, lb2, alpha)
        outs["ks"].append(k_new.reshape(nd, 1, ATT_HEADS, ATT_HD))
        outs["vs"].append(v_new.reshape(nd, 1, ATT_HEADS, ATT_HD))
        outs["rs"].append(r_state)
        outs["cs"].append(jnp.stack([state_conv[l][:, 1, :], s_new.reshape(nd, CONV_DIM)], axis=1))

    st = {k: jnp.stack(v) for k, v in outs.items()}
    return (xp, xs.reshape(nd, 1, d), st["kp"], st["vp"], st["ks"], st["vs"], st["rp"], st["rs"], st["cp"], st["cs"])
```

```python
import functools

import jax
import jax.numpy as jnp
from jax import lax
from jax.experimental import pallas as pl
from jax.experimental.pallas import tpu as pltpu

F32 = jnp.float32
BF16 = jnp.bfloat16

D_MODEL = 1024
ATT_HEADS = 8
ATT_HD = 64
ATT_DIM = ATT_HEADS * ATT_HD
ROPE_DIM = ATT_HD // 4
ROPE_THETA = 500000.0
MOBA_BLOCK = 256
MOBA_TOPK = 3
RET_HEADS = 4
RET_DK = 64
RET_DV = 128
RET_QK_DIM = RET_HEADS * RET_DK
RET_V_DIM = RET_HEADS * RET_DV
RET_THETA = 10000.0
RET_CHUNK = 128
CONV_DIM = 512
CONV_W = 3
N_BRANCH = 3
D_FF = 2816
LN_EPS = 1e-5
N_GATE = N_BRANCH * D_MODEL
N_MIX = 3 * ATT_DIM + 2 * RET_QK_DIM + 2 * RET_V_DIM + 3 * CONV_DIM
IN_DIM = N_MIX + N_GATE

OFF_GATE = 0
OFF_AQ = N_GATE
OFF_AK = OFF_AQ + ATT_DIM
OFF_AV = OFF_AK + ATT_DIM
OFF_RQ = OFF_AV + ATT_DIM
OFF_RK = OFF_RQ + RET_QK_DIM
OFF_RV = OFF_RK + RET_QK_DIM
OFF_RG = OFF_RV + RET_V_DIM
OFF_CB = OFF_RG + RET_V_DIM
OFF_CC = OFF_CB + CONV_DIM
OFF_CX = OFF_CC + CONV_DIM

LANES = 128
BF16_SUBLANES = 16
VT_ROWS = ATT_HD + BF16_SUBLANES
LOG2E = 1.4426950408889634
HPS = 4
VMEM_LIMIT = 48 * 1024 * 1024
NEG_INF = float("-inf")


def _params(semantics, vmem=VMEM_LIMIT):
    return pltpu.CompilerParams(dimension_semantics=semantics, vmem_limit_bytes=vmem)


def _ln(y):
    mu = jnp.mean(y, axis=-1, keepdims=True)
    yc = y - mu
    return yc * lax.rsqrt(jnp.mean(yc * yc, axis=-1, keepdims=True) + LN_EPS)


def _sigmoid(x):
    return 1.0 / (1.0 + jnp.exp(-x))


def _silu(x):
    return x * _sigmoid(x)


def _mod_spec(mod, tm, ngrid):
    per_row = mod.shape[1] != 1
    rows = tm if per_row else 1
    if ngrid == 2:
        return pl.BlockSpec((None, rows, mod.shape[2]), lambda b, i: (b, i if per_row else 0, 0))
    return pl.BlockSpec((None, rows, mod.shape[2]), lambda b, i, j: (b, i if per_row else 0, 0))


def _ada_kernel(c_ref, w_ref, b_ref, o_ref):
    cond = _silu(c_ref[...]).astype(BF16)
    o_ref[...] = jnp.dot(cond, w_ref[...].astype(BF16), preferred_element_type=F32) + b_ref[...]


def _ada(c, w_ada, b_ada):
    depth, d, n = w_ada.shape
    rows = c.shape[0]
    tn = 1536
    return pl.pallas_call(
        _ada_kernel,
        grid=(depth, n // tn),
        in_specs=[pl.BlockSpec((rows, d), lambda l, j: (0, 0)),
                  pl.BlockSpec((None, d, tn), lambda l, j: (l, 0, j)),
                  pl.BlockSpec((None, 1, tn), lambda l, j: (l, 0, j))],
        out_specs=pl.BlockSpec((None, rows, tn), lambda l, j: (l, 0, j)),
        out_shape=jax.ShapeDtypeStruct((depth, rows, n), F32),
        compiler_params=_params(("arbitrary", "arbitrary")),
    )(c, w_ada, b_ada.reshape(depth, 1, n))


def _inproj_kernel(x_ref, sc_ref, sh_ref, w_ref, o_ref, u_ref):
    @pl.when(pl.program_id(2) == 0)
    def _():
        u_ref[...] = (x_ref[...] * (1.0 + sc_ref[...]) + sh_ref[...]).astype(BF16)

    o_ref[...] = jnp.dot(u_ref[...], w_ref[...], preferred_element_type=F32)


def _inproj(x, sc, sh, w):
    bsz, t, d = x.shape
    n = w.shape[1]
    tm = min(t, 1024)
    tn = 1536
    return pl.pallas_call(
        _inproj_kernel,
        grid=(bsz, t // tm, n // tn),
        in_specs=[pl.BlockSpec((None, tm, d), lambda b, i, j: (b, i, 0)),
                  _mod_spec(sc, tm, 3), _mod_spec(sh, tm, 3),
                  pl.BlockSpec((d, tn), lambda b, i, j: (0, j))],
        out_specs=pl.BlockSpec((None, tm, tn), lambda b, i, j: (b, i, j)),
        out_shape=jax.ShapeDtypeStruct((bsz, t, n), F32),
        scratch_shapes=[pltpu.VMEM((tm, d), BF16)],
        compiler_params=_params(("arbitrary", "arbitrary", "arbitrary")),
    )(x, sc, sh, w)


def _rope_att(x, ca, cm, cp):
    return x * ca + pltpu.roll(x, LANES - ROPE_DIM // 2, 1) * cm + pltpu.roll(x, ROPE_DIM // 2, 1) * cp


def _attn_prep_kernel(q_ref, k_ref, v_ref, ca_ref, cm_ref, cp_ref, *out_refs, head_major, tm):
    ca, cm, cp = ca_ref[...], cm_ref[...], cp_ref[...]
    scale = ATT_HD ** -0.5
    if head_major:
        qt_ref, kh_ref, vt_ref, ko_ref, vo_ref = out_refs
    else:
        qo_ref, ko_ref, vo_ref = out_refs
    vo_ref[...] = v_ref[...]
    for c in range(ATT_DIM // LANES):
        sl = slice(c * LANES, (c + 1) * LANES)
        q = _rope_att(q_ref[:, sl], ca, cm, cp) * scale
        k = _rope_att(k_ref[:, sl], ca, cm, cp)
        ko_ref[:, sl] = k
        if head_major:
            qt = q.T
            vt = v_ref[:, sl].T.astype(BF16)
            for hh in range(2):
                hs = slice(hh * ATT_HD, (hh + 1) * ATT_HD)
                kh_ref[2 * c + hh] = k[:, hs].astype(BF16)
                for j in range(tm // MOBA_BLOCK):
                    cols = slice(j * MOBA_BLOCK, (j + 1) * MOBA_BLOCK)
                    qt_ref[2 * c + hh, j] = qt[hs, cols]
                    vt_ref[2 * c + hh, j, 0:ATT_HD, :] = vt[hs, cols]
                    vt_ref[2 * c + hh, j, ATT_HD:VT_ROWS, :] = jnp.ones((VT_ROWS - ATT_HD, MOBA_BLOCK), BF16)
        else:
            qo_ref[:, sl] = q


def _attn_prep(z, tabs, head_major):
    bsz, t, _ = z.shape
    tm = min(t, 512)
    cq, ck, cv = OFF_AQ // ATT_DIM, OFF_AK // ATT_DIM, OFF_AV // ATT_DIM
    tok = pl.BlockSpec((None, tm, ATT_DIM), lambda b, i: (b, i, 0))
    tab = pl.BlockSpec((tm, LANES), lambda b, i: (i, 0))
    tok_shape = jax.ShapeDtypeStruct((bsz, t, ATT_DIM), F32)
    if head_major:
        nblk = tm // MOBA_BLOCK
        hm = pl.BlockSpec((None, ATT_HEADS, tm, ATT_HD), lambda b, i: (b, 0, i, 0))
        tr = pl.BlockSpec((None, ATT_HEADS, nblk, ATT_HD, MOBA_BLOCK), lambda b, i: (b, 0, i, 0, 0))
        trv = pl.BlockSpec((None, ATT_HEADS, nblk, VT_ROWS, MOBA_BLOCK), lambda b, i: (b, 0, i, 0, 0))
        nb = t // MOBA_BLOCK
        out_specs = [tr, hm, trv, tok, tok]
        out_shape = [jax.ShapeDtypeStruct((bsz, ATT_HEADS, nb, ATT_HD, MOBA_BLOCK), F32),
                     jax.ShapeDtypeStruct((bsz, ATT_HEADS, t, ATT_HD), BF16),
                     jax.ShapeDtypeStruct((bsz, ATT_HEADS, nb, VT_ROWS, MOBA_BLOCK), BF16), tok_shape, tok_shape]
    else:
        out_specs = [tok, tok, tok]
        out_shape = [tok_shape, tok_shape, tok_shape]
    return pl.pallas_call(
        functools.partial(_attn_prep_kernel, head_major=head_major, tm=tm),
        grid=(bsz, t // tm),
        in_specs=[pl.BlockSpec((None, tm, ATT_DIM), lambda b, i: (b, i, cq)),
                  pl.BlockSpec((None, tm, ATT_DIM), lambda b, i: (b, i, ck)),
                  pl.BlockSpec((None, tm, ATT_DIM), lambda b, i: (b, i, cv)),
                  tab, tab, tab],
        out_specs=out_specs,
        out_shape=out_shape,
        compiler_params=_params(("arbitrary", "arbitrary")),
    )(z, z, z, *tabs)


def _kmean_kernel(k_ref, o_ref, *, nblk):
    for j in range(nblk):
        o_ref[j:j + 1, :] = jnp.mean(k_ref[j * MOBA_BLOCK:(j + 1) * MOBA_BLOCK, :], axis=0, keepdims=True)


def _kmean(k_tok):
    bsz, t, _ = k_tok.shape
    rows = min(t, 8 * MOBA_BLOCK)
    nblk = rows // MOBA_BLOCK
    return pl.pallas_call(
        functools.partial(_kmean_kernel, nblk=nblk),
        grid=(bsz, t // rows),
        in_specs=[pl.BlockSpec((None, rows, ATT_DIM), lambda b, i: (b, i, 0))],
        out_specs=pl.BlockSpec((None, nblk, ATT_DIM), lambda b, i: (b, i, 0)),
        out_shape=jax.ShapeDtypeStruct((bsz, t // MOBA_BLOCK, ATT_DIM), F32),
        compiler_params=_params(("arbitrary", "arbitrary")),
    )(k_tok)


def _moba_kernel(qt_ref, k_ref, vt_ref, km_ref, o_ref, s_scr, p_scr, *, nb):
    i = pl.program_id(2)
    blk = MOBA_BLOCK
    key = lax.broadcasted_iota(jnp.int32, (blk, blk), 0)
    qry = lax.broadcasted_iota(jnp.int32, (blk, blk), 1)
    bid = lax.broadcasted_iota(jnp.int32, (nb, blk), 0)
    qtbs, pickss, init = [], [], []

    def scores(j):
        off = pl.multiple_of(j * blk, blk)
        return [jnp.dot(k_ref[hh, pl.ds(off, blk), :], qtbs[hh], preferred_element_type=F32) for hh in range(HPS)]

    for hh in range(HPS):
        qt = qt_ref[hh]
        gate = jnp.dot(km_ref[hh], qt, precision=lax.Precision.HIGHEST, preferred_element_type=F32)
        gate = jnp.where(bid < i, gate, NEG_INF)
        picks = []
        for r in range(MOBA_TOPK):
            mx = jnp.max(gate, axis=0, keepdims=True)
            ix = jnp.min(jnp.where(gate == mx, bid, nb), axis=0, keepdims=True)
            gate = jnp.where(bid == ix, NEG_INF, gate)
            picks.append(jnp.where(r < i, ix, -1))
        pickss.append(picks)
        qtbs.append((qt * LOG2E).astype(BF16))

    for hh, s in enumerate(scores(i)):
        s = jnp.where(key <= qry, s, NEG_INF)
        m0 = jnp.max(s, axis=0, keepdims=True)
        p = jnp.exp2(s - m0)
        acc0 = jnp.dot(vt_ref[hh, i], p.astype(BF16), preferred_element_type=F32)
        init.append((m0, acc0, jnp.ones_like(m0)))

    for hh, s in enumerate(scores(0)):
        s_scr[0, hh] = s
        p_scr[hh] = jnp.zeros((blk, blk), BF16)

    def finish(jp, hh, acc, alpha):
        return alpha * acc + jnp.dot(vt_ref[hh, jp], p_scr[hh], preferred_element_type=F32)

    def step(j, carry, cur):
        jp = jnp.maximum(j - 1, 0)
        lagged = [finish(jp, hh, carry[hh][1], carry[hh][2]) for hh in range(HPS)]
        nxt = scores(jnp.minimum(j + 1, nb - 1))
        new = []
        for hh in range(HPS):
            m = carry[hh][0]
            picks = pickss[hh]
            live = (picks[0] == j) | (picks[1] == j) | (picks[2] == j)
            sj = s_scr[cur, hh]
            m_new = jnp.maximum(m, jnp.where(live, jnp.max(sj, axis=0, keepdims=True), NEG_INF))
            alpha = jnp.exp2(m - m_new)
            pj = jnp.exp2(sj - jnp.where(live, m_new, jnp.inf))
            p_scr[hh] = pj.astype(BF16)
            s_scr[1 - cur, hh] = nxt[hh]
            new.append((m_new, lagged[hh], alpha))
        return tuple(new)

    def pair(t, carry):
        return step(2 * t + 1, step(2 * t, carry, 0), 1)

    trips = (i + 1) // 2
    fin = lax.fori_loop(0, trips, pair, tuple(init))
    j_last = jnp.maximum(2 * trips - 1, 0)
    outs = []
    for hh, (_, acc, alpha) in enumerate(fin):
        acc = finish(j_last, hh, acc, alpha)
        outs.append(acc[0:ATT_HD] / acc[ATT_HD:ATT_HD + 1])
    o_ref[...] = jnp.concatenate(outs, axis=0).T


def _moba(qt, k_hm, vt, kmean_hm):
    bsz, nh, nb, hd, blk = qt.shape
    t = nb * blk
    return pl.pallas_call(
        functools.partial(_moba_kernel, nb=nb),
        grid=(bsz, nh // HPS, nb),
        in_specs=[pl.BlockSpec((None, HPS, None, hd, blk), lambda b, h, i: (b, h, i, 0, 0)),
                  pl.BlockSpec((None, HPS, t, hd), lambda b, h, i: (b, h, 0, 0)),
                  pl.BlockSpec((None, HPS, nb, vt.shape[3], blk), lambda b, h, i: (b, h, 0, 0, 0)),
                  pl.BlockSpec((None, HPS, nb, hd), lambda b, h, i: (b, h, 0, 0))],
        out_specs=pl.BlockSpec((None, blk, HPS * hd), lambda b, h, i: (b, i, h)),
        out_shape=jax.ShapeDtypeStruct((bsz, t, nh * hd), F32),
        scratch_shapes=[pltpu.VMEM((2, HPS, blk, blk), F32), pltpu.VMEM((HPS, blk, blk), BF16)],
        compiler_params=_params(("arbitrary", "arbitrary", "arbitrary")),
    )(qt, k_hm, vt, kmean_hm)


def _rope_ret(x, c, s):
    w = x.shape[1]
    d = lax.broadcasted_iota(jnp.int32, x.shape, 1) % RET_DK
    half = RET_DK // 2
    partner = jnp.where(d < half, pltpu.roll(x, w - half, 1), pltpu.roll(x, half, 1))
    return x * c + partner * s


def _ret_decay(h):
    return 1.0 - 2.0 ** (-5.0 - h)


def _retention_kernel(q_ref, k_ref, v_ref, g_ref, c_ref, s_ref, dm_ref, qd_ref, kd_ref, o_ref, st_ref, state, *, tm):
    ch = RET_CHUNK

    @pl.when(pl.program_id(1) == 0)
    def _():
        state[...] = jnp.zeros_like(state)

    q = _rope_ret(q_ref[...], c_ref[...], s_ref[...])
    k = _rope_ret(k_ref[...], c_ref[...], s_ref[...]) * (RET_DK ** -0.5)
    kt = k.T
    for c in range(tm // ch):
        rows = slice(c * ch, (c + 1) * ch)
        for h in range(RET_HEADS):
            qc = q[rows, h * RET_DK:(h + 1) * RET_DK]
            ktc = kt[h * RET_DK:(h + 1) * RET_DK, rows]
            vc = v_ref[rows, h * RET_DV:(h + 1) * RET_DV].astype(BF16)
            s_old = state[h]
            inner = jnp.dot(qc.astype(BF16), ktc.astype(BF16), preferred_element_type=F32) * dm_ref[h]
            o = jnp.dot(inner.astype(BF16), vc, preferred_element_type=F32)
            o = o + jnp.dot((qc * qd_ref[h]).astype(BF16), s_old.astype(BF16), preferred_element_type=F32)
            c_dec = _ret_decay(h) ** ch
            state[h] = c_dec * s_old + jnp.dot((ktc * kd_ref[h]).astype(BF16), vc, preferred_element_type=F32)
            gate = g_ref[rows, h * RET_DV:(h + 1) * RET_DV]
            o_ref[rows, h * RET_DV:(h + 1) * RET_DV] = _ln(o) * _silu(gate)
    st_ref[...] = state[...]


def _retention(z, tabs, consts):
    bsz, t, _ = z.shape
    tm = min(t, 512)
    dmat, qdec, kdec = consts
    const3 = lambda a: pl.BlockSpec(a.shape, lambda b, i: (0, 0, 0))
    tab = pl.BlockSpec((tm, RET_QK_DIM), lambda b, i: (i, 0))
    return pl.pallas_call(
        functools.partial(_retention_kernel, tm=tm),
        grid=(bsz, t // tm),
        in_specs=[pl.BlockSpec((None, tm, RET_QK_DIM), lambda b, i: (b, i, OFF_RQ // RET_QK_DIM)),
                  pl.BlockSpec((None, tm, RET_QK_DIM), lambda b, i: (b, i, OFF_RK // RET_QK_DIM)),
                  pl.BlockSpec((None, tm, RET_V_DIM), lambda b, i: (b, i, OFF_RV // RET_V_DIM)),
                  pl.BlockSpec((None, tm, RET_V_DIM), lambda b, i: (b, i, OFF_RG // RET_V_DIM)),
                  tab, tab, const3(dmat), const3(qdec), const3(kdec)],
        out_specs=[pl.BlockSpec((None, tm, RET_V_DIM), lambda b, i: (b, i, 0)),
                   pl.BlockSpec((None, RET_HEADS, RET_DK, RET_DV), lambda b, i: (b, 0, 0, 0))],
        out_shape=[jax.ShapeDtypeStruct((bsz, t, RET_V_DIM), F32),
                   jax.ShapeDtypeStruct((bsz, RET_HEADS, RET_DK, RET_DV), F32)],
        scratch_shapes=[pltpu.VMEM((RET_HEADS, RET_DK, RET_DV), F32)],
        compiler_params=_params(("arbitrary", "arbitrary")),
    )(z, z, z, z, *tabs, dmat, qdec, kdec)


def _ret_consts():
    ch = RET_CHUNK
    log_g = jnp.log(1.0 - 2.0 ** (-5.0 - jnp.arange(RET_HEADS, dtype=F32)))
    i = jnp.arange(ch, dtype=F32)
    diff = i[:, None] - i[None, :]
    dmat = jnp.where(diff >= 0, jnp.exp(jnp.maximum(diff, 0.0)[None] * log_g[:, None, None]), 0.0)
    qdec = jnp.exp((i + 1.0)[None, :] * log_g[:, None])[:, :, None]
    kdec = jnp.exp((ch - 1.0 - i)[None, :] * log_g[:, None])[:, None, :]
    return dmat, qdec, kdec


def _row_to_col(row):
    n = row.shape[1]
    eye = lax.broadcasted_iota(jnp.int32, (n, n), 0) == lax.broadcasted_iota(jnp.int32, (n, n), 1)
    return jnp.sum(jnp.where(eye, jnp.broadcast_to(row, (n, n)), 0.0), axis=1, keepdims=True)


def _ret_step_kernel(q_ref, k_ref, v_ref, g_ref, c_ref, s_ref, st_ref, o_ref, sn_ref):
    q = _rope_ret(q_ref[...], c_ref[...], s_ref[...])
    k = _rope_ret(k_ref[...], c_ref[...], s_ref[...]) * (RET_DK ** -0.5)
    for h in range(RET_HEADS):
        g = _ret_decay(h)
        qrow = q[:, h * RET_DK:(h + 1) * RET_DK]
        krow = k[:, h * RET_DK:(h + 1) * RET_DK]
        qcol = _row_to_col(qrow)
        kcol = _row_to_col(krow)
        v = v_ref[:, h * RET_DV:(h + 1) * RET_DV]
        s_old = st_ref[h]
        inner = jnp.sum(qrow * krow, axis=1, keepdims=True)
        o = inner * v + jnp.sum((qcol * g) * s_old, axis=0, keepdims=True)
        sn_ref[h] = g * s_old + kcol * v
        o_ref[:, h * RET_DV:(h + 1) * RET_DV] = _ln(o) * _silu(g_ref[:, h * RET_DV:(h + 1) * RET_DV])


def _ret_step(z, tabs, state):
    n = z.shape[0]
    tab = pl.BlockSpec((None, 1, RET_QK_DIM), lambda b: (b, 0, 0))
    st = pl.BlockSpec((None, RET_HEADS, RET_DK, RET_DV), lambda b: (b, 0, 0, 0))
    return pl.pallas_call(
        _ret_step_kernel,
        grid=(n,),
        in_specs=[pl.BlockSpec((None, 1, RET_QK_DIM), lambda b: (b, 0, OFF_RQ // RET_QK_DIM)),
                  pl.BlockSpec((None, 1, RET_QK_DIM), lambda b: (b, 0, OFF_RK // RET_QK_DIM)),
                  pl.BlockSpec((None, 1, RET_V_DIM), lambda b: (b, 0, OFF_RV // RET_V_DIM)),
                  pl.BlockSpec((None, 1, RET_V_DIM), lambda b: (b, 0, OFF_RG // RET_V_DIM)),
                  tab, tab, st],
        out_specs=[pl.BlockSpec((None, 1, RET_V_DIM), lambda b: (b, 0, 0)), st],
        out_shape=[jax.ShapeDtypeStruct((n, 1, RET_V_DIM), F32),
                   jax.ShapeDtypeStruct(state.shape, F32)],
        compiler_params=_params(("arbitrary",)),
    )(z, z, z, z, *tabs, state)


def _merge_kernel(*refs, tm, seq_mode, alpha):
    if seq_mode:
        (x_ref, g1_ref, a_ref, r_ref, cb_ref, cc_ref, cx_ref, gt0_ref, gt1_ref, gt2_ref, cw_ref, wbr_ref, wo_ref,
         lg_ref, lb_ref, o_ref, cs_ref, carry) = refs
    else:
        (x_ref, g1_ref, a_ref, r_ref, cb_ref, cc_ref, cx_ref, gt0_ref, gt1_ref, gt2_ref, cw_ref, wbr_ref, wo_ref,
         lg_ref, lb_ref, p0_ref, p1_ref, o_ref, cs_ref) = refs
    s = cc_ref[...] * cx_ref[...]
    if seq_mode:
        @pl.when(pl.program_id(1) == 0)
        def _():
            carry[...] = jnp.zeros_like(carry)

        rid = lax.broadcasted_iota(jnp.int32, s.shape, 0)
        prev0, prev1 = carry[0:1, :], carry[1:2, :]
        s1 = jnp.where(rid == 0, prev1, pltpu.roll(s, 1, 0))
        s2 = jnp.where(rid == 0, prev0, jnp.where(rid == 1, prev1, pltpu.roll(s, 2, 0)))
        carry[...] = s[tm - 2:tm, :]
        cs_ref[...] = s[tm - 2:tm, :]
    else:
        s2, s1 = p0_ref[...], p1_ref[...]
        cs_ref[...] = s
    y = cw_ref[0:1, :] * s2 + cw_ref[1:2, :] * s1 + cw_ref[2:3, :] * s
    c_out = cb_ref[...] * y
    merged = _sigmoid(gt0_ref[...]) * jnp.dot(a_ref[...].astype(BF16), wbr_ref[0], preferred_element_type=F32)
    merged = merged + _sigmoid(gt1_ref[...]) * jnp.dot(r_ref[...].astype(BF16), wbr_ref[1],
                                                       preferred_element_type=F32)
    merged = merged + _sigmoid(gt2_ref[...]) * jnp.dot(c_out.astype(BF16), wbr_ref[2], preferred_element_type=F32)
    m = jnp.dot(merged.astype(BF16), wo_ref[...], preferred_element_type=F32)
    o_ref[...] = _ln(alpha * x_ref[...] + g1_ref[...] * m) * lg_ref[...] + lb_ref[...]


def _merge(x, g1, a_out, r_out, z, conv_w, w_br, w_o, ln_g, ln_b, alpha, conv_prev=None):
    bsz, t, d = x.shape
    seq_mode = conv_prev is None
    tm = min(t, 512)
    tok512 = lambda cblk: pl.BlockSpec((None, tm, CONV_DIM), lambda b, i: (b, i, cblk))
    gate = lambda n: pl.BlockSpec((None, tm, d), lambda b, i: (b, i, OFF_GATE // d + n))
    full2 = lambda a: pl.BlockSpec(a.shape, lambda b, i: (0, 0))
    in_specs = [pl.BlockSpec((None, tm, d), lambda b, i: (b, i, 0)), _mod_spec(g1, tm, 2),
                tok512(0), tok512(0), tok512(OFF_CB // CONV_DIM), tok512(OFF_CC // CONV_DIM),
                tok512(OFF_CX // CONV_DIM), gate(0), gate(1), gate(2),
                full2(conv_w), pl.BlockSpec(w_br.shape, lambda b, i: (0, 0, 0)), full2(w_o),
                full2(ln_g), full2(ln_b)]
    args = [x, g1, a_out, r_out, z, z, z, z, z, z, conv_w, w_br, w_o, ln_g, ln_b]
    scratch = []
    if seq_mode:
        cs_spec = pl.BlockSpec((None, CONV_W - 1, CONV_DIM), lambda b, i: (b, 0, 0))
        cs_shape = jax.ShapeDtypeStruct((bsz, CONV_W - 1, CONV_DIM), F32)
        scratch = [pltpu.VMEM((CONV_W - 1, CONV_DIM), F32)]
    else:
        in_specs += [tok512(0), tok512(0)]
        args += list(conv_prev)
        cs_spec = tok512(0)
        cs_shape = jax.ShapeDtypeStruct((bsz, t, CONV_DIM), F32)
    return pl.pallas_call(
        functools.partial(_merge_kernel, tm=tm, seq_mode=seq_mode, alpha=alpha),
        grid=(bsz, t // tm),
        in_specs=in_specs,
        out_specs=[pl.BlockSpec((None, tm, d), lambda b, i: (b, i, 0)), cs_spec],
        out_shape=[jax.ShapeDtypeStruct((bsz, t, d), F32), cs_shape],
        scratch_shapes=scratch,
        compiler_params=_params(("arbitrary", "arbitrary")),
    )(*args)


def _ffn_kernel(x_ref, sc_ref, sh_ref, g2_ref, wa_ref, wb_ref, wo_ref, lg_ref, lb_ref, o_ref, u_ref, acc_ref, *,
                alpha):
    f = pl.program_id(2)

    @pl.when(f == 0)
    def _():
        u_ref[...] = (x_ref[...] * (1.0 + sc_ref[...]) + sh_ref[...]).astype(BF16)
        acc_ref[...] = jnp.zeros_like(acc_ref)

    u = u_ref[...]
    a = jnp.dot(u, wa_ref[...], preferred_element_type=F32)
    b = jnp.dot(u, wb_ref[...], preferred_element_type=F32)
    acc_ref[...] += jnp.dot((_silu(a) * b).astype(BF16), wo_ref[...], preferred_element_type=F32)

    @pl.when(f == pl.num_programs(2) - 1)
    def _():
        o_ref[...] = _ln(alpha * x_ref[...] + g2_ref[...] * acc_ref[...]) * lg_ref[...] + lb_ref[...]


def _ffn(x, sc, sh, g2, w_in, w_out, ln_g, ln_b, alpha):
    bsz, t, d = x.shape
    tm = min(t, 512)
    tf = D_FF // 2
    nf = D_FF // tf
    full2 = lambda a: pl.BlockSpec(a.shape, lambda b, i, f: (0, 0))
    return pl.pallas_call(
        functools.partial(_ffn_kernel, alpha=alpha),
        grid=(bsz, t // tm, nf),
        in_specs=[pl.BlockSpec((None, tm, d), lambda b, i, f: (b, i, 0)),
                  _mod_spec(sc, tm, 3), _mod_spec(sh, tm, 3), _mod_spec(g2, tm, 3),
                  pl.BlockSpec((d, tf), lambda b, i, f: (0, f)),
                  pl.BlockSpec((d, tf), lambda b, i, f: (0, nf + f)),
                  pl.BlockSpec((tf, d), lambda b, i, f: (f, 0)),
                  full2(ln_g), full2(ln_b)],
        out_specs=pl.BlockSpec((None, tm, d), lambda b, i, f: (b, i, 0)),
        out_shape=jax.ShapeDtypeStruct((bsz, t, d), F32),
        scratch_shapes=[pltpu.VMEM((tm, d), BF16), pltpu.VMEM((tm, d), F32)],
        compiler_params=_params(("arbitrary", "arbitrary", "arbitrary")),
    )(x, sc, sh, g2, w_in, w_in, w_out, ln_g, ln_b)


def _col_to_row(col):
    n = col.shape[0]
    eye = lax.broadcasted_iota(jnp.int32, (n, n), 0) == lax.broadcasted_iota(jnp.int32, (n, n), 1)
    return jnp.sum(jnp.where(eye, jnp.broadcast_to(col, (n, n)), 0.0), axis=0, keepdims=True)


def _dec_scan_kernel(pt_ref, q_ref, ck_ref, o_ref, buf, sem, *, layer, n_pages, cp):
    b = pl.program_id(0)
    n_chunks = n_pages // cp
    cb = cp // 2
    nbp = n_pages // 2

    def copies(c, slot):
        return [pltpu.make_async_copy(ck_ref.at[layer, pt_ref[b * n_pages + c * cp + p]], buf.at[slot, p],
                                      sem.at[slot]) for p in range(cp)]

    for cpy in copies(0, 0):
        cpy.start()

    qcols = [_row_to_col(q_ref[h:h + 1, :]) for h in range(ATT_HEADS)]
    head = lax.broadcasted_iota(jnp.int32, (ATT_HEADS, LANES), 0)
    lane = lax.broadcasted_iota(jnp.int32, (ATT_HEADS, LANES), 1)

    def body(c, gates):
        slot = c % 2

        @pl.when(c + 1 < n_chunks)
        def _():
            for cpy in copies(c + 1, 1 - slot):
                cpy.start()

        for cpy in copies(c, slot):
            cpy.wait()
        for jb in range(cb):
            per_token = jnp.zeros((ATT_HEADS, LANES), F32)
            for h in range(ATT_HEADS):
                kt = buf[slot, 2 * jb, h] + buf[slot, 2 * jb + 1, h]
                per_token = jnp.where(head == h, jnp.sum(kt * qcols[h], axis=0, keepdims=True), per_token)
            g = jnp.sum(per_token, axis=1, keepdims=True) * (1.0 / MOBA_BLOCK)
            gates = jnp.where(lane == c * cb + jb, g, gates)
        return gates

    gates = lax.fori_loop(0, n_chunks, body, jnp.full((ATT_HEADS, LANES), NEG_INF, F32))
    res = jnp.zeros((ATT_HEADS, LANES), jnp.int32)
    for r in range(min(MOBA_TOPK, nbp)):
        mx = jnp.max(gates, axis=1, keepdims=True)
        ix = jnp.min(jnp.where(gates == mx, lane, LANES), axis=1, keepdims=True)
        gates = jnp.where(lane == ix, NEG_INF, gates)
        res = jnp.where(lane == r, ix, res)
    o_ref[...] = res


def _dec_scan(page_table, q_hd, cache_kt, layer):
    n, n_pages = page_table.shape
    page_size = cache_kt.shape[4]
    assert page_size * 2 == MOBA_BLOCK and page_size == LANES and n_pages // 2 <= LANES
    cp = min(16, n_pages)
    grid_spec = pltpu.PrefetchScalarGridSpec(
        num_scalar_prefetch=1,
        grid=(n,),
        in_specs=[pl.BlockSpec((None, ATT_HEADS, ATT_HD), lambda b, pt: (b, 0, 0)),
                  pl.BlockSpec(memory_space=pl.ANY)],
        out_specs=pl.BlockSpec((None, ATT_HEADS, LANES), lambda b, pt: (b, 0, 0)),
        scratch_shapes=[pltpu.VMEM((2, cp, ATT_HEADS, ATT_HD, page_size), F32),
                        pltpu.SemaphoreType.DMA((2,))],
    )
    return pl.pallas_call(
        functools.partial(_dec_scan_kernel, layer=layer, n_pages=n_pages, cp=cp),
        grid_spec=grid_spec,
        out_shape=jax.ShapeDtypeStruct((n, ATT_HEADS, LANES), jnp.int32),
        compiler_params=_params(("arbitrary",)),
    )(page_table.reshape(-1), q_hd, cache_kt)


def _dec_attend_kernel(pt_ref, pick_ref, q_ref, kn_ref, vn_ref, ck_ref, cv_ref, o_ref, kb, vb, sem, *,
                       layer, n_pages, n_sel):
    b = pl.program_id(0)
    n_tiles = 2 * n_sel

    def copies(h, slot):
        out = []
        for r in range(n_sel):
            blk = pick_ref[(b * ATT_HEADS + h) * n_sel + r]
            for p in range(2):
                page = pt_ref[b * n_pages + 2 * blk + p]
                out.append(pltpu.make_async_copy(ck_ref.at[layer, page, h], kb.at[slot, 2 * r + p], sem.at[2 * slot]))
                out.append(pltpu.make_async_copy(cv_ref.at[layer, page, h], vb.at[slot, 2 * r + p],
                                                 sem.at[2 * slot + 1]))
        return out

    for cpy in copies(0, 0):
        cpy.start()
    for h in range(ATT_HEADS):
        slot = h % 2
        if h + 1 < ATT_HEADS:
            for cpy in copies(h + 1, 1 - slot):
                cpy.start()
        qrow = q_ref[h:h + 1, :]
        qcol = _row_to_col(qrow)
        s_new = jnp.sum(kn_ref[h:h + 1, :] * qrow, axis=1, keepdims=True)
        vcol_new = _row_to_col(vn_ref[h:h + 1, :])
        for cpy in copies(h, slot):
            cpy.wait()
        s = [jnp.sum(kb[slot, t] * qcol, axis=0, keepdims=True) for t in range(n_tiles)]
        m = s_new
        for st in s:
            m = jnp.maximum(m, jnp.max(st, axis=1, keepdims=True))
        p_new = jnp.exp(s_new - m)
        den = p_new
        num = p_new * vcol_new
        for t, st in enumerate(s):
            p = jnp.exp(st - m)
            den = den + jnp.sum(p, axis=1, keepdims=True)
            num = num + jnp.sum(vb[slot, t] * p, axis=1, keepdims=True)
        o_ref[h:h + 1, :] = _col_to_row(num / den)


def _dec_attend(page_table, picks, q_hd, k_new, v_new, cache_kt, cache_vt, layer):
    n, n_pages = page_table.shape
    page_size = cache_kt.shape[4]
    n_sel = min(MOBA_TOPK, n_pages // 2)
    tok = pl.BlockSpec((None, ATT_HEADS, ATT_HD), lambda b, pt, pk: (b, 0, 0))
    grid_spec = pltpu.PrefetchScalarGridSpec(
        num_scalar_prefetch=2,
        grid=(n,),
        in_specs=[tok, tok, tok, pl.BlockSpec(memory_space=pl.ANY), pl.BlockSpec(memory_space=pl.ANY)],
        out_specs=tok,
        scratch_shapes=[pltpu.VMEM((2, 2 * n_sel, ATT_HD, page_size), F32),
                        pltpu.VMEM((2, 2 * n_sel, ATT_HD, page_size), F32),
                        pltpu.SemaphoreType.DMA((4,))],
    )
    return pl.pallas_call(
        functools.partial(_dec_attend_kernel, layer=layer, n_pages=n_pages, n_sel=n_sel),
        grid_spec=grid_spec,
        out_shape=jax.ShapeDtypeStruct((n, ATT_HEADS, ATT_HD), F32),
        compiler_params=_params(("arbitrary",)),
    )(page_table.reshape(-1), picks, q_hd, k_new, v_new, cache_kt, cache_vt)


def _att_rope_tables(pos):
    half = ROPE_DIM // 2
    inv = ROPE_THETA ** (-2.0 * jnp.arange(half, dtype=F32) / ROPE_DIM)
    ang = pos.astype(F32)[:, None] * inv[None, :]
    cos, sin = jnp.cos(ang), jnp.sin(ang)
    t = pos.shape[0]
    rest = ATT_HD - ROPE_DIM
    ca = jnp.concatenate([cos, cos, jnp.ones((t, rest), F32)], axis=1)
    cm = jnp.concatenate([-sin, jnp.zeros((t, half + rest), F32)], axis=1)
    cp = jnp.concatenate([jnp.zeros((t, half), F32), sin, jnp.zeros((t, rest), F32)], axis=1)
    rep = LANES // ATT_HD
    return tuple(jnp.tile(a, (1, rep)) for a in (ca, cm, cp))


def _ret_rope_tables(pos):
    half = RET_DK // 2
    inv = RET_THETA ** (-2.0 * jnp.arange(half, dtype=F32) / RET_DK)
    ang = pos.astype(F32)[:, None] * inv[None, :]
    cos, sin = jnp.cos(ang), jnp.sin(ang)
    c = jnp.concatenate([cos, cos], axis=1)
    s = jnp.concatenate([-sin, sin], axis=1)
    return jnp.tile(c, (1, RET_HEADS)), jnp.tile(s, (1, RET_HEADS))


def kernel(x_prompt, x_sample, cache_k, cache_v, state_ret, state_conv, page_table, c_prompt, c_sample, w_ada, b_ada,
           w_in, conv_w, w_br, w_o, ln1_g, ln1_b, w_ffn_in, w_ffn_out, ln2_g, ln2_b):
    depth = w_in.shape[0]
    alpha = (2 * depth) ** 0.25
    bsz, seq, d = x_prompt.shape
    nd = x_sample.shape[0]
    assert x_sample.shape[1] == 1, "one new token per decode sequence"
    n_pool, page_size = cache_k.shape[1], cache_k.shape[2]
    past_len = page_table.shape[1] * page_size

    ada = _ada(jnp.concatenate([c_prompt, c_sample], axis=0), w_ada, b_ada)
    w_in_b = jnp.concatenate([w_in[:, :, N_MIX:], w_in[:, :, :N_MIX]], axis=2).astype(BF16)
    w_br_b, w_o_b = w_br.astype(BF16), w_o.astype(BF16)
    w_fi_b, w_fo_b = w_ffn_in.astype(BF16), w_ffn_out.astype(BF16)
    cache_kt = jnp.transpose(cache_k, (0, 1, 3, 4, 2))
    cache_vt = jnp.transpose(cache_v, (0, 1, 3, 4, 2))

    pos_p = jnp.arange(seq, dtype=jnp.int32)
    pos_s = jnp.full((nd,), past_len, jnp.int32)
    att_tab_p, ret_tab_p = _att_rope_tables(pos_p), _ret_rope_tables(pos_p)
    att_tab_s = _att_rope_tables(pos_s)
    ret_tab_s = tuple(a.reshape(nd, 1, RET_QK_DIM) for a in _ret_rope_tables(pos_s))
    ret_consts = _ret_consts()

    xp = x_prompt
    xs = x_sample.reshape(1, nd, d)
    outs = {k: [] for k in ("kp", "vp", "ks", "vs", "rp", "rs", "cp", "cs")}
    for l in range(depth):
        mods_p = [m.reshape(bsz, 1, d) for m in jnp.split(ada[l, :bsz], 6, axis=-1)]
        mods_s = [m.reshape(1, nd, d) for m in jnp.split(ada[l, bsz:], 6, axis=-1)]
        lg1, lb1 = ln1_g[l].reshape(1, d), ln1_b[l].reshape(1, d)
        lg2, lb2 = ln2_g[l].reshape(1, d), ln2_b[l].reshape(1, d)

        sh1, sc1, g1, sh2, sc2, g2 = mods_p
        z = _inproj(xp, sc1, sh1, w_in_b[l])
        q_t, k_hm, v_t, k_tok, v_tok = _attn_prep(z, att_tab_p, head_major=True)
        kmean = _kmean(k_tok)
        kmean_hm = kmean.reshape(bsz, -1, ATT_HEADS, ATT_HD).transpose(0, 2, 1, 3)
        a_out = _moba(q_t, k_hm, v_t, kmean_hm)
        r_out, r_state = _retention(z, ret_tab_p, ret_consts)
        xp, c_state = _merge(xp, g1, a_out, r_out, z, conv_w[l], w_br_b[l], w_o_b[l], lg1, lb1, alpha)
        xp = _ffn(xp, sc2, sh2, g2, w_fi_b[l], w_fo_b[l], lg2, lb2, alpha)
        outs["kp"].append(k_tok.reshape(bsz, seq, ATT_HEADS, ATT_HD))
        outs["vp"].append(v_tok.reshape(bsz, seq, ATT_HEADS, ATT_HD))
        outs["rp"].append(r_state)
        outs["cp"].append(c_state)

        sh1, sc1, g1, sh2, sc2, g2 = mods_s
        z = _inproj(xs, sc1, sh1, w_in_b[l])
        q_tok, k_new, v_new = _attn_prep(z, att_tab_s, head_major=False)
        q_tok, k_new, v_new = (a.reshape(nd, ATT_HEADS, ATT_HD) for a in (q_tok, k_new, v_new))
        picks = _dec_scan(page_table, q_tok, cache_kt, l)
        n_sel = min(MOBA_TOPK, page_table.shape[1] // 2)
        a_out = _dec_attend(page_table, picks[:, :, :n_sel].reshape(-1), q_tok, k_new, v_new, cache_kt, cache_vt, l)
        r_out, r_state = _ret_step(z.reshape(nd, 1, IN_DIM), ret_tab_s, state_ret[l])
        prev = (state_conv[l][:, 0, :].reshape(1, nd, CONV_DIM), state_conv[l][:, 1, :].reshape(1, nd, CONV_DIM))
        xs, s_new = _merge(xs, g1, a_out.reshape(1, nd, ATT_DIM), r_out.reshape(1, nd, RET_V_DIM), z, conv_w[l],
                           w_br_b[l], w_o_b[l], lg1, lb1, alpha, conv_prev=prev)
        xs = _ffn(xs, sc2, sh2, g2, w_fi_b[l], w_fo_b[l], lg2, lb2, alpha)
        outs["ks"].append(k_new.reshape(nd, 1, ATT_HEADS, ATT_HD))
        outs["vs"].append(v_new.reshape(nd, 1, ATT_HEADS, ATT_HD))
        outs["rs"].append(r_state)
        outs["cs"].append(jnp.stack([state_conv[l][:, 1, :], s_new.reshape(nd, CONV_DIM)], axis=1))

    st = {k: jnp.stack(v) for k, v in outs.items()}
    return (xp, xs.reshape(nd, 1, d), st["kp"], st["vp"], st["ks"], st["vs"], st["rp"], st["rs"], st["cp"], st["cs"])
```

```python
import functools

import jax
import jax.numpy as jnp
from jax import lax
from jax.experimental import pallas as pl
from jax.experimental.pallas import tpu as pltpu

F32 = jnp.float32
BF16 = jnp.bfloat16

D_MODEL = 1024
ATT_HEADS = 8
ATT_HD = 64
ATT_DIM = ATT_HEADS * ATT_HD
ROPE_DIM = ATT_HD // 4
ROPE_THETA = 500000.0
MOBA_BLOCK = 256
MOBA_TOPK = 3
RET_HEADS = 4
RET_DK = 64
RET_DV = 128
RET_QK_DIM = RET_HEADS * RET_DK
RET_V_DIM = RET_HEADS * RET_DV
RET_THETA = 10000.0
RET_CHUNK = 128
CONV_DIM = 512
CONV_W = 3
N_BRANCH = 3
D_FF = 2816
LN_EPS = 1e-5
N_MIX = 3 * ATT_DIM + 2 * RET_QK_DIM + 2 * RET_V_DIM + 3 * CONV_DIM

OFF_AQ = 0
OFF_AK = OFF_AQ + ATT_DIM
OFF_AV = OFF_AK + ATT_DIM
OFF_RQ = OFF_AV + ATT_DIM
OFF_RK = OFF_RQ + RET_QK_DIM
OFF_RV = OFF_RK + RET_QK_DIM
OFF_RG = OFF_RV + RET_V_DIM
OFF_CB = OFF_RG + RET_V_DIM
OFF_CC = OFF_CB + CONV_DIM
OFF_CX = OFF_CC + CONV_DIM

LANES = 128
BF16_SUBLANES = 16
VT_ROWS = ATT_HD + BF16_SUBLANES
LOG2E = 1.4426950408889634
HPS = 4
VMEM_LIMIT = 48 * 1024 * 1024
NEG_INF = float("-inf")


def _params(semantics, vmem=VMEM_LIMIT):
    return pltpu.CompilerParams(dimension_semantics=semantics, vmem_limit_bytes=vmem)


def _ln(y):
    mu = jnp.mean(y, axis=-1, keepdims=True)
    yc = y - mu
    return yc * lax.rsqrt(jnp.mean(yc * yc, axis=-1, keepdims=True) + LN_EPS)


def _sigmoid(x):
    return 1.0 / (1.0 + jnp.exp(-x))


def _silu(x):
    return x * _sigmoid(x)


def _mod_spec(mod, tm, ngrid):
    per_row = mod.shape[1] != 1
    rows = tm if per_row else 1
    if ngrid == 2:
        return pl.BlockSpec((None, rows, mod.shape[2]), lambda b, i: (b, i if per_row else 0, 0))
    return pl.BlockSpec((None, rows, mod.shape[2]), lambda b, i, j: (b, i if per_row else 0, 0))


def _ada_kernel(c_ref, w_ref, b_ref, o_ref):
    cond = _silu(c_ref[...]).astype(BF16)
    o_ref[...] = jnp.dot(cond, w_ref[...].astype(BF16), preferred_element_type=F32) + b_ref[...]


def _ada(c, w_ada, b_ada):
    depth, d, n = w_ada.shape
    rows = c.shape[0]
    tn = 1536
    return pl.pallas_call(
        _ada_kernel,
        grid=(depth, n // tn),
        in_specs=[pl.BlockSpec((rows, d), lambda l, j: (0, 0)),
                  pl.BlockSpec((None, d, tn), lambda l, j: (l, 0, j)),
                  pl.BlockSpec((None, 1, tn), lambda l, j: (l, 0, j))],
        out_specs=pl.BlockSpec((None, rows, tn), lambda l, j: (l, 0, j)),
        out_shape=jax.ShapeDtypeStruct((depth, rows, n), F32),
        compiler_params=_params(("arbitrary", "arbitrary")),
    )(c, w_ada, b_ada.reshape(depth, 1, n))


def _inproj_kernel(x_ref, sc_ref, sh_ref, w_ref, o_ref, u_ref):
    @pl.when(pl.program_id(2) == 0)
    def _():
        u_ref[...] = (x_ref[...] * (1.0 + sc_ref[...]) + sh_ref[...]).astype(BF16)

    o_ref[...] = jnp.dot(u_ref[...], w_ref[...], preferred_element_type=F32)


def _inproj(x, sc, sh, w):
    bsz, t, d = x.shape
    n = w.shape[1]
    tm = min(t, 1024)
    tn = 1536
    return pl.pallas_call(
        _inproj_kernel,
        grid=(bsz, t // tm, n // tn),
        in_specs=[pl.BlockSpec((None, tm, d), lambda b, i, j: (b, i, 0)),
                  _mod_spec(sc, tm, 3), _mod_spec(sh, tm, 3),
                  pl.BlockSpec((d, tn), lambda b, i, j: (0, j))],
        out_specs=pl.BlockSpec((None, tm, tn), lambda b, i, j: (b, i, j)),
        out_shape=jax.ShapeDtypeStruct((bsz, t, n), F32),
        scratch_shapes=[pltpu.VMEM((tm, d), BF16)],
        compiler_params=_params(("arbitrary", "arbitrary", "arbitrary")),
    )(x, sc, sh, w)


def _rope_att(x, ca, cm, cp):
    return x * ca + pltpu.roll(x, LANES - ROPE_DIM // 2, 1) * cm + pltpu.roll(x, ROPE_DIM // 2, 1) * cp


def _attn_prep_kernel(q_ref, k_ref, v_ref, ca_ref, cm_ref, cp_ref, *out_refs, head_major, tm):
    ca, cm, cp = ca_ref[...], cm_ref[...], cp_ref[...]
    scale = ATT_HD ** -0.5
    if head_major:
        qt_ref, kh_ref, vt_ref, ko_ref, vo_ref = out_refs
    else:
        qo_ref, ko_ref, vo_ref = out_refs
    vo_ref[...] = v_ref[...]
    for c in range(ATT_DIM // LANES):
        sl = slice(c * LANES, (c + 1) * LANES)
        q = _rope_att(q_ref[:, sl], ca, cm, cp) * scale
        k = _rope_att(k_ref[:, sl], ca, cm, cp)
        ko_ref[:, sl] = k
        if head_major:
            qt = q.T
            vt = v_ref[:, sl].T.astype(BF16)
            for hh in range(2):
                hs = slice(hh * ATT_HD, (hh + 1) * ATT_HD)
                kh_ref[2 * c + hh] = k[:, hs].astype(BF16)
                for j in range(tm // MOBA_BLOCK):
                    cols = slice(j * MOBA_BLOCK, (j + 1) * MOBA_BLOCK)
                    qt_ref[2 * c + hh, j] = qt[hs, cols]
                    vt_ref[2 * c + hh, j, 0:ATT_HD, :] = vt[hs, cols]
                    vt_ref[2 * c + hh, j, ATT_HD:VT_ROWS, :] = jnp.ones((VT_ROWS - ATT_HD, MOBA_BLOCK), BF16)
        else:
            qo_ref[:, sl] = q


def _attn_prep(z, tabs, head_major):
    bsz, t, _ = z.shape
    tm = min(t, 512)
    cq, ck, cv = OFF_AQ // ATT_DIM, OFF_AK // ATT_DIM, OFF_AV // ATT_DIM
    tok = pl.BlockSpec((None, tm, ATT_DIM), lambda b, i: (b, i, 0))
    tab = pl.BlockSpec((tm, LANES), lambda b, i: (i, 0))
    tok_shape = jax.ShapeDtypeStruct((bsz, t, ATT_DIM), F32)
    if head_major:
        nblk = tm // MOBA_BLOCK
        hm = pl.BlockSpec((None, ATT_HEADS, tm, ATT_HD), lambda b, i: (b, 0, i, 0))
        tr = pl.BlockSpec((None, ATT_HEADS, nblk, ATT_HD, MOBA_BLOCK), lambda b, i: (b, 0, i, 0, 0))
        trv = pl.BlockSpec((None, ATT_HEADS, nblk, VT_ROWS, MOBA_BLOCK), lambda b, i: (b, 0, i, 0, 0))
        nb = t // MOBA_BLOCK
        out_specs = [tr, hm, trv, tok, tok]
        out_shape = [jax.ShapeDtypeStruct((bsz, ATT_HEADS, nb, ATT_HD, MOBA_BLOCK), F32),
                     jax.ShapeDtypeStruct((bsz, ATT_HEADS, t, ATT_HD), BF16),
                     jax.ShapeDtypeStruct((bsz, ATT_HEADS, nb, VT_ROWS, MOBA_BLOCK), BF16), tok_shape, tok_shape]
    else:
        out_specs = [tok, tok, tok]
        out_shape = [tok_shape, tok_shape, tok_shape]
    return pl.pallas_call(
        functools.partial(_attn_prep_kernel, head_major=head_major, tm=tm),
        grid=(bsz, t // tm),
        in_specs=[pl.BlockSpec((None, tm, ATT_DIM), lambda b, i: (b, i, cq)),
                  pl.BlockSpec((None, tm, ATT_DIM), lambda b, i: (b, i, ck)),
                  pl.BlockSpec((None, tm, ATT_DIM), lambda b, i: (b, i, cv)),
                  tab, tab, tab],
        out_specs=out_specs,
        out_shape=out_shape,
        compiler_params=_params(("arbitrary", "arbitrary")),
    )(z, z, z, *tabs)


def _kmean_kernel(k_ref, o_ref, *, nblk):
    for j in range(nblk):
        o_ref[j:j + 1, :] = jnp.mean(k_ref[j * MOBA_BLOCK:(j + 1) * MOBA_BLOCK, :], axis=0, keepdims=True)


def _kmean(k_tok):
    bsz, t, _ = k_tok.shape
    rows = min(t, 8 * MOBA_BLOCK)
    nblk = rows // MOBA_BLOCK
    return pl.pallas_call(
        functools.partial(_kmean_kernel, nblk=nblk),
        grid=(bsz, t // rows),
        in_specs=[pl.BlockSpec((None, rows, ATT_DIM), lambda b, i: (b, i, 0))],
        out_specs=pl.BlockSpec((None, nblk, ATT_DIM), lambda b, i: (b, i, 0)),
        out_shape=jax.ShapeDtypeStruct((bsz, t // MOBA_BLOCK, ATT_DIM), F32),
        compiler_params=_params(("arbitrary", "arbitrary")),
    )(k_tok)


def _moba_kernel(qt_ref, k_ref, vt_ref, km_ref, o_ref, s_scr, p_scr, *, nb):
    i = pl.program_id(2)
    blk = MOBA_BLOCK
    key = lax.broadcasted_iota(jnp.int32, (blk, blk), 0)
    qry = lax.broadcasted_iota(jnp.int32, (blk, blk), 1)
    bid = lax.broadcasted_iota(jnp.int32, (nb, blk), 0)
    qtbs, pickss, init = [], [], []

    def scores(j):
        off = pl.multiple_of(j * blk, blk)
        return [jnp.dot(k_ref[hh, pl.ds(off, blk), :], qtbs[hh], preferred_element_type=F32) for hh in range(HPS)]

    qts = [qt_ref[hh] for hh in range(HPS)]
    qtbs.extend((qt * LOG2E).astype(BF16) for qt in qts)

    for hh, s in enumerate(scores(0)):
        s_scr[0, hh] = s
        p_scr[hh] = jnp.zeros((blk, blk), BF16)
    own = scores(i)

    for hh in range(HPS):
        gate = jnp.dot(km_ref[hh], qts[hh], precision=lax.Precision.HIGHEST, preferred_element_type=F32)
        gate = jnp.where(bid < i, gate, NEG_INF)
        picks = []
        for r in range(MOBA_TOPK):
            mx = jnp.max(gate, axis=0, keepdims=True)
            ix = jnp.min(jnp.where(gate == mx, bid, nb), axis=0, keepdims=True)
            gate = jnp.where(bid == ix, NEG_INF, gate)
            picks.append(jnp.where(r < i, ix, -1))
        pickss.append(picks)

    for hh, s in enumerate(own):
        s = jnp.where(key <= qry, s, NEG_INF)
        m0 = jnp.max(s, axis=0, keepdims=True)
        p = jnp.exp2(s - m0)
        acc0 = jnp.dot(vt_ref[hh, i], p.astype(BF16), preferred_element_type=F32)
        init.append((m0, acc0, jnp.ones_like(m0)))

    def finish(jp, hh, acc, alpha):
        return alpha * acc + jnp.dot(vt_ref[hh, jp], p_scr[hh], preferred_element_type=F32)

    def step(j, carry, cur):
        jp = jnp.maximum(j - 1, 0)
        lagged = [finish(jp, hh, carry[hh][1], carry[hh][2]) for hh in range(HPS)]
        nxt = scores(jnp.minimum(j + 1, nb - 1))
        new = []
        for hh in range(HPS):
            m = carry[hh][0]
            picks = pickss[hh]
            live = (picks[0] == j) | (picks[1] == j) | (picks[2] == j)
            sj = s_scr[cur, hh]
            m_new = jnp.maximum(m, jnp.where(live, jnp.max(sj, axis=0, keepdims=True), NEG_INF))
            alpha = jnp.exp2(m - m_new)
            pj = jnp.exp2(sj - jnp.where(live, m_new, jnp.inf))
            p_scr[hh] = pj.astype(BF16)
            s_scr[1 - cur, hh] = nxt[hh]
            new.append((m_new, lagged[hh], alpha))
        return tuple(new)

    def pair(t, carry):
        return step(2 * t + 1, step(2 * t, carry, 0), 1)

    trips = (i + 1) // 2
    fin = lax.fori_loop(0, trips, pair, tuple(init))
    j_last = jnp.maximum(2 * trips - 1, 0)
    outs = []
    for hh, (_, acc, alpha) in enumerate(fin):
        acc = finish(j_last, hh, acc, alpha)
        outs.append(acc[0:ATT_HD] / acc[ATT_HD:ATT_HD + 1])
    o_ref[...] = jnp.concatenate(outs, axis=0).T


def _moba(qt, k_hm, vt, kmean_hm):
    bsz, nh, nb, hd, blk = qt.shape
    t = nb * blk
    return pl.pallas_call(
        functools.partial(_moba_kernel, nb=nb),
        grid=(bsz, nh // HPS, nb),
        in_specs=[pl.BlockSpec((None, HPS, None, hd, blk), lambda b, h, i: (b, h, i, 0, 0)),
                  pl.BlockSpec((None, HPS, t, hd), lambda b, h, i: (b, h, 0, 0)),
                  pl.BlockSpec((None, HPS, nb, vt.shape[3], blk), lambda b, h, i: (b, h, 0, 0, 0)),
                  pl.BlockSpec((None, HPS, nb, hd), lambda b, h, i: (b, h, 0, 0))],
        out_specs=pl.BlockSpec((None, blk, HPS * hd), lambda b, h, i: (b, i, h)),
        out_shape=jax.ShapeDtypeStruct((bsz, t, nh * hd), F32),
        scratch_shapes=[pltpu.VMEM((2, HPS, blk, blk), F32), pltpu.VMEM((HPS, blk, blk), BF16)],
        compiler_params=_params(("arbitrary", "arbitrary", "arbitrary")),
    )(qt, k_hm, vt, kmean_hm)


def _rope_ret(x, c, s):
    w = x.shape[1]
    d = lax.broadcasted_iota(jnp.int32, x.shape, 1) % RET_DK
    half = RET_DK // 2
    partner = jnp.where(d < half, pltpu.roll(x, w - half, 1), pltpu.roll(x, half, 1))
    return x * c + partner * s


def _ret_decay(h):
    return 1.0 - 2.0 ** (-5.0 - h)


def _retention_kernel(q_ref, k_ref, v_ref, g_ref, c_ref, s_ref, dm_ref, qd_ref, kd_ref, o_ref, st_ref, state, *, tm):
    ch = RET_CHUNK

    @pl.when(pl.program_id(1) == 0)
    def _():
        state[...] = jnp.zeros_like(state)

    q = _rope_ret(q_ref[...], c_ref[...], s_ref[...])
    k = _rope_ret(k_ref[...], c_ref[...], s_ref[...]) * (RET_DK ** -0.5)
    kt = k.T
    for c in range(tm // ch):
        rows = slice(c * ch, (c + 1) * ch)
        for h in range(RET_HEADS):
            qc = q[rows, h * RET_DK:(h + 1) * RET_DK]
            ktc = kt[h * RET_DK:(h + 1) * RET_DK, rows]
            vc = v_ref[rows, h * RET_DV:(h + 1) * RET_DV].astype(BF16)
            s_old = state[h]
            inner = jnp.dot(qc.astype(BF16), ktc.astype(BF16), preferred_element_type=F32) * dm_ref[h]
            o = jnp.dot(inner.astype(BF16), vc, preferred_element_type=F32)
            o = o + jnp.dot((qc * qd_ref[h]).astype(BF16), s_old.astype(BF16), preferred_element_type=F32)
            c_dec = _ret_decay(h) ** ch
            state[h] = c_dec * s_old + jnp.dot((ktc * kd_ref[h]).astype(BF16), vc, preferred_element_type=F32)
            gate = g_ref[rows, h * RET_DV:(h + 1) * RET_DV]
            o_ref[rows, h * RET_DV:(h + 1) * RET_DV] = _ln(o) * _silu(gate)
    st_ref[...] = state[...]


def _retention(z, tabs, consts):
    bsz, t, _ = z.shape
    tm = min(t, 512)
    dmat, qdec, kdec = consts
    const3 = lambda a: pl.BlockSpec(a.shape, lambda b, i: (0, 0, 0))
    tab = pl.BlockSpec((tm, RET_QK_DIM), lambda b, i: (i, 0))
    return pl.pallas_call(
        functools.partial(_retention_kernel, tm=tm),
        grid=(bsz, t // tm),
        in_specs=[pl.BlockSpec((None, tm, RET_QK_DIM), lambda b, i: (b, i, OFF_RQ // RET_QK_DIM)),
                  pl.BlockSpec((None, tm, RET_QK_DIM), lambda b, i: (b, i, OFF_RK // RET_QK_DIM)),
                  pl.BlockSpec((None, tm, RET_V_DIM), lambda b, i: (b, i, OFF_RV // RET_V_DIM)),
                  pl.BlockSpec((None, tm, RET_V_DIM), lambda b, i: (b, i, OFF_RG // RET_V_DIM)),
                  tab, tab, const3(dmat), const3(qdec), const3(kdec)],
        out_specs=[pl.BlockSpec((None, tm, RET_V_DIM), lambda b, i: (b, i, 0)),
                   pl.BlockSpec((None, RET_HEADS, RET_DK, RET_DV), lambda b, i: (b, 0, 0, 0))],
        out_shape=[jax.ShapeDtypeStruct((bsz, t, RET_V_DIM), F32),
                   jax.ShapeDtypeStruct((bsz, RET_HEADS, RET_DK, RET_DV), F32)],
        scratch_shapes=[pltpu.VMEM((RET_HEADS, RET_DK, RET_DV), F32)],
        compiler_params=_params(("arbitrary", "arbitrary")),
    )(z, z, z, z, *tabs, dmat, qdec, kdec)


def _ret_consts():
    ch = RET_CHUNK
    log_g = jnp.log(1.0 - 2.0 ** (-5.0 - jnp.arange(RET_HEADS, dtype=F32)))
    i = jnp.arange(ch, dtype=F32)
    diff = i[:, None] - i[None, :]
    dmat = jnp.where(diff >= 0, jnp.exp(jnp.maximum(diff, 0.0)[None] * log_g[:, None, None]), 0.0)
    qdec = jnp.exp((i + 1.0)[None, :] * log_g[:, None])[:, :, None]
    kdec = jnp.exp((ch - 1.0 - i)[None, :] * log_g[:, None])[:, None, :]
    return dmat, qdec, kdec


def _row_to_col(row):
    n = row.shape[1]
    eye = lax.broadcasted_iota(jnp.int32, (n, n), 0) == lax.broadcasted_iota(jnp.int32, (n, n), 1)
    return jnp.sum(jnp.where(eye, jnp.broadcast_to(row, (n, n)), 0.0), axis=1, keepdims=True)


def _ret_step_kernel(q_ref, k_ref, v_ref, g_ref, c_ref, s_ref, st_ref, o_ref, sn_ref):
    q = _rope_ret(q_ref[...], c_ref[...], s_ref[...])
    k = _rope_ret(k_ref[...], c_ref[...], s_ref[...]) * (RET_DK ** -0.5)
    for h in range(RET_HEADS):
        g = _ret_decay(h)
        qrow = q[:, h * RET_DK:(h + 1) * RET_DK]
        krow = k[:, h * RET_DK:(h + 1) * RET_DK]
        qcol = _row_to_col(qrow)
        kcol = _row_to_col(krow)
        v = v_ref[:, h * RET_DV:(h + 1) * RET_DV]
        s_old = st_ref[h]
        inner = jnp.sum(qrow * krow, axis=1, keepdims=True)
        o = inner * v + jnp.sum((qcol * g) * s_old, axis=0, keepdims=True)
        sn_ref[h] = g * s_old + kcol * v
        o_ref[:, h * RET_DV:(h + 1) * RET_DV] = _ln(o) * _silu(g_ref[:, h * RET_DV:(h + 1) * RET_DV])


def _ret_step(z, tabs, state):
    n = z.shape[0]
    tab = pl.BlockSpec((None, 1, RET_QK_DIM), lambda b: (b, 0, 0))
    st = pl.BlockSpec((None, RET_HEADS, RET_DK, RET_DV), lambda b: (b, 0, 0, 0))
    return pl.pallas_call(
        _ret_step_kernel,
        grid=(n,),
        in_specs=[pl.BlockSpec((None, 1, RET_QK_DIM), lambda b: (b, 0, OFF_RQ // RET_QK_DIM)),
                  pl.BlockSpec((None, 1, RET_QK_DIM), lambda b: (b, 0, OFF_RK // RET_QK_DIM)),
                  pl.BlockSpec((None, 1, RET_V_DIM), lambda b: (b, 0, OFF_RV // RET_V_DIM)),
                  pl.BlockSpec((None, 1, RET_V_DIM), lambda b: (b, 0, OFF_RG // RET_V_DIM)),
                  tab, tab, st],
        out_specs=[pl.BlockSpec((None, 1, RET_V_DIM), lambda b: (b, 0, 0)), st],
        out_shape=[jax.ShapeDtypeStruct((n, 1, RET_V_DIM), F32),
                   jax.ShapeDtypeStruct(state.shape, F32)],
        compiler_params=_params(("arbitrary",)),
    )(z, z, z, z, *tabs, state)


def _merge_kernel(*refs, tm, seq_mode, alpha):
    if seq_mode:
        (x_ref, sc_ref, sh_ref, g1_ref, a_ref, r_ref, cb_ref, cc_ref, cx_ref, cw_ref, wg_ref, wbr_ref, wo_ref,
         lg_ref, lb_ref, o_ref, cs_ref, carry) = refs
    else:
        (x_ref, sc_ref, sh_ref, g1_ref, a_ref, r_ref, cb_ref, cc_ref, cx_ref, cw_ref, wg_ref, wbr_ref, wo_ref,
         lg_ref, lb_ref, p0_ref, p1_ref, o_ref, cs_ref) = refs
    d = x_ref.shape[-1]
    x = x_ref[...]
    u = (x * (1.0 + sc_ref[...]) + sh_ref[...]).astype(BF16)
    s = cc_ref[...] * cx_ref[...]
    if seq_mode:
        @pl.when(pl.program_id(1) == 0)
        def _():
            carry[...] = jnp.zeros_like(carry)

        rid = lax.broadcasted_iota(jnp.int32, s.shape, 0)
        prev0, prev1 = carry[0:1, :], carry[1:2, :]
        s1 = jnp.where(rid == 0, prev1, pltpu.roll(s, 1, 0))
        s2 = jnp.where(rid == 0, prev0, jnp.where(rid == 1, prev1, pltpu.roll(s, 2, 0)))
        carry[...] = s[tm - 2:tm, :]
        cs_ref[...] = s[tm - 2:tm, :]
    else:
        s2, s1 = p0_ref[...], p1_ref[...]
        cs_ref[...] = s
    y = cw_ref[0:1, :] * s2 + cw_ref[1:2, :] * s1 + cw_ref[2:3, :] * s
    c_out = cb_ref[...] * y
    merged = None
    for n, branch in enumerate((a_ref[...], r_ref[...], c_out)):
        gt = jnp.dot(u, wg_ref[:, n * d:(n + 1) * d], preferred_element_type=F32)
        term = _sigmoid(gt) * jnp.dot(branch.astype(BF16), wbr_ref[n], preferred_element_type=F32)
        merged = term if merged is None else merged + term
    m = jnp.dot(merged.astype(BF16), wo_ref[...], preferred_element_type=F32)
    o_ref[...] = _ln(alpha * x + g1_ref[...] * m) * lg_ref[...] + lb_ref[...]


def _merge(x, sc, sh, g1, a_out, r_out, z, conv_w, w_gate, w_br, w_o, ln_g, ln_b, alpha, conv_prev=None):
    bsz, t, d = x.shape
    seq_mode = conv_prev is None
    tm = min(t, 512)
    tok512 = lambda cblk: pl.BlockSpec((None, tm, CONV_DIM), lambda b, i: (b, i, cblk))
    full2 = lambda a: pl.BlockSpec(a.shape, lambda b, i: (0, 0))
    in_specs = [pl.BlockSpec((None, tm, d), lambda b, i: (b, i, 0)),
                _mod_spec(sc, tm, 2), _mod_spec(sh, tm, 2), _mod_spec(g1, tm, 2),
                tok512(0), tok512(0), tok512(OFF_CB // CONV_DIM), tok512(OFF_CC // CONV_DIM),
                tok512(OFF_CX // CONV_DIM),
                full2(conv_w), full2(w_gate), pl.BlockSpec(w_br.shape, lambda b, i: (0, 0, 0)), full2(w_o),
                full2(ln_g), full2(ln_b)]
    args = [x, sc, sh, g1, a_out, r_out, z, z, z, conv_w, w_gate, w_br, w_o, ln_g, ln_b]
    scratch = []
    if seq_mode:
        cs_spec = pl.BlockSpec((None, CONV_W - 1, CONV_DIM), lambda b, i: (b, 0, 0))
        cs_shape = jax.ShapeDtypeStruct((bsz, CONV_W - 1, CONV_DIM), F32)
        scratch = [pltpu.VMEM((CONV_W - 1, CONV_DIM), F32)]
    else:
        in_specs += [tok512(0), tok512(0)]
        args += list(conv_prev)
        cs_spec = tok512(0)
        cs_shape = jax.ShapeDtypeStruct((bsz, t, CONV_DIM), F32)
    return pl.pallas_call(
        functools.partial(_merge_kernel, tm=tm, seq_mode=seq_mode, alpha=alpha),
        grid=(bsz, t // tm),
        in_specs=in_specs,
        out_specs=[pl.BlockSpec((None, tm, d), lambda b, i: (b, i, 0)), cs_spec],
        out_shape=[jax.ShapeDtypeStruct((bsz, t, d), F32), cs_shape],
        scratch_shapes=scratch,
        compiler_params=_params(("arbitrary", "arbitrary")),
    )(*args)


def _ffn_kernel(x_ref, sc_ref, sh_ref, g2_ref, wa_ref, wb_ref, wo_ref, lg_ref, lb_ref, o_ref, u_ref, acc_ref, *,
                alpha):
    f = pl.program_id(2)

    @pl.when(f == 0)
    def _():
        u_ref[...] = (x_ref[...] * (1.0 + sc_ref[...]) + sh_ref[...]).astype(BF16)
        acc_ref[...] = jnp.zeros_like(acc_ref)

    u = u_ref[...]
    a = jnp.dot(u, wa_ref[...], preferred_element_type=F32)
    b = jnp.dot(u, wb_ref[...], preferred_element_type=F32)
    acc_ref[...] += jnp.dot((_silu(a) * b).astype(BF16), wo_ref[...], preferred_element_type=F32)

    @pl.when(f == pl.num_programs(2) - 1)
    def _():
        o_ref[...] = _ln(alpha * x_ref[...] + g2_ref[...] * acc_ref[...]) * lg_ref[...] + lb_ref[...]


def _ffn(x, sc, sh, g2, w_in, w_out, ln_g, ln_b, alpha):
    bsz, t, d = x.shape
    tm = min(t, 512)
    tf = D_FF // 2
    nf = D_FF // tf
    full2 = lambda a: pl.BlockSpec(a.shape, lambda b, i, f: (0, 0))
    return pl.pallas_call(
        functools.partial(_ffn_kernel, alpha=alpha),
        grid=(bsz, t // tm, nf),
        in_specs=[pl.BlockSpec((None, tm, d), lambda b, i, f: (b, i, 0)),
                  _mod_spec(sc, tm, 3), _mod_spec(sh, tm, 3), _mod_spec(g2, tm, 3),
                  pl.BlockSpec((d, tf), lambda b, i, f: (0, f)),
                  pl.BlockSpec((d, tf), lambda b, i, f: (0, nf + f)),
                  pl.BlockSpec((tf, d), lambda b, i, f: (f, 0)),
                  full2(ln_g), full2(ln_b)],
        out_specs=pl.BlockSpec((None, tm, d), lambda b, i, f: (b, i, 0)),
        out_shape=jax.ShapeDtypeStruct((bsz, t, d), F32),
        scratch_shapes=[pltpu.VMEM((tm, d), BF16), pltpu.VMEM((tm, d), F32)],
        compiler_params=_params(("arbitrary", "arbitrary", "arbitrary")),
    )(x, sc, sh, g2, w_in, w_in, w_out, ln_g, ln_b)


def _col_to_row(col):
    n = col.shape[0]
    eye = lax.broadcasted_iota(jnp.int32, (n, n), 0) == lax.broadcasted_iota(jnp.int32, (n, n), 1)
    return jnp.sum(jnp.where(eye, jnp.broadcast_to(col, (n, n)), 0.0), axis=0, keepdims=True)


def _dec_scan_kernel(pt_ref, q_ref, ck_ref, o_ref, buf, sem, *, layer, n_pages, cp):
    b = pl.program_id(0)
    n_seq = pl.num_programs(0)
    n_chunks = n_pages // cp
    cb = cp // 2
    nbp = n_pages // 2

    def copies(w):
        slot = w % 2
        return [pltpu.make_async_copy(ck_ref.at[layer, pt_ref[w * cp + p]], buf.at[slot, p], sem.at[slot])
                for p in range(cp)]

    @pl.when(b == 0)
    def _():
        for cpy in copies(0):
            cpy.start()

    qcols = [_row_to_col(q_ref[h:h + 1, :]) for h in range(ATT_HEADS)]
    head = lax.broadcasted_iota(jnp.int32, (ATT_HEADS, LANES), 0)
    lane = lax.broadcasted_iota(jnp.int32, (ATT_HEADS, LANES), 1)

    def body(c, gates):
        w = b * n_chunks + c
        slot = w % 2

        @pl.when(w + 1 < n_seq * n_chunks)
        def _():
            for cpy in copies(w + 1):
                cpy.start()

        for cpy in copies(w):
            cpy.wait()
        for jb in range(cb):
            per_token = jnp.zeros((ATT_HEADS, LANES), F32)
            for h in range(ATT_HEADS):
                kt = buf[slot, 2 * jb, h] + buf[slot, 2 * jb + 1, h]
                per_token = jnp.where(head == h, jnp.sum(kt * qcols[h], axis=0, keepdims=True), per_token)
            g = jnp.sum(per_token, axis=1, keepdims=True) * (1.0 / MOBA_BLOCK)
            gates = jnp.where(lane == c * cb + jb, g, gates)
        return gates

    gates = lax.fori_loop(0, n_chunks, body, jnp.full((ATT_HEADS, LANES), NEG_INF, F32))
    res = jnp.zeros((ATT_HEADS, LANES), jnp.int32)
    for r in range(min(MOBA_TOPK, nbp)):
        mx = jnp.max(gates, axis=1, keepdims=True)
        ix = jnp.min(jnp.where(gates == mx, lane, LANES), axis=1, keepdims=True)
        gates = jnp.where(lane == ix, NEG_INF, gates)
        res = jnp.where(lane == r, ix, res)
    o_ref[...] = res


def _dec_scan(page_table, q_hd, cache_kt, layer):
    n, n_pages = page_table.shape
    page_size = cache_kt.shape[4]
    assert page_size * 2 == MOBA_BLOCK and page_size == LANES and n_pages // 2 <= LANES
    cp = min(16, n_pages)
    grid_spec = pltpu.PrefetchScalarGridSpec(
        num_scalar_prefetch=1,
        grid=(n,),
        in_specs=[pl.BlockSpec((None, ATT_HEADS, ATT_HD), lambda b, pt: (b, 0, 0)),
                  pl.BlockSpec(memory_space=pl.ANY)],
        out_specs=pl.BlockSpec((None, ATT_HEADS, LANES), lambda b, pt: (b, 0, 0)),
        scratch_shapes=[pltpu.VMEM((2, cp, ATT_HEADS, ATT_HD, page_size), F32),
                        pltpu.SemaphoreType.DMA((2,))],
    )
    return pl.pallas_call(
        functools.partial(_dec_scan_kernel, layer=layer, n_pages=n_pages, cp=cp),
        grid_spec=grid_spec,
        out_shape=jax.ShapeDtypeStruct((n, ATT_HEADS, LANES), jnp.int32),
        compiler_params=_params(("arbitrary",)),
    )(page_table.reshape(-1), q_hd, cache_kt)


def _dec_attend_kernel(pt_ref, pick_ref, q_ref, kn_ref, vn_ref, ck_ref, cv_ref, o_ref, kb, vb, sem, *,
                       layer, n_pages, n_sel):
    b = pl.program_id(0)
    n_tiles = 2 * n_sel

    def copies(h, slot):
        out = []
        for r in range(n_sel):
            blk = pick_ref[(b * ATT_HEADS + h) * n_sel + r]
            for p in range(2):
                page = pt_ref[b * n_pages + 2 * blk + p]
                out.append(pltpu.make_async_copy(ck_ref.at[layer, page, h], kb.at[slot, 2 * r + p], sem.at[2 * slot]))
                out.append(pltpu.make_async_copy(cv_ref.at[layer, page, h], vb.at[slot, 2 * r + p],
                                                 sem.at[2 * slot + 1]))
        return out

    for cpy in copies(0, 0):
        cpy.start()
    for h in range(ATT_HEADS):
        slot = h % 2
        if h + 1 < ATT_HEADS:
            for cpy in copies(h + 1, 1 - slot):
                cpy.start()
        qrow = q_ref[h:h + 1, :]
        qcol = _row_to_col(qrow)
        s_new = jnp.sum(kn_ref[h:h + 1, :] * qrow, axis=1, keepdims=True)
        vcol_new = _row_to_col(vn_ref[h:h + 1, :])
        for cpy in copies(h, slot):
            cpy.wait()
        s = [jnp.sum(kb[slot, t] * qcol, axis=0, keepdims=True) for t in range(n_tiles)]
        m = s_new
        for st in s:
            m = jnp.maximum(m, jnp.max(st, axis=1, keepdims=True))
        p_new = jnp.exp(s_new - m)
        den = p_new
        num = p_new * vcol_new
        for t, st in enumerate(s):
            p = jnp.exp(st - m)
            den = den + jnp.sum(p, axis=1, keepdims=True)
            num = num + jnp.sum(vb[slot, t] * p, axis=1, keepdims=True)
        o_ref[h:h + 1, :] = _col_to_row(num / den)


def _dec_attend(page_table, picks, q_hd, k_new, v_new, cache_kt, cache_vt, layer):
    n, n_pages = page_table.shape
    page_size = cache_kt.shape[4]
    n_sel = min(MOBA_TOPK, n_pages // 2)
    tok = pl.BlockSpec((None, ATT_HEADS, ATT_HD), lambda b, pt, pk: (b, 0, 0))
    grid_spec = pltpu.PrefetchScalarGridSpec(
        num_scalar_prefetch=2,
        grid=(n,),
        in_specs=[tok, tok, tok, pl.BlockSpec(memory_space=pl.ANY), pl.BlockSpec(memory_space=pl.ANY)],
        out_specs=tok,
        scratch_shapes=[pltpu.VMEM((2, 2 * n_sel, ATT_HD, page_size), F32),
                        pltpu.VMEM((2, 2 * n_sel, ATT_HD, page_size), F32),
                        pltpu.SemaphoreType.DMA((4,))],
    )
    return pl.pallas_call(
        functools.partial(_dec_attend_kernel, layer=layer, n_pages=n_pages, n_sel=n_sel),
        grid_spec=grid_spec,
        out_shape=jax.ShapeDtypeStruct((n, ATT_HEADS, ATT_HD), F32),
        compiler_params=_params(("arbitrary",)),
    )(page_table.reshape(-1), picks, q_hd, k_new, v_new, cache_kt, cache_vt)


def _att_rope_tables(pos):
    half = ROPE_DIM // 2
    inv = ROPE_THETA ** (-2.0 * jnp.arange(half, dtype=F32) / ROPE_DIM)
    ang = pos.astype(F32)[:, None] * inv[None, :]
    cos, sin = jnp.cos(ang), jnp.sin(ang)
    t = pos.shape[0]
    rest = ATT_HD - ROPE_DIM
    ca = jnp.concatenate([cos, cos, jnp.ones((t, rest), F32)], axis=1)
    cm = jnp.concatenate([-sin, jnp.zeros((t, half + rest), F32)], axis=1)
    cp = jnp.concatenate([jnp.zeros((t, half), F32), sin, jnp.zeros((t, rest), F32)], axis=1)
    rep = LANES // ATT_HD
    return tuple(jnp.tile(a, (1, rep)) for a in (ca, cm, cp))


def _ret_rope_tables(pos):
    half = RET_DK // 2
    inv = RET_THETA ** (-2.0 * jnp.arange(half, dtype=F32) / RET_DK)
    ang = pos.astype(F32)[:, None] * inv[None, :]
    cos, sin = jnp.cos(ang), jnp.sin(ang)
    c = jnp.concatenate([cos, cos], axis=1)
    s = jnp.concatenate([-sin, sin], axis=1)
    return jnp.tile(c, (1, RET_HEADS)), jnp.tile(s, (1, RET_HEADS))


def kernel(x_prompt, x_sample, cache_k, cache_v, state_ret, state_conv, page_table, c_prompt, c_sample, w_ada, b_ada,
           w_in, conv_w, w_br, w_o, ln1_g, ln1_b, w_ffn_in, w_ffn_out, ln2_g, ln2_b):
    depth = w_in.shape[0]
    alpha = (2 * depth) ** 0.25
    bsz, seq, d = x_prompt.shape
    nd = x_sample.shape[0]
    assert x_sample.shape[1] == 1, "one new token per decode sequence"
    n_pool, page_size = cache_k.shape[1], cache_k.shape[2]
    past_len = page_table.shape[1] * page_size

    ada = _ada(jnp.concatenate([c_prompt, c_sample], axis=0), w_ada, b_ada)
    w_in_b = w_in[:, :, :N_MIX].astype(BF16)
    w_gate_b = w_in[:, :, N_MIX:].astype(BF16)
    w_br_b, w_o_b = w_br.astype(BF16), w_o.astype(BF16)
    w_fi_b, w_fo_b = w_ffn_in.astype(BF16), w_ffn_out.astype(BF16)
    cache_kt = jnp.transpose(cache_k, (0, 1, 3, 4, 2))
    cache_vt = jnp.transpose(cache_v, (0, 1, 3, 4, 2))

    pos_p = jnp.arange(seq, dtype=jnp.int32)
    pos_s = jnp.full((nd,), past_len, jnp.int32)
    att_tab_p, ret_tab_p = _att_rope_tables(pos_p), _ret_rope_tables(pos_p)
    att_tab_s = _att_rope_tables(pos_s)
    ret_tab_s = tuple(a.reshape(nd, 1, RET_QK_DIM) for a in _ret_rope_tables(pos_s))
    ret_consts = _ret_consts()

    xp = x_prompt
    xs = x_sample.reshape(1, nd, d)
    outs = {k: [] for k in ("kp", "vp", "ks", "vs", "rp", "rs", "cp", "cs")}
    for l in range(depth):
        mods_p = [m.reshape(bsz, 1, d) for m in jnp.split(ada[l, :bsz], 6, axis=-1)]
        mods_s = [m.reshape(1, nd, d) for m in jnp.split(ada[l, bsz:], 6, axis=-1)]
        lg1, lb1 = ln1_g[l].reshape(1, d), ln1_b[l].reshape(1, d)
        lg2, lb2 = ln2_g[l].reshape(1, d), ln2_b[l].reshape(1, d)

        sh1, sc1, g1, sh2, sc2, g2 = mods_p
        z = _inproj(xp, sc1, sh1, w_in_b[l])
        q_t, k_hm, v_t, k_tok, v_tok = _attn_prep(z, att_tab_p, head_major=True)
        kmean = _kmean(k_tok)
        kmean_hm = kmean.reshape(bsz, -1, ATT_HEADS, ATT_HD).transpose(0, 2, 1, 3)
        a_out = _moba(q_t, k_hm, v_t, kmean_hm)
        r_out, r_state = _retention(z, ret_tab_p, ret_consts)
        xp, c_state = _merge(xp, sc1, sh1, g1, a_out, r_out, z, conv_w[l], w_gate_b[l], w_br_b[l], w_o_b[l], lg1, lb1,
                             alpha)
        xp = _ffn(xp, sc2, sh2, g2, w_fi_b[l], w_fo_b[l], lg2, lb2, alpha)
        outs["kp"].append(k_tok.reshape(bsz, seq, ATT_HEADS, ATT_HD))
        outs["vp"].append(v_tok.reshape(bsz, seq, ATT_HEADS, ATT_HD))
        outs["rp"].append(r_state)
        outs["cp"].append(c_state)

        sh1, sc1, g1, sh2, sc2, g2 = mods_s
        z = _inproj(xs, sc1, sh1, w_in_b[l])
        q_tok, k_new, v_new = _attn_prep(z, att_tab_s, head_major=False)
        q_tok, k_new, v_new = (a.reshape(nd, ATT_HEADS, ATT_HD) for a in (q_tok, k_new, v_new))
        picks = _dec_scan(page_table, q_tok, cache_kt, l)
        n_sel = min(MOBA_TOPK, page_table.shape[1] // 2)
        a_out = _dec_attend(page_table, picks[:, :, :n_sel].reshape(-1), q_tok, k_new, v_new, cache_kt, cache_vt, l)
        r_out, r_state = _ret_step(z.reshape(nd, 1, N_MIX), ret_tab_s, state_ret[l])
        prev = (state_conv[l][:, 0, :].reshape(1, nd, CONV_DIM), state_conv[l][:, 1, :].reshape(1, nd, CONV_DIM))
        xs, s_new = _merge(xs, sc1, sh1, g1, a_out.reshape(1, nd, ATT_DIM), r_out.reshape(1, nd, RET_V_DIM), z,
                           conv_w[l], w_gate_b[l], w_br_b[l], w_o_b[l], lg1, lb1, alpha, conv_prev=prev)
        xs = _ffn(xs, sc2, sh2, g2, w_fi_b[l], w_fo_b[l], lg2, lb2, alpha)
        outs["ks"].append(k_new.reshape(nd, 1, ATT_HEADS, ATT_HD))
        outs["vs"].append(v_new.reshape(nd, 1, ATT_HEADS, ATT_HD))
        outs["rs"].append(r_state)
        outs["cs"].append(jnp.stack([state_conv[l][:, 1, :], s_new.reshape(nd, CONV_DIM)], axis=1))

    st = {k: jnp.stack(v) for k, v in outs.items()}
    return (xp, xs.reshape(nd, 1, d), st["kp"], st["vp"], st["ks"], st["vs"], st["rp"], st["rs"], st["cp"], st["cs"])
```

```python
import functools

import jax
import jax.numpy as jnp
from jax import lax
from jax.experimental import pallas as pl
from jax.experimental.pallas import tpu as pltpu

F32 = jnp.float32
BF16 = jnp.bfloat16

D_MODEL = 1024
ATT_HEADS = 8
ATT_HD = 64
ATT_DIM = ATT_HEADS * ATT_HD
ROPE_DIM = ATT_HD // 4
ROPE_THETA = 500000.0
MOBA_BLOCK = 256
MOBA_TOPK = 3
RET_HEADS = 4
RET_DK = 64
RET_DV = 128
RET_QK_DIM = RET_HEADS * RET_DK
RET_V_DIM = RET_HEADS * RET_DV
RET_THETA = 10000.0
RET_CHUNK = 128
CONV_DIM = 512
CONV_W = 3
N_BRANCH = 3
D_FF = 2816
LN_EPS = 1e-5
N_MIX = 3 * ATT_DIM + 2 * RET_QK_DIM + 2 * RET_V_DIM + 3 * CONV_DIM

OFF_AQ = 0
OFF_AK = OFF_AQ + ATT_DIM
OFF_AV = OFF_AK + ATT_DIM
OFF_RQ = OFF_AV + ATT_DIM
OFF_RK = OFF_RQ + RET_QK_DIM
OFF_RV = OFF_RK + RET_QK_DIM
OFF_RG = OFF_RV + RET_V_DIM
OFF_CB = OFF_RG + RET_V_DIM
OFF_CC = OFF_CB + CONV_DIM
OFF_CX = OFF_CC + CONV_DIM

LANES = 128
BF16_SUBLANES = 16
VT_ROWS = ATT_HD + BF16_SUBLANES
LOG2E = 1.4426950408889634
HPS = 4
VMEM_LIMIT = 48 * 1024 * 1024
NEG_INF = float("-inf")


def _params(semantics, vmem=VMEM_LIMIT):
    return pltpu.CompilerParams(dimension_semantics=semantics, vmem_limit_bytes=vmem)


def _ln(y):
    mu = jnp.mean(y, axis=-1, keepdims=True)
    yc = y - mu
    return yc * lax.rsqrt(jnp.mean(yc * yc, axis=-1, keepdims=True) + LN_EPS)


def _sigmoid(x):
    return 1.0 / (1.0 + jnp.exp(-x))


def _silu(x):
    return x * _sigmoid(x)


def _mod_spec(mod, tm, ngrid):
    per_row = mod.shape[1] != 1
    rows = tm if per_row else 1
    if ngrid == 2:
        return pl.BlockSpec((None, rows, mod.shape[2]), lambda b, i: (b, i if per_row else 0, 0))
    return pl.BlockSpec((None, rows, mod.shape[2]), lambda b, i, j: (b, i if per_row else 0, 0))


def _ada_kernel(c_ref, w_ref, b_ref, o_ref):
    cond = _silu(c_ref[...]).astype(BF16)
    o_ref[...] = jnp.dot(cond, w_ref[...].astype(BF16), preferred_element_type=F32) + b_ref[...]


def _ada(c, w_ada, b_ada):
    depth, d, n = w_ada.shape
    rows = c.shape[0]
    tn = 1536
    return pl.pallas_call(
        _ada_kernel,
        grid=(depth, n // tn),
        in_specs=[pl.BlockSpec((rows, d), lambda l, j: (0, 0)),
                  pl.BlockSpec((None, d, tn), lambda l, j: (l, 0, j)),
                  pl.BlockSpec((None, 1, tn), lambda l, j: (l, 0, j))],
        out_specs=pl.BlockSpec((None, rows, tn), lambda l, j: (l, 0, j)),
        out_shape=jax.ShapeDtypeStruct((depth, rows, n), F32),
        compiler_params=_params(("arbitrary", "arbitrary")),
    )(c, w_ada, b_ada.reshape(depth, 1, n))


def _inproj_kernel(x_ref, sc_ref, sh_ref, w_ref, o_ref, u_ref):
    @pl.when(pl.program_id(2) == 0)
    def _():
        u_ref[...] = (x_ref[...] * (1.0 + sc_ref[...]) + sh_ref[...]).astype(BF16)

    o_ref[...] = jnp.dot(u_ref[...], w_ref[...], preferred_element_type=F32)


def _inproj(x, sc, sh, w):
    bsz, t, d = x.shape
    n = w.shape[1]
    tm = min(t, 1024)
    tn = 1536
    return pl.pallas_call(
        _inproj_kernel,
        grid=(bsz, t // tm, n // tn),
        in_specs=[pl.BlockSpec((None, tm, d), lambda b, i, j: (b, i, 0)),
                  _mod_spec(sc, tm, 3), _mod_spec(sh, tm, 3),
                  pl.BlockSpec((d, tn), lambda b, i, j: (0, j))],
        out_specs=pl.BlockSpec((None, tm, tn), lambda b, i, j: (b, i, j)),
        out_shape=jax.ShapeDtypeStruct((bsz, t, n), F32),
        scratch_shapes=[pltpu.VMEM((tm, d), BF16)],
        compiler_params=_params(("arbitrary", "arbitrary", "arbitrary")),
    )(x, sc, sh, w)


def _rope_att(x, ca, cm, cp):
    return x * ca + pltpu.roll(x, LANES - ROPE_DIM // 2, 1) * cm + pltpu.roll(x, ROPE_DIM // 2, 1) * cp


def _attn_prep_kernel(q_ref, k_ref, v_ref, ca_ref, cm_ref, cp_ref, *out_refs, head_major, tm):
    ca, cm, cp = ca_ref[...], cm_ref[...], cp_ref[...]
    scale = ATT_HD ** -0.5
    if head_major:
        qt_ref, kh_ref, vt_ref, ko_ref, vo_ref = out_refs
    else:
        qo_ref, ko_ref, vo_ref = out_refs
    vo_ref[...] = v_ref[...]
    for c in range(ATT_DIM // LANES):
        sl = slice(c * LANES, (c + 1) * LANES)
        q = _rope_att(q_ref[:, sl], ca, cm, cp) * scale
        k = _rope_att(k_ref[:, sl], ca, cm, cp)
        ko_ref[:, sl] = k
        if head_major:
            qt = q.T
            vt = v_ref[:, sl].T.astype(BF16)
            for hh in range(2):
                hs = slice(hh * ATT_HD, (hh + 1) * ATT_HD)
                kh_ref[2 * c + hh] = k[:, hs].astype(BF16)
                for j in range(tm // MOBA_BLOCK):
                    cols = slice(j * MOBA_BLOCK, (j + 1) * MOBA_BLOCK)
                    qt_ref[2 * c + hh, j] = qt[hs, cols]
                    vt_ref[2 * c + hh, j, 0:ATT_HD, :] = vt[hs, cols]
                    vt_ref[2 * c + hh, j, ATT_HD:VT_ROWS, :] = jnp.ones((VT_ROWS - ATT_HD, MOBA_BLOCK), BF16)
        else:
            qo_ref[:, sl] = q


def _attn_prep(z, tabs, head_major):
    bsz, t, _ = z.shape
    tm = min(t, 512)
    cq, ck, cv = OFF_AQ // ATT_DIM, OFF_AK // ATT_DIM, OFF_AV // ATT_DIM
    tok = pl.BlockSpec((None, tm, ATT_DIM), lambda b, i: (b, i, 0))
    tab = pl.BlockSpec((tm, LANES), lambda b, i: (i, 0))
    tok_shape = jax.ShapeDtypeStruct((bsz, t, ATT_DIM), F32)
    if head_major:
        nblk = tm // MOBA_BLOCK
        hm = pl.BlockSpec((None, ATT_HEADS, tm, ATT_HD), lambda b, i: (b, 0, i, 0))
        tr = pl.BlockSpec((None, ATT_HEADS, nblk, ATT_HD, MOBA_BLOCK), lambda b, i: (b, 0, i, 0, 0))
        trv = pl.BlockSpec((None, ATT_HEADS, nblk, VT_ROWS, MOBA_BLOCK), lambda b, i: (b, 0, i, 0, 0))
        nb = t // MOBA_BLOCK
        out_specs = [tr, hm, trv, tok, tok]
        out_shape = [jax.ShapeDtypeStruct((bsz, ATT_HEADS, nb, ATT_HD, MOBA_BLOCK), F32),
                     jax.ShapeDtypeStruct((bsz, ATT_HEADS, t, ATT_HD), BF16),
                     jax.ShapeDtypeStruct((bsz, ATT_HEADS, nb, VT_ROWS, MOBA_BLOCK), BF16), tok_shape, tok_shape]
    else:
        out_specs = [tok, tok, tok]
        out_shape = [tok_shape, tok_shape, tok_shape]
    return pl.pallas_call(
        functools.partial(_attn_prep_kernel, head_major=head_major, tm=tm),
        grid=(bsz, t // tm),
        in_specs=[pl.BlockSpec((None, tm, ATT_DIM), lambda b, i: (b, i, cq)),
                  pl.BlockSpec((None, tm, ATT_DIM), lambda b, i: (b, i, ck)),
                  pl.BlockSpec((None, tm, ATT_DIM), lambda b, i: (b, i, cv)),
                  tab, tab, tab],
        out_specs=out_specs,
        out_shape=out_shape,
        compiler_params=_params(("arbitrary", "arbitrary")),
    )(z, z, z, *tabs)


def _kmean_kernel(k_ref, o_ref, *, nblk):
    for j in range(nblk):
        o_ref[j:j + 1, :] = jnp.mean(k_ref[j * MOBA_BLOCK:(j + 1) * MOBA_BLOCK, :], axis=0, keepdims=True)


def _kmean(k_tok):
    bsz, t, _ = k_tok.shape
    rows = min(t, 8 * MOBA_BLOCK)
    nblk = rows // MOBA_BLOCK
    return pl.pallas_call(
        functools.partial(_kmean_kernel, nblk=nblk),
        grid=(bsz, t // rows),
        in_specs=[pl.BlockSpec((None, rows, ATT_DIM), lambda b, i: (b, i, 0))],
        out_specs=pl.BlockSpec((None, nblk, ATT_DIM), lambda b, i: (b, i, 0)),
        out_shape=jax.ShapeDtypeStruct((bsz, t // MOBA_BLOCK, ATT_DIM), F32),
        compiler_params=_params(("arbitrary", "arbitrary")),
    )(k_tok)


def _moba_kernel(qt_ref, k_ref, vt_ref, km_ref, o_ref, s_scr, p_scr, *, nb):
    i = pl.program_id(2)
    blk = MOBA_BLOCK
    key = lax.broadcasted_iota(jnp.int32, (blk, blk), 0)
    qry = lax.broadcasted_iota(jnp.int32, (blk, blk), 1)
    bid = lax.broadcasted_iota(jnp.int32, (nb, blk), 0)
    qtbs, pickss, init = [], [], []

    def scores(j):
        off = pl.multiple_of(j * blk, blk)
        return [jnp.dot(k_ref[hh, pl.ds(off, blk), :], qtbs[hh], preferred_element_type=F32) for hh in range(HPS)]

    qts = [qt_ref[hh] for hh in range(HPS)]
    qtbs.extend((qt * LOG2E).astype(BF16) for qt in qts)

    for hh, s in enumerate(scores(0)):
        s_scr[0, hh] = s
        p_scr[hh] = jnp.zeros((blk, blk), BF16)
    own = scores(i)

    for hh in range(HPS):
        gate = jnp.dot(km_ref[hh], qts[hh], precision=lax.Precision.HIGHEST, preferred_element_type=F32)
        gate = jnp.where(bid < i, gate, NEG_INF)
        picks = []
        for r in range(MOBA_TOPK):
            mx = jnp.max(gate, axis=0, keepdims=True)
            ix = jnp.min(jnp.where(gate == mx, bid, nb), axis=0, keepdims=True)
            gate = jnp.where(bid == ix, NEG_INF, gate)
            picks.append(jnp.where(r < i, ix, -1))
        pickss.append(picks)

    for hh, s in enumerate(own):
        s = jnp.where(key <= qry, s, NEG_INF)
        m0 = jnp.max(s, axis=0, keepdims=True)
        p = jnp.exp2(s - m0)
        acc0 = jnp.dot(vt_ref[hh, i], p.astype(BF16), preferred_element_type=F32)
        init.append((m0, acc0, jnp.ones_like(m0)))

    def finish(jp, hh, acc, alpha):
        return alpha * acc + jnp.dot(vt_ref[hh, jp], p_scr[hh], preferred_element_type=F32)

    def step(j, carry, cur):
        jp = jnp.maximum(j - 1, 0)
        lagged = [finish(jp, hh, carry[hh][1], carry[hh][2]) for hh in range(HPS)]
        nxt = scores(jnp.minimum(j + 1, nb - 1))
        new = []
        for hh in range(HPS):
            m = carry[hh][0]
            picks = pickss[hh]
            live = (picks[0] == j) | (picks[1] == j) | (picks[2] == j)
            sj = s_scr[cur, hh]
            m_new = jnp.maximum(m, jnp.where(live, jnp.max(sj, axis=0, keepdims=True), NEG_INF))
            alpha = jnp.exp2(m - m_new)
            pj = jnp.exp2(sj - jnp.where(live, m_new, jnp.inf))
            p_scr[hh] = pj.astype(BF16)
            s_scr[1 - cur, hh] = nxt[hh]
            new.append((m_new, lagged[hh], alpha))
        return tuple(new)

    def pair(t, carry):
        return step(2 * t + 1, step(2 * t, carry, 0), 1)

    trips = (i + 1) // 2
    fin = lax.fori_loop(0, trips, pair, tuple(init))
    j_last = jnp.maximum(2 * trips - 1, 0)
    outs = []
    for hh, (_, acc, alpha) in enumerate(fin):
        acc = finish(j_last, hh, acc, alpha)
        outs.append(acc[0:ATT_HD] / acc[ATT_HD:ATT_HD + 1])
    o_ref[...] = jnp.concatenate(outs, axis=0).T


def _moba(qt, k_hm, vt, kmean_hm):
    bsz, nh, nb, hd, blk = qt.shape
    t = nb * blk
    return pl.pallas_call(
        functools.partial(_moba_kernel, nb=nb),
        grid=(bsz, nh // HPS, nb),
        in_specs=[pl.BlockSpec((None, HPS, None, hd, blk), lambda b, h, i: (b, h, i, 0, 0)),
                  pl.BlockSpec((None, HPS, t, hd), lambda b, h, i: (b, h, 0, 0)),
                  pl.BlockSpec((None, HPS, nb, vt.shape[3], blk), lambda b, h, i: (b, h, 0, 0, 0)),
                  pl.BlockSpec((None, HPS, nb, hd), lambda b, h, i: (b, h, 0, 0))],
        out_specs=pl.BlockSpec((None, blk, HPS * hd), lambda b, h, i: (b, i, h)),
        out_shape=jax.ShapeDtypeStruct((bsz, t, nh * hd), F32),
        scratch_shapes=[pltpu.VMEM((2, HPS, blk, blk), F32), pltpu.VMEM((HPS, blk, blk), BF16)],
        compiler_params=_params(("arbitrary", "arbitrary", "arbitrary")),
    )(qt, k_hm, vt, kmean_hm)


def _rope_ret(x, c, s):
    w = x.shape[1]
    d = lax.broadcasted_iota(jnp.int32, x.shape, 1) % RET_DK
    half = RET_DK // 2
    partner = jnp.where(d < half, pltpu.roll(x, w - half, 1), pltpu.roll(x, half, 1))
    return x * c + partner * s


def _ret_decay(h):
    return 1.0 - 2.0 ** (-5.0 - h)


def _retention_kernel(q_ref, k_ref, v_ref, g_ref, c_ref, s_ref, dm_ref, qd_ref, kd_ref, o_ref, st_ref, state, *, tm):
    ch = RET_CHUNK

    @pl.when(pl.program_id(1) == 0)
    def _():
        state[...] = jnp.zeros_like(state)

    q = _rope_ret(q_ref[...], c_ref[...], s_ref[...])
    k = _rope_ret(k_ref[...], c_ref[...], s_ref[...]) * (RET_DK ** -0.5)
    kt = k.T
    for c in range(tm // ch):
        rows = slice(c * ch, (c + 1) * ch)
        for h in range(RET_HEADS):
            qc = q[rows, h * RET_DK:(h + 1) * RET_DK]
            ktc = kt[h * RET_DK:(h + 1) * RET_DK, rows]
            vc = v_ref[rows, h * RET_DV:(h + 1) * RET_DV].astype(BF16)
            s_old = state[h]
            inner = jnp.dot(qc.astype(BF16), ktc.astype(BF16), preferred_element_type=F32) * dm_ref[h]
            o = jnp.dot(inner.astype(BF16), vc, preferred_element_type=F32)
            o = o + jnp.dot((qc * qd_ref[h]).astype(BF16), s_old.astype(BF16), preferred_element_type=F32)
            c_dec = _ret_decay(h) ** ch
            state[h] = c_dec * s_old + jnp.dot((ktc * kd_ref[h]).astype(BF16), vc, preferred_element_type=F32)
            gate = g_ref[rows, h * RET_DV:(h + 1) * RET_DV]
            o_ref[rows, h * RET_DV:(h + 1) * RET_DV] = _ln(o) * _silu(gate)
    st_ref[...] = state[...]


def _retention(z, tabs, consts):
    bsz, t, _ = z.shape
    tm = min(t, 512)
    dmat, qdec, kdec = consts
    const3 = lambda a: pl.BlockSpec(a.shape, lambda b, i: (0, 0, 0))
    tab = pl.BlockSpec((tm, RET_QK_DIM), lambda b, i: (i, 0))
    return pl.pallas_call(
        functools.partial(_retention_kernel, tm=tm),
        grid=(bsz, t // tm),
        in_specs=[pl.BlockSpec((None, tm, RET_QK_DIM), lambda b, i: (b, i, OFF_RQ // RET_QK_DIM)),
                  pl.BlockSpec((None, tm, RET_QK_DIM), lambda b, i: (b, i, OFF_RK // RET_QK_DIM)),
                  pl.BlockSpec((None, tm, RET_V_DIM), lambda b, i: (b, i, OFF_RV // RET_V_DIM)),
                  pl.BlockSpec((None, tm, RET_V_DIM), lambda b, i: (b, i, OFF_RG // RET_V_DIM)),
                  tab, tab, const3(dmat), const3(qdec), const3(kdec)],
        out_specs=[pl.BlockSpec((None, tm, RET_V_DIM), lambda b, i: (b, i, 0)),
                   pl.BlockSpec((None, RET_HEADS, RET_DK, RET_DV), lambda b, i: (b, 0, 0, 0))],
        out_shape=[jax.ShapeDtypeStruct((bsz, t, RET_V_DIM), F32),
                   jax.ShapeDtypeStruct((bsz, RET_HEADS, RET_DK, RET_DV), F32)],
        scratch_shapes=[pltpu.VMEM((RET_HEADS, RET_DK, RET_DV), F32)],
        compiler_params=_params(("arbitrary", "arbitrary")),
    )(z, z, z, z, *tabs, dmat, qdec, kdec)


def _ret_consts():
    ch = RET_CHUNK
    log_g = jnp.log(1.0 - 2.0 ** (-5.0 - jnp.arange(RET_HEADS, dtype=F32)))
    i = jnp.arange(ch, dtype=F32)
    diff = i[:, None] - i[None, :]
    dmat = jnp.where(diff >= 0, jnp.exp(jnp.maximum(diff, 0.0)[None] * log_g[:, None, None]), 0.0)
    qdec = jnp.exp((i + 1.0)[None, :] * log_g[:, None])[:, :, None]
    kdec = jnp.exp((ch - 1.0 - i)[None, :] * log_g[:, None])[:, None, :]
    return dmat, qdec, kdec


def _row_to_col(row):
    n = row.shape[1]
    eye = lax.broadcasted_iota(jnp.int32, (n, n), 0) == lax.broadcasted_iota(jnp.int32, (n, n), 1)
    return jnp.sum(jnp.where(eye, jnp.broadcast_to(row, (n, n)), 0.0), axis=1, keepdims=True)


def _ret_step_kernel(q_ref, k_ref, v_ref, g_ref, c_ref, s_ref, st_ref, o_ref, sn_ref):
    q = _rope_ret(q_ref[...], c_ref[...], s_ref[...])
    k = _rope_ret(k_ref[...], c_ref[...], s_ref[...]) * (RET_DK ** -0.5)
    for h in range(RET_HEADS):
        g = _ret_decay(h)
        qrow = q[:, h * RET_DK:(h + 1) * RET_DK]
        krow = k[:, h * RET_DK:(h + 1) * RET_DK]
        qcol = _row_to_col(qrow)
        kcol = _row_to_col(krow)
        v = v_ref[:, h * RET_DV:(h + 1) * RET_DV]
        s_old = st_ref[h]
        inner = jnp.sum(qrow * krow, axis=1, keepdims=True)
        o = inner * v + jnp.sum((qcol * g) * s_old, axis=0, keepdims=True)
        sn_ref[h] = g * s_old + kcol * v
        o_ref[:, h * RET_DV:(h + 1) * RET_DV] = _ln(o) * _silu(g_ref[:, h * RET_DV:(h + 1) * RET_DV])


def _ret_step(z, tabs, state):
    n = z.shape[0]
    tab = pl.BlockSpec((None, 1, RET_QK_DIM), lambda b: (b, 0, 0))
    st = pl.BlockSpec((None, RET_HEADS, RET_DK, RET_DV), lambda b: (b, 0, 0, 0))
    return pl.pallas_call(
        _ret_step_kernel,
        grid=(n,),
        in_specs=[pl.BlockSpec((None, 1, RET_QK_DIM), lambda b: (b, 0, OFF_RQ // RET_QK_DIM)),
                  pl.BlockSpec((None, 1, RET_QK_DIM), lambda b: (b, 0, OFF_RK // RET_QK_DIM)),
                  pl.BlockSpec((None, 1, RET_V_DIM), lambda b: (b, 0, OFF_RV // RET_V_DIM)),
                  pl.BlockSpec((None, 1, RET_V_DIM), lambda b: (b, 0, OFF_RG // RET_V_DIM)),
                  tab, tab, st],
        out_specs=[pl.BlockSpec((None, 1, RET_V_DIM), lambda b: (b, 0, 0)), st],
        out_shape=[jax.ShapeDtypeStruct((n, 1, RET_V_DIM), F32),
                   jax.ShapeDtypeStruct(state.shape, F32)],
        compiler_params=_params(("arbitrary",)),
    )(z, z, z, z, *tabs, state)


def _merge_kernel(*refs, tm, seq_mode, alpha):
    if seq_mode:
        (x_ref, sc_ref, sh_ref, g1_ref, a_ref, r_ref, cb_ref, cc_ref, cx_ref, cw_ref, wg_ref, wbr_ref, wo_ref,
         lg_ref, lb_ref, o_ref, cs_ref, carry) = refs
    else:
        (x_ref, sc_ref, sh_ref, g1_ref, a_ref, r_ref, cb_ref, cc_ref, cx_ref, cw_ref, wg_ref, wbr_ref, wo_ref,
         lg_ref, lb_ref, p0_ref, p1_ref, o_ref, cs_ref) = refs
    d = x_ref.shape[-1]
    x = x_ref[...]
    u = (x * (1.0 + sc_ref[...]) + sh_ref[...]).astype(BF16)
    s = cc_ref[...] * cx_ref[...]
    if seq_mode:
        @pl.when(pl.program_id(1) == 0)
        def _():
            carry[...] = jnp.zeros_like(carry)

        rid = lax.broadcasted_iota(jnp.int32, s.shape, 0)
        prev0, prev1 = carry[0:1, :], carry[1:2, :]
        s1 = jnp.where(rid == 0, prev1, pltpu.roll(s, 1, 0))
        s2 = jnp.where(rid == 0, prev0, jnp.where(rid == 1, prev1, pltpu.roll(s, 2, 0)))
        carry[...] = s[tm - 2:tm, :]
        cs_ref[...] = s[tm - 2:tm, :]
    else:
        s2, s1 = p0_ref[...], p1_ref[...]
        cs_ref[...] = s
    y = cw_ref[0:1, :] * s2 + cw_ref[1:2, :] * s1 + cw_ref[2:3, :] * s
    c_out = cb_ref[...] * y
    merged = None
    for n, branch in enumerate((a_ref[...], r_ref[...], c_out)):
        gt = jnp.dot(u, wg_ref[:, n * d:(n + 1) * d], preferred_element_type=F32)
        term = _sigmoid(gt) * jnp.dot(branch.astype(BF16), wbr_ref[n], preferred_element_type=F32)
        merged = term if merged is None else merged + term
    m = jnp.dot(merged.astype(BF16), wo_ref[...], preferred_element_type=F32)
    o_ref[...] = _ln(alpha * x + g1_ref[...] * m) * lg_ref[...] + lb_ref[...]


def _merge(x, sc, sh, g1, a_out, r_out, z, conv_w, w_gate, w_br, w_o, ln_g, ln_b, alpha, conv_prev=None):
    bsz, t, d = x.shape
    seq_mode = conv_prev is None
    tm = min(t, 512)
    tok512 = lambda cblk: pl.BlockSpec((None, tm, CONV_DIM), lambda b, i: (b, i, cblk))
    full2 = lambda a: pl.BlockSpec(a.shape, lambda b, i: (0, 0))
    in_specs = [pl.BlockSpec((None, tm, d), lambda b, i: (b, i, 0)),
                _mod_spec(sc, tm, 2), _mod_spec(sh, tm, 2), _mod_spec(g1, tm, 2),
                tok512(0), tok512(0), tok512(OFF_CB // CONV_DIM), tok512(OFF_CC // CONV_DIM),
                tok512(OFF_CX // CONV_DIM),
                full2(conv_w), full2(w_gate), pl.BlockSpec(w_br.shape, lambda b, i: (0, 0, 0)), full2(w_o),
                full2(ln_g), full2(ln_b)]
    args = [x, sc, sh, g1, a_out, r_out, z, z, z, conv_w, w_gate, w_br, w_o, ln_g, ln_b]
    scratch = []
    if seq_mode:
        cs_spec = pl.BlockSpec((None, CONV_W - 1, CONV_DIM), lambda b, i: (b, 0, 0))
        cs_shape = jax.ShapeDtypeStruct((bsz, CONV_W - 1, CONV_DIM), F32)
        scratch = [pltpu.VMEM((CONV_W - 1, CONV_DIM), F32)]
    else:
        in_specs += [tok512(0), tok512(0)]
        args += list(conv_prev)
        cs_spec = tok512(0)
        cs_shape = jax.ShapeDtypeStruct((bsz, t, CONV_DIM), F32)
    return pl.pallas_call(
        functools.partial(_merge_kernel, tm=tm, seq_mode=seq_mode, alpha=alpha),
        grid=(bsz, t // tm),
        in_specs=in_specs,
        out_specs=[pl.BlockSpec((None, tm, d), lambda b, i: (b, i, 0)), cs_spec],
        out_shape=[jax.ShapeDtypeStruct((bsz, t, d), F32), cs_shape],
        scratch_shapes=scratch,
        compiler_params=_params(("arbitrary", "arbitrary")),
    )(*args)


def _ffn_kernel(x_ref, sc_ref, sh_ref, g2_ref, wa_ref, wb_ref, wo_ref, lg_ref, lb_ref, o_ref, u_ref, acc_ref, *,
                alpha):
    f = pl.program_id(2)

    @pl.when(f == 0)
    def _():
        u_ref[...] = (x_ref[...] * (1.0 + sc_ref[...]) + sh_ref[...]).astype(BF16)
        acc_ref[...] = jnp.zeros_like(acc_ref)

    u = u_ref[...]
    a = jnp.dot(u, wa_ref[...], preferred_element_type=F32)
    b = jnp.dot(u, wb_ref[...], preferred_element_type=F32)
    acc_ref[...] += jnp.dot((_silu(a) * b).astype(BF16), wo_ref[...], preferred_element_type=F32)

    @pl.when(f == pl.num_programs(2) - 1)
    def _():
        o_ref[...] = _ln(alpha * x_ref[...] + g2_ref[...] * acc_ref[...]) * lg_ref[...] + lb_ref[...]


def _ffn(x, sc, sh, g2, w_in, w_out, ln_g, ln_b, alpha):
    bsz, t, d = x.shape
    tm = min(t, 512)
    tf = D_FF // 2
    nf = D_FF // tf
    full2 = lambda a: pl.BlockSpec(a.shape, lambda b, i, f: (0, 0))
    return pl.pallas_call(
        functools.partial(_ffn_kernel, alpha=alpha),
        grid=(bsz, t // tm, nf),
        in_specs=[pl.BlockSpec((None, tm, d), lambda b, i, f: (b, i, 0)),
                  _mod_spec(sc, tm, 3), _mod_spec(sh, tm, 3), _mod_spec(g2, tm, 3),
                  pl.BlockSpec((d, tf), lambda b, i, f: (0, f)),
                  pl.BlockSpec((d, tf), lambda b, i, f: (0, nf + f)),
                  pl.BlockSpec((tf, d), lambda b, i, f: (f, 0)),
                  full2(ln_g), full2(ln_b)],
        out_specs=pl.BlockSpec((None, tm, d), lambda b, i, f: (b, i, 0)),
        out_shape=jax.ShapeDtypeStruct((bsz, t, d), F32),
        scratch_shapes=[pltpu.VMEM((tm, d), BF16), pltpu.VMEM((tm, d), F32)],
        compiler_params=_params(("arbitrary", "arbitrary", "arbitrary")),
    )(x, sc, sh, g2, w_in, w_in, w_out, ln_g, ln_b)


def _col_to_row(col):
    n = col.shape[0]
    eye = lax.broadcasted_iota(jnp.int32, (n, n), 0) == lax.broadcasted_iota(jnp.int32, (n, n), 1)
    return jnp.sum(jnp.where(eye, jnp.broadcast_to(col, (n, n)), 0.0), axis=0, keepdims=True)


def _dec_scan_kernel(pt_ref, q_ref, ck_ref, o_ref, buf, sem, *, layer, n_pages, cp):
    b = pl.program_id(0)
    n_seq = pl.num_programs(0)
    n_chunks = n_pages // cp
    cb = cp // 2
    nbp = n_pages // 2

    def copies(w):
        slot = w % 2
        return [pltpu.make_async_copy(ck_ref.at[layer, pt_ref[w * cp + p]], buf.at[slot, p], sem.at[slot])
                for p in range(cp)]

    @pl.when(b == 0)
    def _():
        for cpy in copies(0):
            cpy.start()

    qcols = [_row_to_col(q_ref[h:h + 1, :]) for h in range(ATT_HEADS)]
    head = lax.broadcasted_iota(jnp.int32, (ATT_HEADS, LANES), 0)
    lane = lax.broadcasted_iota(jnp.int32, (ATT_HEADS, LANES), 1)

    def body(c, gates):
        w = b * n_chunks + c
        slot = w % 2

        @pl.when(w + 1 < n_seq * n_chunks)
        def _():
            for cpy in copies(w + 1):
                cpy.start()

        for cpy in copies(w):
            cpy.wait()
        for jb in range(cb):
            per_token = jnp.zeros((ATT_HEADS, LANES), F32)
            for h in range(ATT_HEADS):
                kt = buf[slot, 2 * jb, h] + buf[slot, 2 * jb + 1, h]
                per_token = jnp.where(head == h, jnp.sum(kt * qcols[h], axis=0, keepdims=True), per_token)
            g = jnp.sum(per_token, axis=1, keepdims=True) * (1.0 / MOBA_BLOCK)
            gates = jnp.where(lane == c * cb + jb, g, gates)
        return gates

    gates = lax.fori_loop(0, n_chunks, body, jnp.full((ATT_HEADS, LANES), NEG_INF, F32))
    res = jnp.zeros((ATT_HEADS, LANES), jnp.int32)
    for r in range(min(MOBA_TOPK, nbp)):
        mx = jnp.max(gates, axis=1, keepdims=True)
        ix = jnp.min(jnp.where(gates == mx, lane, LANES), axis=1, keepdims=True)
        gates = jnp.where(lane == ix, NEG_INF, gates)
        res = jnp.where(lane == r, ix, res)
    o_ref[...] = res


def _dec_scan(page_table, q_hd, cache_kt, layer):
    n, n_pages = page_table.shape
    page_size = cache_kt.shape[4]
    assert page_size * 2 == MOBA_BLOCK and page_size == LANES and n_pages // 2 <= LANES
    cp = min(16, n_pages)
    grid_spec = pltpu.PrefetchScalarGridSpec(
        num_scalar_prefetch=1,
        grid=(n,),
        in_specs=[pl.BlockSpec((None, ATT_HEADS, ATT_HD), lambda b, pt: (b, 0, 0)),
                  pl.BlockSpec(memory_space=pl.ANY)],
        out_specs=pl.BlockSpec((None, ATT_HEADS, LANES), lambda b, pt: (b, 0, 0)),
        scratch_shapes=[pltpu.VMEM((2, cp, ATT_HEADS, ATT_HD, page_size), F32),
                        pltpu.SemaphoreType.DMA((2,))],
    )
    return pl.pallas_call(
        functools.partial(_dec_scan_kernel, layer=layer, n_pages=n_pages, cp=cp),
        grid_spec=grid_spec,
        out_shape=jax.ShapeDtypeStruct((n, ATT_HEADS, LANES), jnp.int32),
        compiler_params=_params(("arbitrary",)),
    )(page_table.reshape(-1), q_hd, cache_kt)


def _dec_attend_kernel(pt_ref, pick_ref, q_ref, kn_ref, vn_ref, ck_ref, cv_ref, o_ref, kb, vb, sem, *,
                       layer, n_pages, n_sel):
    b = pl.program_id(0)
    n_tiles = 2 * n_sel

    def copies(seq):
        slot = seq % 2
        out = []
        for h in range(ATT_HEADS):
            for r in range(n_sel):
                blk = pick_ref[(seq * ATT_HEADS + h) * n_sel + r]
                for p in range(2):
                    page = pt_ref[seq * n_pages + 2 * blk + p]
                    out.append(pltpu.make_async_copy(ck_ref.at[layer, page, h], kb.at[slot, h, 2 * r + p],
                                                     sem.at[2 * slot]))
                    out.append(pltpu.make_async_copy(cv_ref.at[layer, page, h], vb.at[slot, h, 2 * r + p],
                                                     sem.at[2 * slot + 1]))
        return out

    @pl.when(b == 0)
    def _():
        for cpy in copies(0):
            cpy.start()

    @pl.when(b + 1 < pl.num_programs(0))
    def _():
        for cpy in copies(b + 1):
            cpy.start()

    for cpy in copies(b):
        cpy.wait()
    slot = b % 2
    for h in range(ATT_HEADS):
        qrow = q_ref[h:h + 1, :]
        qcol = _row_to_col(qrow)
        s_new = jnp.sum(kn_ref[h:h + 1, :] * qrow, axis=1, keepdims=True)
        vcol_new = _row_to_col(vn_ref[h:h + 1, :])
        s = [jnp.sum(kb[slot, h, t] * qcol, axis=0, keepdims=True) for t in range(n_tiles)]
        m = s_new
        for st in s:
            m = jnp.maximum(m, jnp.max(st, axis=1, keepdims=True))
        p_new = jnp.exp(s_new - m)
        den = p_new
        num = p_new * vcol_new
        for t, st in enumerate(s):
            p = jnp.exp(st - m)
            den = den + jnp.sum(p, axis=1, keepdims=True)
            num = num + jnp.sum(vb[slot, h, t] * p, axis=1, keepdims=True)
        o_ref[h:h + 1, :] = _col_to_row(num / den)


def _dec_attend(page_table, picks, q_hd, k_new, v_new, cache_kt, cache_vt, layer):
    n, n_pages = page_table.shape
    page_size = cache_kt.shape[4]
    n_sel = min(MOBA_TOPK, n_pages // 2)
    tok = pl.BlockSpec((None, ATT_HEADS, ATT_HD), lambda b, pt, pk: (b, 0, 0))
    grid_spec = pltpu.PrefetchScalarGridSpec(
        num_scalar_prefetch=2,
        grid=(n,),
        in_specs=[tok, tok, tok, pl.BlockSpec(memory_space=pl.ANY), pl.BlockSpec(memory_space=pl.ANY)],
        out_specs=tok,
        scratch_shapes=[pltpu.VMEM((2, ATT_HEADS, 2 * n_sel, ATT_HD, page_size), F32),
                        pltpu.VMEM((2, ATT_HEADS, 2 * n_sel, ATT_HD, page_size), F32),
                        pltpu.SemaphoreType.DMA((4,))],
    )
    return pl.pallas_call(
        functools.partial(_dec_attend_kernel, layer=layer, n_pages=n_pages, n_sel=n_sel),
        grid_spec=grid_spec,
        out_shape=jax.ShapeDtypeStruct((n, ATT_HEADS, ATT_HD), F32),
        compiler_params=_params(("arbitrary",)),
    )(page_table.reshape(-1), picks, q_hd, k_new, v_new, cache_kt, cache_vt)


def _att_rope_tables(pos):
    half = ROPE_DIM // 2
    inv = ROPE_THETA ** (-2.0 * jnp.arange(half, dtype=F32) / ROPE_DIM)
    ang = pos.astype(F32)[:, None] * inv[None, :]
    cos, sin = jnp.cos(ang), jnp.sin(ang)
    t = pos.shape[0]
    rest = ATT_HD - ROPE_DIM
    ca = jnp.concatenate([cos, cos, jnp.ones((t, rest), F32)], axis=1)
    cm = jnp.concatenate([-sin, jnp.zeros((t, half + rest), F32)], axis=1)
    cp = jnp.concatenate([jnp.zeros((t, half), F32), sin, jnp.zeros((t, rest), F32)], axis=1)
    rep = LANES // ATT_HD
    return tuple(jnp.tile(a, (1, rep)) for a in (ca, cm, cp))


def _ret_rope_tables(pos):
    half = RET_DK // 2
    inv = RET_THETA ** (-2.0 * jnp.arange(half, dtype=F32) / RET_DK)
    ang = pos.astype(F32)[:, None] * inv[None, :]
    cos, sin = jnp.cos(ang), jnp.sin(ang)
    c = jnp.concatenate([cos, cos], axis=1)
    s = jnp.concatenate([-sin, sin], axis=1)
    return jnp.tile(c, (1, RET_HEADS)), jnp.tile(s, (1, RET_HEADS))


def kernel(x_prompt, x_sample, cache_k, cache_v, state_ret, state_conv, page_table, c_prompt, c_sample, w_ada, b_ada,
           w_in, conv_w, w_br, w_o, ln1_g, ln1_b, w_ffn_in, w_ffn_out, ln2_g, ln2_b):
    depth = w_in.shape[0]
    alpha = (2 * depth) ** 0.25
    bsz, seq, d = x_prompt.shape
    nd = x_sample.shape[0]
    assert x_sample.shape[1] == 1, "one new token per decode sequence"
    n_pool, page_size = cache_k.shape[1], cache_k.shape[2]
    past_len = page_table.shape[1] * page_size

    ada = _ada(jnp.concatenate([c_prompt, c_sample], axis=0), w_ada, b_ada)
    w_in_b = w_in[:, :, :N_MIX].astype(BF16)
    w_gate_b = w_in[:, :, N_MIX:].astype(BF16)
    w_br_b, w_o_b = w_br.astype(BF16), w_o.astype(BF16)
    w_fi_b, w_fo_b = w_ffn_in.astype(BF16), w_ffn_out.astype(BF16)
    cache_kt = jnp.transpose(cache_k, (0, 1, 3, 4, 2))
    cache_vt = jnp.transpose(cache_v, (0, 1, 3, 4, 2))

    pos_p = jnp.arange(seq, dtype=jnp.int32)
    pos_s = jnp.full((nd,), past_len, jnp.int32)
    att_tab_p, ret_tab_p = _att_rope_tables(pos_p), _ret_rope_tables(pos_p)
    att_tab_s = _att_rope_tables(pos_s)
    ret_tab_s = tuple(a.reshape(nd, 1, RET_QK_DIM) for a in _ret_rope_tables(pos_s))
    ret_consts = _ret_consts()

    xp = x_prompt
    xs = x_sample.reshape(1, nd, d)
    outs = {k: [] for k in ("kp", "vp", "ks", "vs", "rp", "rs", "cp", "cs")}
    for l in range(depth):
        mods_p = [m.reshape(bsz, 1, d) for m in jnp.split(ada[l, :bsz], 6, axis=-1)]
        mods_s = [m.reshape(1, nd, d) for m in jnp.split(ada[l, bsz:], 6, axis=-1)]
        lg1, lb1 = ln1_g[l].reshape(1, d), ln1_b[l].reshape(1, d)
        lg2, lb2 = ln2_g[l].reshape(1, d), ln2_b[l].reshape(1, d)

        sh1, sc1, g1, sh2, sc2, g2 = mods_p
        z = _inproj(xp, sc1, sh1, w_in_b[l])
        q_t, k_hm, v_t, k_tok, v_tok = _attn_prep(z, att_tab_p, head_major=True)
        kmean = _kmean(k_tok)
        kmean_hm = kmean.reshape(bsz, -1, ATT_HEADS, ATT_HD).transpose(0, 2, 1, 3)
        a_out = _moba(q_t, k_hm, v_t, kmean_hm)
        r_out, r_state = _retention(z, ret_tab_p, ret_consts)
        xp, c_state = _merge(xp, sc1, sh1, g1, a_out, r_out, z, conv_w[l], w_gate_b[l], w_br_b[l], w_o_b[l], lg1, lb1,
                             alpha)
        xp = _ffn(xp, sc2, sh2, g2, w_fi_b[l], w_fo_b[l], lg2, lb2, alpha)
        outs["kp"].append(k_tok.reshape(bsz, seq, ATT_HEADS, ATT_HD))
        outs["vp"].append(v_tok.reshape(bsz, seq, ATT_HEADS, ATT_HD))
        outs["rp"].append(r_state)
        outs["cp"].append(c_state)

        sh1, sc1, g1, sh2, sc2, g2 = mods_s
        z = _inproj(xs, sc1, sh1, w_in_b[l])
        q_tok, k_new, v_new = _attn_prep(z, att_tab_s, head_major=False)
        q_tok, k_new, v_new = (a.reshape(nd, ATT_HEADS, ATT_HD) for a in (q_tok, k_new, v_new))
        picks = _dec_scan(page_table, q_tok, cache_kt, l)
        n_sel = min(MOBA_TOPK, page_table.shape[1] // 2)
        a_out = _dec_attend(page_table, picks[:, :, :n_sel].reshape(-1), q_tok, k_new, v_new, cache_kt, cache_vt, l)
        r_out, r_state = _ret_step(z.reshape(nd, 1, N_MIX), ret_tab_s, state_ret[l])
        prev = (state_conv[l][:, 0, :].reshape(1, nd, CONV_DIM), state_conv[l][:, 1, :].reshape(1, nd, CONV_DIM))
        xs, s_new = _merge(xs, sc1, sh1, g1, a_out.reshape(1, nd, ATT_DIM), r_out.reshape(1, nd, RET_V_DIM), z,
                           conv_w[l], w_gate_b[l], w_br_b[l], w_o_b[l], lg1, lb1, alpha, conv_prev=prev)
        xs = _ffn(xs, sc2, sh2, g2, w_fi_b[l], w_fo_b[l], lg2, lb2, alpha)
        outs["ks"].append(k_new.reshape(nd, 1, ATT_HEADS, ATT_HD))
        outs["vs"].append(v_new.reshape(nd, 1, ATT_HEADS, ATT_HD))
        outs["rs"].append(r_state)
        outs["cs"].append(jnp.stack([state_conv[l][:, 1, :], s_new.reshape(nd, CONV_DIM)], axis=1))

    st = {k: jnp.stack(v) for k, v in outs.items()}
    return (xp, xs.reshape(nd, 1, d), st["kp"], st["vp"], st["ks"], st["vs"], st["rp"], st["rs"], st["cp"], st["cs"])
```

```python
import functools

import jax
import jax.numpy as jnp
from jax import lax
from jax.experimental import pallas as pl
from jax.experimental.pallas import tpu as pltpu

F32 = jnp.float32
BF16 = jnp.bfloat16

D_MODEL = 1024
ATT_HEADS = 8
ATT_HD = 64
ATT_DIM = ATT_HEADS * ATT_HD
ROPE_DIM = ATT_HD // 4
ROPE_THETA = 500000.0
MOBA_BLOCK = 256
MOBA_TOPK = 3
RET_HEADS = 4
RET_DK = 64
RET_DV = 128
RET_QK_DIM = RET_HEADS * RET_DK
RET_V_DIM = RET_HEADS * RET_DV
RET_THETA = 10000.0
RET_CHUNK = 128
CONV_DIM = 512
CONV_W = 3
N_BRANCH = 3
D_FF = 2816
LN_EPS = 1e-5
N_MIX = 3 * ATT_DIM + 2 * RET_QK_DIM + 2 * RET_V_DIM + 3 * CONV_DIM

OFF_AQ = 0
OFF_AK = OFF_AQ + ATT_DIM
OFF_AV = OFF_AK + ATT_DIM
OFF_RQ = OFF_AV + ATT_DIM
OFF_RK = OFF_RQ + RET_QK_DIM
OFF_RV = OFF_RK + RET_QK_DIM
OFF_RG = OFF_RV + RET_V_DIM
OFF_CB = OFF_RG + RET_V_DIM
OFF_CC = OFF_CB + CONV_DIM
OFF_CX = OFF_CC + CONV_DIM

LANES = 128
BF16_SUBLANES = 16
VT_ROWS = ATT_HD + BF16_SUBLANES
LOG2E = 1.4426950408889634
HPS = 4
VMEM_LIMIT = 48 * 1024 * 1024
NEG_INF = float("-inf")


def _params(semantics, vmem=VMEM_LIMIT):
    return pltpu.CompilerParams(dimension_semantics=semantics, vmem_limit_bytes=vmem)


def _ln(y):
    mu = jnp.mean(y, axis=-1, keepdims=True)
    yc = y - mu
    return yc * lax.rsqrt(jnp.mean(yc * yc, axis=-1, keepdims=True) + LN_EPS)


def _sigmoid(x):
    return 1.0 / (1.0 + jnp.exp(-x))


def _silu(x):
    return x * _sigmoid(x)


def _mod_spec(mod, tm, ngrid):
    per_row = mod.shape[1] != 1
    rows = tm if per_row else 1
    if ngrid == 2:
        return pl.BlockSpec((None, rows, mod.shape[2]), lambda b, i: (b, i if per_row else 0, 0))
    return pl.BlockSpec((None, rows, mod.shape[2]), lambda b, i, j: (b, i if per_row else 0, 0))


def _ada_kernel(c_ref, w_ref, b_ref, o_ref):
    cond = _silu(c_ref[...]).astype(BF16)
    o_ref[...] = jnp.dot(cond, w_ref[...].astype(BF16), preferred_element_type=F32) + b_ref[...]


def _ada(c, w_ada, b_ada):
    depth, d, n = w_ada.shape
    rows = c.shape[0]
    tn = 1536
    return pl.pallas_call(
        _ada_kernel,
        grid=(depth, n // tn),
        in_specs=[pl.BlockSpec((rows, d), lambda l, j: (0, 0)),
                  pl.BlockSpec((None, d, tn), lambda l, j: (l, 0, j)),
                  pl.BlockSpec((None, 1, tn), lambda l, j: (l, 0, j))],
        out_specs=pl.BlockSpec((None, rows, tn), lambda l, j: (l, 0, j)),
        out_shape=jax.ShapeDtypeStruct((depth, rows, n), F32),
        compiler_params=_params(("arbitrary", "arbitrary")),
    )(c, w_ada, b_ada.reshape(depth, 1, n))


def _inproj_kernel(x_ref, sc_ref, sh_ref, w_ref, o_ref, u_ref):
    @pl.when(pl.program_id(2) == 0)
    def _():
        u_ref[...] = (x_ref[...] * (1.0 + sc_ref[...]) + sh_ref[...]).astype(BF16)

    o_ref[...] = jnp.dot(u_ref[...], w_ref[...], preferred_element_type=F32)


def _inproj(x, sc, sh, w, layer):
    bsz, t, d = x.shape
    n = w.shape[2]
    tm = min(t, 1024)
    tn = 1536
    return pl.pallas_call(
        _inproj_kernel,
        grid=(bsz, t // tm, n // tn),
        in_specs=[pl.BlockSpec((None, tm, d), lambda b, i, j: (b, i, 0)),
                  _mod_spec(sc, tm, 3), _mod_spec(sh, tm, 3),
                  pl.BlockSpec((None, d, tn), lambda b, i, j: (layer, 0, j))],
        out_specs=pl.BlockSpec((None, tm, tn), lambda b, i, j: (b, i, j)),
        out_shape=jax.ShapeDtypeStruct((bsz, t, n), F32),
        scratch_shapes=[pltpu.VMEM((tm, d), BF16)],
        compiler_params=_params(("arbitrary", "arbitrary", "arbitrary")),
    )(x, sc, sh, w)


def _rope_att(x, ca, cm, cp):
    return x * ca + pltpu.roll(x, LANES - ROPE_DIM // 2, 1) * cm + pltpu.roll(x, ROPE_DIM // 2, 1) * cp


def _attn_prep_kernel(q_ref, k_ref, v_ref, ca_ref, cm_ref, cp_ref, *out_refs, head_major, tm):
    ca, cm, cp = ca_ref[...], cm_ref[...], cp_ref[...]
    scale = ATT_HD ** -0.5
    if head_major:
        qt_ref, kh_ref, vt_ref, ko_ref, vo_ref = out_refs
    else:
        qo_ref, ko_ref, vo_ref = out_refs
    vo_ref[...] = v_ref[...]
    for c in range(ATT_DIM // LANES):
        sl = slice(c * LANES, (c + 1) * LANES)
        q = _rope_att(q_ref[:, sl], ca, cm, cp) * scale
        k = _rope_att(k_ref[:, sl], ca, cm, cp)
        ko_ref[:, sl] = k
        if head_major:
            qt = q.T
            vt = v_ref[:, sl].T.astype(BF16)
            for hh in range(2):
                hs = slice(hh * ATT_HD, (hh + 1) * ATT_HD)
                kh_ref[2 * c + hh] = k[:, hs].astype(BF16)
                for j in range(tm // MOBA_BLOCK):
                    cols = slice(j * MOBA_BLOCK, (j + 1) * MOBA_BLOCK)
                    qt_ref[2 * c + hh, j] = qt[hs, cols]
                    vt_ref[2 * c + hh, j, 0:ATT_HD, :] = vt[hs, cols]
                    vt_ref[2 * c + hh, j, ATT_HD:VT_ROWS, :] = jnp.ones((VT_ROWS - ATT_HD, MOBA_BLOCK), BF16)
        else:
            qo_ref[:, sl] = q


def _attn_prep(z, tabs, head_major):
    bsz, t, _ = z.shape
    tm = min(t, 512)
    cq, ck, cv = OFF_AQ // ATT_DIM, OFF_AK // ATT_DIM, OFF_AV // ATT_DIM
    tok = pl.BlockSpec((None, tm, ATT_DIM), lambda b, i: (b, i, 0))
    tab = pl.BlockSpec((tm, LANES), lambda b, i: (i, 0))
    tok_shape = jax.ShapeDtypeStruct((bsz, t, ATT_DIM), F32)
    if head_major:
        nblk = tm // MOBA_BLOCK
        hm = pl.BlockSpec((None, ATT_HEADS, tm, ATT_HD), lambda b, i: (b, 0, i, 0))
        tr = pl.BlockSpec((None, ATT_HEADS, nblk, ATT_HD, MOBA_BLOCK), lambda b, i: (b, 0, i, 0, 0))
        trv = pl.BlockSpec((None, ATT_HEADS, nblk, VT_ROWS, MOBA_BLOCK), lambda b, i: (b, 0, i, 0, 0))
        nb = t // MOBA_BLOCK
        out_specs = [tr, hm, trv, tok, tok]
        out_shape = [jax.ShapeDtypeStruct((bsz, ATT_HEADS, nb, ATT_HD, MOBA_BLOCK), F32),
                     jax.ShapeDtypeStruct((bsz, ATT_HEADS, t, ATT_HD), BF16),
                     jax.ShapeDtypeStruct((bsz, ATT_HEADS, nb, VT_ROWS, MOBA_BLOCK), BF16), tok_shape, tok_shape]
    else:
        out_specs = [tok, tok, tok]
        out_shape = [tok_shape, tok_shape, tok_shape]
    return pl.pallas_call(
        functools.partial(_attn_prep_kernel, head_major=head_major, tm=tm),
        grid=(bsz, t // tm),
        in_specs=[pl.BlockSpec((None, tm, ATT_DIM), lambda b, i: (b, i, cq)),
                  pl.BlockSpec((None, tm, ATT_DIM), lambda b, i: (b, i, ck)),
                  pl.BlockSpec((None, tm, ATT_DIM), lambda b, i: (b, i, cv)),
                  tab, tab, tab],
        out_specs=out_specs,
        out_shape=out_shape,
        compiler_params=_params(("arbitrary", "arbitrary")),
    )(z, z, z, *tabs)


def _kmean_kernel(k_ref, o_ref, *, nblk):
    for j in range(nblk):
        o_ref[j:j + 1, :] = jnp.mean(k_ref[j * MOBA_BLOCK:(j + 1) * MOBA_BLOCK, :], axis=0, keepdims=True)


def _kmean(k_tok):
    bsz, t, _ = k_tok.shape
    rows = min(t, 8 * MOBA_BLOCK)
    nblk = rows // MOBA_BLOCK
    return pl.pallas_call(
        functools.partial(_kmean_kernel, nblk=nblk),
        grid=(bsz, t // rows),
        in_specs=[pl.BlockSpec((None, rows, ATT_DIM), lambda b, i: (b, i, 0))],
        out_specs=pl.BlockSpec((None, nblk, ATT_DIM), lambda b, i: (b, i, 0)),
        out_shape=jax.ShapeDtypeStruct((bsz, t // MOBA_BLOCK, ATT_DIM), F32),
        compiler_params=_params(("arbitrary", "arbitrary")),
    )(k_tok)


def _moba_kernel(qt_ref, k_ref, vt_ref, km_ref, o_ref, s_scr, p_scr, *, nb):
    i = pl.program_id(2)
    blk = MOBA_BLOCK
    key = lax.broadcasted_iota(jnp.int32, (blk, blk), 0)
    qry = lax.broadcasted_iota(jnp.int32, (blk, blk), 1)
    bid = lax.broadcasted_iota(jnp.int32, (nb, blk), 0)
    qtbs, pickss, init = [], [], []

    def scores(j):
        off = pl.multiple_of(j * blk, blk)
        return [jnp.dot(k_ref[hh, pl.ds(off, blk), :], qtbs[hh], preferred_element_type=F32) for hh in range(HPS)]

    qts = [qt_ref[hh] for hh in range(HPS)]
    qtbs.extend((qt * LOG2E).astype(BF16) for qt in qts)

    for hh, s in enumerate(scores(0)):
        s_scr[0, hh] = s
        p_scr[hh] = jnp.zeros((blk, blk), BF16)
    own = scores(i)

    for hh in range(HPS):
        gate = jnp.dot(km_ref[hh], qts[hh], precision=lax.Precision.HIGHEST, preferred_element_type=F32)
        gate = jnp.where(bid < i, gate, NEG_INF)
        picks = []
        for r in range(MOBA_TOPK):
            mx = jnp.max(gate, axis=0, keepdims=True)
            ix = jnp.min(jnp.where(gate == mx, bid, nb), axis=0, keepdims=True)
            gate = jnp.where(bid == ix, NEG_INF, gate)
            picks.append(jnp.where(r < i, ix, -1))
        pickss.append(picks)

    for hh, s in enumerate(own):
        s = jnp.where(key <= qry, s, NEG_INF)
        m0 = jnp.max(s, axis=0, keepdims=True)
        p = jnp.exp2(s - m0)
        acc0 = jnp.dot(vt_ref[hh, i], p.astype(BF16), preferred_element_type=F32)
        init.append((m0, acc0, jnp.ones_like(m0)))

    def finish(jp, hh, acc, alpha):
        return alpha * acc + jnp.dot(vt_ref[hh, jp], p_scr[hh], preferred_element_type=F32)

    def step(j, carry, cur):
        jp = jnp.maximum(j - 1, 0)
        lagged = [finish(jp, hh, carry[hh][1], carry[hh][2]) for hh in range(HPS)]
        nxt = scores(jnp.minimum(j + 1, nb - 1))
        new = []
        for hh in range(HPS):
            m = carry[hh][0]
            picks = pickss[hh]
            live = (picks[0] == j) | (picks[1] == j) | (picks[2] == j)
            sj = s_scr[cur, hh]
            m_new = jnp.maximum(m, jnp.where(live, jnp.max(sj, axis=0, keepdims=True), NEG_INF))
            alpha = jnp.exp2(m - m_new)
            pj = jnp.exp2(sj - jnp.where(live, m_new, jnp.inf))
            p_scr[hh] = pj.astype(BF16)
            s_scr[1 - cur, hh] = nxt[hh]
            new.append((m_new, lagged[hh], alpha))
        return tuple(new)

    def pair(t, carry):
        return step(2 * t + 1, step(2 * t, carry, 0), 1)

    trips = (i + 1) // 2
    fin = lax.fori_loop(0, trips, pair, tuple(init))
    j_last = jnp.maximum(2 * trips - 1, 0)
    outs = []
    for hh, (_, acc, alpha) in enumerate(fin):
        acc = finish(j_last, hh, acc, alpha)
        outs.append(acc[0:ATT_HD] / acc[ATT_HD:ATT_HD + 1])
    o_ref[...] = jnp.concatenate(outs, axis=0).T


def _moba(qt, k_hm, vt, kmean_hm):
    bsz, nh, nb, hd, blk = qt.shape
    t = nb * blk
    return pl.pallas_call(
        functools.partial(_moba_kernel, nb=nb),
        grid=(bsz, nh // HPS, nb),
        in_specs=[pl.BlockSpec((None, HPS, None, hd, blk), lambda b, h, i: (b, h, i, 0, 0)),
                  pl.BlockSpec((None, HPS, t, hd), lambda b, h, i: (b, h, 0, 0)),
                  pl.BlockSpec((None, HPS, nb, vt.shape[3], blk), lambda b, h, i: (b, h, 0, 0, 0)),
                  pl.BlockSpec((None, HPS, nb, hd), lambda b, h, i: (b, h, 0, 0))],
        out_specs=pl.BlockSpec((None, blk, HPS * hd), lambda b, h, i: (b, i, h)),
        out_shape=jax.ShapeDtypeStruct((bsz, t, nh * hd), F32),
        scratch_shapes=[pltpu.VMEM((2, HPS, blk, blk), F32), pltpu.VMEM((HPS, blk, blk), BF16)],
        compiler_params=_params(("arbitrary", "arbitrary", "arbitrary")),
    )(qt, k_hm, vt, kmean_hm)


def _rope_ret(x, c, s):
    w = x.shape[1]
    d = lax.broadcasted_iota(jnp.int32, x.shape, 1) % RET_DK
    half = RET_DK // 2
    partner = jnp.where(d < half, pltpu.roll(x, w - half, 1), pltpu.roll(x, half, 1))
    return x * c + partner * s


def _ret_decay(h):
    return 1.0 - 2.0 ** (-5.0 - h)


def _retention_kernel(q_ref, k_ref, v_ref, g_ref, c_ref, s_ref, dm_ref, qd_ref, kd_ref, o_ref, st_ref, state, *, tm):
    ch = RET_CHUNK

    @pl.when(pl.program_id(1) == 0)
    def _():
        state[...] = jnp.zeros_like(state)

    q = _rope_ret(q_ref[...], c_ref[...], s_ref[...])
    k = _rope_ret(k_ref[...], c_ref[...], s_ref[...]) * (RET_DK ** -0.5)
    kt = k.T
    for c in range(tm // ch):
        rows = slice(c * ch, (c + 1) * ch)
        for h in range(RET_HEADS):
            qc = q[rows, h * RET_DK:(h + 1) * RET_DK]
            ktc = kt[h * RET_DK:(h + 1) * RET_DK, rows]
            vc = v_ref[rows, h * RET_DV:(h + 1) * RET_DV].astype(BF16)
            s_old = state[h]
            inner = jnp.dot(qc.astype(BF16), ktc.astype(BF16), preferred_element_type=F32) * dm_ref[h]
            o = jnp.dot(inner.astype(BF16), vc, preferred_element_type=F32)
            o = o + jnp.dot((qc * qd_ref[h]).astype(BF16), s_old.astype(BF16), preferred_element_type=F32)
            c_dec = _ret_decay(h) ** ch
            state[h] = c_dec * s_old + jnp.dot((ktc * kd_ref[h]).astype(BF16), vc, preferred_element_type=F32)
            gate = g_ref[rows, h * RET_DV:(h + 1) * RET_DV]
            o_ref[rows, h * RET_DV:(h + 1) * RET_DV] = _ln(o) * _silu(gate)
    st_ref[...] = state[...]


def _retention(z, tabs, consts):
    bsz, t, _ = z.shape
    tm = min(t, 512)
    dmat, qdec, kdec = consts
    const3 = lambda a: pl.BlockSpec(a.shape, lambda b, i: (0, 0, 0))
    tab = pl.BlockSpec((tm, RET_QK_DIM), lambda b, i: (i, 0))
    return pl.pallas_call(
        functools.partial(_retention_kernel, tm=tm),
        grid=(bsz, t // tm),
        in_specs=[pl.BlockSpec((None, tm, RET_QK_DIM), lambda b, i: (b, i, OFF_RQ // RET_QK_DIM)),
                  pl.BlockSpec((None, tm, RET_QK_DIM), lambda b, i: (b, i, OFF_RK // RET_QK_DIM)),
                  pl.BlockSpec((None, tm, RET_V_DIM), lambda b, i: (b, i, OFF_RV // RET_V_DIM)),
                  pl.BlockSpec((None, tm, RET_V_DIM), lambda b, i: (b, i, OFF_RG // RET_V_DIM)),
                  tab, tab, const3(dmat), const3(qdec), const3(kdec)],
        out_specs=[pl.BlockSpec((None, tm, RET_V_DIM), lambda b, i: (b, i, 0)),
                   pl.BlockSpec((None, RET_HEADS, RET_DK, RET_DV), lambda b, i: (b, 0, 0, 0))],
        out_shape=[jax.ShapeDtypeStruct((bsz, t, RET_V_DIM), F32),
                   jax.ShapeDtypeStruct((bsz, RET_HEADS, RET_DK, RET_DV), F32)],
        scratch_shapes=[pltpu.VMEM((RET_HEADS, RET_DK, RET_DV), F32)],
        compiler_params=_params(("arbitrary", "arbitrary")),
    )(z, z, z, z, *tabs, dmat, qdec, kdec)


def _ret_consts():
    ch = RET_CHUNK
    log_g = jnp.log(1.0 - 2.0 ** (-5.0 - jnp.arange(RET_HEADS, dtype=F32)))
    i = jnp.arange(ch, dtype=F32)
    diff = i[:, None] - i[None, :]
    dmat = jnp.where(diff >= 0, jnp.exp(jnp.maximum(diff, 0.0)[None] * log_g[:, None, None]), 0.0)
    qdec = jnp.exp((i + 1.0)[None, :] * log_g[:, None])[:, :, None]
    kdec = jnp.exp((ch - 1.0 - i)[None, :] * log_g[:, None])[:, None, :]
    return dmat, qdec, kdec


def _row_to_col(row):
    n = row.shape[1]
    eye = lax.broadcasted_iota(jnp.int32, (n, n), 0) == lax.broadcasted_iota(jnp.int32, (n, n), 1)
    return jnp.sum(jnp.where(eye, jnp.broadcast_to(row, (n, n)), 0.0), axis=1, keepdims=True)


def _ret_step_kernel(q_ref, k_ref, v_ref, g_ref, c_ref, s_ref, st_ref, o_ref, sn_ref):
    q = _rope_ret(q_ref[...], c_ref[...], s_ref[...])
    k = _rope_ret(k_ref[...], c_ref[...], s_ref[...]) * (RET_DK ** -0.5)
    for h in range(RET_HEADS):
        g = _ret_decay(h)
        qrow = q[:, h * RET_DK:(h + 1) * RET_DK]
        krow = k[:, h * RET_DK:(h + 1) * RET_DK]
        qcol = _row_to_col(qrow)
        kcol = _row_to_col(krow)
        v = v_ref[:, h * RET_DV:(h + 1) * RET_DV]
        s_old = st_ref[h]
        inner = jnp.sum(qrow * krow, axis=1, keepdims=True)
        o = inner * v + jnp.sum((qcol * g) * s_old, axis=0, keepdims=True)
        sn_ref[h] = g * s_old + kcol * v
        o_ref[:, h * RET_DV:(h + 1) * RET_DV] = _ln(o) * _silu(g_ref[:, h * RET_DV:(h + 1) * RET_DV])


def _ret_step(z, tabs, state):
    n = z.shape[0]
    tab = pl.BlockSpec((None, 1, RET_QK_DIM), lambda b: (b, 0, 0))
    st = pl.BlockSpec((None, RET_HEADS, RET_DK, RET_DV), lambda b: (b, 0, 0, 0))
    return pl.pallas_call(
        _ret_step_kernel,
        grid=(n,),
        in_specs=[pl.BlockSpec((None, 1, RET_QK_DIM), lambda b: (b, 0, OFF_RQ // RET_QK_DIM)),
                  pl.BlockSpec((None, 1, RET_QK_DIM), lambda b: (b, 0, OFF_RK // RET_QK_DIM)),
                  pl.BlockSpec((None, 1, RET_V_DIM), lambda b: (b, 0, OFF_RV // RET_V_DIM)),
                  pl.BlockSpec((None, 1, RET_V_DIM), lambda b: (b, 0, OFF_RG // RET_V_DIM)),
                  tab, tab, st],
        out_specs=[pl.BlockSpec((None, 1, RET_V_DIM), lambda b: (b, 0, 0)), st],
        out_shape=[jax.ShapeDtypeStruct((n, 1, RET_V_DIM), F32),
                   jax.ShapeDtypeStruct(state.shape, F32)],
        compiler_params=_params(("arbitrary",)),
    )(z, z, z, z, *tabs, state)


def _merge_kernel(*refs, tm, seq_mode, alpha):
    if seq_mode:
        (x_ref, sc_ref, sh_ref, g1_ref, a_ref, r_ref, cb_ref, cc_ref, cx_ref, cw_ref, wg_ref, wbr_ref, wo_ref,
         lg_ref, lb_ref, o_ref, cs_ref, carry) = refs
    else:
        (x_ref, sc_ref, sh_ref, g1_ref, a_ref, r_ref, cb_ref, cc_ref, cx_ref, cw_ref, wg_ref, wbr_ref, wo_ref,
         lg_ref, lb_ref, p0_ref, p1_ref, o_ref, cs_ref) = refs
    d = x_ref.shape[-1]
    x = x_ref[...]
    u = (x * (1.0 + sc_ref[...]) + sh_ref[...]).astype(BF16)
    s = cc_ref[...] * cx_ref[...]
    if seq_mode:
        @pl.when(pl.program_id(1) == 0)
        def _():
            carry[...] = jnp.zeros_like(carry)

        rid = lax.broadcasted_iota(jnp.int32, s.shape, 0)
        prev0, prev1 = carry[0:1, :], carry[1:2, :]
        s1 = jnp.where(rid == 0, prev1, pltpu.roll(s, 1, 0))
        s2 = jnp.where(rid == 0, prev0, jnp.where(rid == 1, prev1, pltpu.roll(s, 2, 0)))
        carry[...] = s[tm - 2:tm, :]
        cs_ref[...] = s[tm - 2:tm, :]
    else:
        s2, s1 = p0_ref[...], p1_ref[...]
        cs_ref[...] = s
    y = cw_ref[0:1, :] * s2 + cw_ref[1:2, :] * s1 + cw_ref[2:3, :] * s
    c_out = cb_ref[...] * y
    merged = None
    for n, branch in enumerate((a_ref[...], r_ref[...], c_out)):
        gt = jnp.dot(u, wg_ref[:, n * d:(n + 1) * d], preferred_element_type=F32)
        term = _sigmoid(gt) * jnp.dot(branch.astype(BF16), wbr_ref[n], preferred_element_type=F32)
        merged = term if merged is None else merged + term
    m = jnp.dot(merged.astype(BF16), wo_ref[...], preferred_element_type=F32)
    o_ref[...] = _ln(alpha * x + g1_ref[...] * m) * lg_ref[...] + lb_ref[...]


def _merge(x, sc, sh, g1, a_out, r_out, z, conv_w, w_gate, w_br, w_o, ln_g, ln_b, alpha, layer, conv_prev=None):
    bsz, t, d = x.shape
    seq_mode = conv_prev is None
    tm = min(t, 512)
    tok512 = lambda cblk: pl.BlockSpec((None, tm, CONV_DIM), lambda b, i: (b, i, cblk))
    full2 = lambda a: pl.BlockSpec(a.shape, lambda b, i: (0, 0))
    in_specs = [pl.BlockSpec((None, tm, d), lambda b, i: (b, i, 0)),
                _mod_spec(sc, tm, 2), _mod_spec(sh, tm, 2), _mod_spec(g1, tm, 2),
                tok512(0), tok512(0), tok512(OFF_CB // CONV_DIM), tok512(OFF_CC // CONV_DIM),
                tok512(OFF_CX // CONV_DIM),
                full2(conv_w), pl.BlockSpec((None,) + w_gate.shape[1:], lambda b, i: (layer, 0, 0)),
                pl.BlockSpec((None,) + w_br.shape[1:], lambda b, i: (layer, 0, 0, 0)),
                pl.BlockSpec((None,) + w_o.shape[1:], lambda b, i: (layer, 0, 0)),
                full2(ln_g), full2(ln_b)]
    args = [x, sc, sh, g1, a_out, r_out, z, z, z, conv_w, w_gate, w_br, w_o, ln_g, ln_b]
    scratch = []
    if seq_mode:
        cs_spec = pl.BlockSpec((None, CONV_W - 1, CONV_DIM), lambda b, i: (b, 0, 0))
        cs_shape = jax.ShapeDtypeStruct((bsz, CONV_W - 1, CONV_DIM), F32)
        scratch = [pltpu.VMEM((CONV_W - 1, CONV_DIM), F32)]
    else:
        in_specs += [tok512(0), tok512(0)]
        args += list(conv_prev)
        cs_spec = tok512(0)
        cs_shape = jax.ShapeDtypeStruct((bsz, t, CONV_DIM), F32)
    return pl.pallas_call(
        functools.partial(_merge_kernel, tm=tm, seq_mode=seq_mode, alpha=alpha),
        grid=(bsz, t // tm),
        in_specs=in_specs,
        out_specs=[pl.BlockSpec((None, tm, d), lambda b, i: (b, i, 0)), cs_spec],
        out_shape=[jax.ShapeDtypeStruct((bsz, t, d), F32), cs_shape],
        scratch_shapes=scratch,
        compiler_params=_params(("arbitrary", "arbitrary")),
    )(*args)


def _ffn_kernel(x_ref, sc_ref, sh_ref, g2_ref, wa_ref, wb_ref, wo_ref, lg_ref, lb_ref, o_ref, u_ref, acc_ref, *,
                alpha):
    f = pl.program_id(2)

    @pl.when(f == 0)
    def _():
        u_ref[...] = (x_ref[...] * (1.0 + sc_ref[...]) + sh_ref[...]).astype(BF16)
        acc_ref[...] = jnp.zeros_like(acc_ref)

    u = u_ref[...]
    a = jnp.dot(u, wa_ref[...], preferred_element_type=F32)
    b = jnp.dot(u, wb_ref[...], preferred_element_type=F32)
    acc_ref[...] += jnp.dot((_silu(a) * b).astype(BF16), wo_ref[...], preferred_element_type=F32)

    @pl.when(f == pl.num_programs(2) - 1)
    def _():
        o_ref[...] = _ln(alpha * x_ref[...] + g2_ref[...] * acc_ref[...]) * lg_ref[...] + lb_ref[...]


def _ffn(x, sc, sh, g2, w_in, w_out, ln_g, ln_b, alpha, layer):
    bsz, t, d = x.shape
    tm = min(t, 512)
    tf = D_FF // 2
    nf = D_FF // tf
    full2 = lambda a: pl.BlockSpec(a.shape, lambda b, i, f: (0, 0))
    return pl.pallas_call(
        functools.partial(_ffn_kernel, alpha=alpha),
        grid=(bsz, t // tm, nf),
        in_specs=[pl.BlockSpec((None, tm, d), lambda b, i, f: (b, i, 0)),
                  _mod_spec(sc, tm, 3), _mod_spec(sh, tm, 3), _mod_spec(g2, tm, 3),
                  pl.BlockSpec((None, d, tf), lambda b, i, f: (layer, 0, f)),
                  pl.BlockSpec((None, d, tf), lambda b, i, f: (layer, 0, nf + f)),
                  pl.BlockSpec((None, tf, d), lambda b, i, f: (layer, f, 0)),
                  full2(ln_g), full2(ln_b)],
        out_specs=pl.BlockSpec((None, tm, d), lambda b, i, f: (b, i, 0)),
        out_shape=jax.ShapeDtypeStruct((bsz, t, d), F32),
        scratch_shapes=[pltpu.VMEM((tm, d), BF16), pltpu.VMEM((tm, d), F32)],
        compiler_params=_params(("arbitrary", "arbitrary", "arbitrary")),
    )(x, sc, sh, g2, w_in, w_in, w_out, ln_g, ln_b)


def _col_to_row(col):
    n = col.shape[0]
    eye = lax.broadcasted_iota(jnp.int32, (n, n), 0) == lax.broadcasted_iota(jnp.int32, (n, n), 1)
    return jnp.sum(jnp.where(eye, jnp.broadcast_to(col, (n, n)), 0.0), axis=0, keepdims=True)


def _dec_scan_kernel(pt_ref, q_ref, ck_ref, o_ref, buf, sem, *, layer, n_pages, cp):
    b = pl.program_id(0)
    n_seq = pl.num_programs(0)
    n_chunks = n_pages // cp
    cb = cp // 2
    nbp = n_pages // 2

    def copies(w):
        slot = w % 2
        return [pltpu.make_async_copy(ck_ref.at[layer, pt_ref[w * cp + p]], buf.at[slot, p], sem.at[slot])
                for p in range(cp)]

    @pl.when(b == 0)
    def _():
        for cpy in copies(0):
            cpy.start()

    qcols = [_row_to_col(q_ref[h:h + 1, :]) for h in range(ATT_HEADS)]
    head = lax.broadcasted_iota(jnp.int32, (ATT_HEADS, LANES), 0)
    lane = lax.broadcasted_iota(jnp.int32, (ATT_HEADS, LANES), 1)

    def body(c, gates):
        w = b * n_chunks + c
        slot = w % 2

        @pl.when(w + 1 < n_seq * n_chunks)
        def _():
            for cpy in copies(w + 1):
                cpy.start()

        for cpy in copies(w):
            cpy.wait()
        for jb in range(cb):
            per_token = jnp.zeros((ATT_HEADS, LANES), F32)
            for h in range(ATT_HEADS):
                kt = buf[slot, 2 * jb, h] + buf[slot, 2 * jb + 1, h]
                per_token = jnp.where(head == h, jnp.sum(kt * qcols[h], axis=0, keepdims=True), per_token)
            g = jnp.sum(per_token, axis=1, keepdims=True) * (1.0 / MOBA_BLOCK)
            gates = jnp.where(lane == c * cb + jb, g, gates)
        return gates

    gates = lax.fori_loop(0, n_chunks, body, jnp.full((ATT_HEADS, LANES), NEG_INF, F32))
    res = jnp.zeros((ATT_HEADS, LANES), jnp.int32)
    for r in range(min(MOBA_TOPK, nbp)):
        mx = jnp.max(gates, axis=1, keepdims=True)
        ix = jnp.min(jnp.where(gates == mx, lane, LANES), axis=1, keepdims=True)
        gates = jnp.where(lane == ix, NEG_INF, gates)
        res = jnp.where(lane == r, ix, res)
    o_ref[...] = res


def _dec_scan(page_table, q_hd, cache_kt, layer):
    n, n_pages = page_table.shape
    page_size = cache_kt.shape[4]
    assert page_size * 2 == MOBA_BLOCK and page_size == LANES and n_pages // 2 <= LANES
    cp = min(16, n_pages)
    grid_spec = pltpu.PrefetchScalarGridSpec(
        num_scalar_prefetch=1,
        grid=(n,),
        in_specs=[pl.BlockSpec((None, ATT_HEADS, ATT_HD), lambda b, pt: (b, 0, 0)),
                  pl.BlockSpec(memory_space=pl.ANY)],
        out_specs=pl.BlockSpec((None, ATT_HEADS, LANES), lambda b, pt: (b, 0, 0)),
        scratch_shapes=[pltpu.VMEM((2, cp, ATT_HEADS, ATT_HD, page_size), F32),
                        pltpu.SemaphoreType.DMA((2,))],
    )
    return pl.pallas_call(
        functools.partial(_dec_scan_kernel, layer=layer, n_pages=n_pages, cp=cp),
        grid_spec=grid_spec,
        out_shape=jax.ShapeDtypeStruct((n, ATT_HEADS, LANES), jnp.int32),
        compiler_params=_params(("arbitrary",)),
    )(page_table.reshape(-1), q_hd, cache_kt)


def _dec_attend_kernel(pt_ref, pick_ref, q_ref, kn_ref, vn_ref, ck_ref, cv_ref, o_ref, kb, vb, sem, *,
                       layer, n_pages, n_sel):
    b = pl.program_id(0)
    n_tiles = 2 * n_sel

    def copies(seq):
        slot = seq % 2
        out = []
        for h in range(ATT_HEADS):
            for r in range(n_sel):
                blk = pick_ref[(seq * ATT_HEADS + h) * n_sel + r]
                for p in range(2):
                    page = pt_ref[seq * n_pages + 2 * blk + p]
                    out.append(pltpu.make_async_copy(ck_ref.at[layer, page, h], kb.at[slot, h, 2 * r + p],
                                                     sem.at[2 * slot]))
                    out.append(pltpu.make_async_copy(cv_ref.at[layer, page, h], vb.at[slot, h, 2 * r + p],
                                                     sem.at[2 * slot + 1]))
        return out

    @pl.when(b == 0)
    def _():
        for cpy in copies(0):
            cpy.start()

    @pl.when(b + 1 < pl.num_programs(0))
    def _():
        for cpy in copies(b + 1):
            cpy.start()

    for cpy in copies(b):
        cpy.wait()
    slot = b % 2
    for h in range(ATT_HEADS):
        qrow = q_ref[h:h + 1, :]
        qcol = _row_to_col(qrow)
        s_new = jnp.sum(kn_ref[h:h + 1, :] * qrow, axis=1, keepdims=True)
        vcol_new = _row_to_col(vn_ref[h:h + 1, :])
        s = [jnp.sum(kb[slot, h, t] * qcol, axis=0, keepdims=True) for t in range(n_tiles)]
        m = s_new
        for st in s:
            m = jnp.maximum(m, jnp.max(st, axis=1, keepdims=True))
        p_new = jnp.exp(s_new - m)
        den = p_new
        num = p_new * vcol_new
        for t, st in enumerate(s):
            p = jnp.exp(st - m)
            den = den + jnp.sum(p, axis=1, keepdims=True)
            num = num + jnp.sum(vb[slot, h, t] * p, axis=1, keepdims=True)
        o_ref[h:h + 1, :] = _col_to_row(num / den)


def _dec_attend(page_table, picks, q_hd, k_new, v_new, cache_kt, cache_vt, layer):
    n, n_pages = page_table.shape
    page_size = cache_kt.shape[4]
    n_sel = min(MOBA_TOPK, n_pages // 2)
    tok = pl.BlockSpec((None, ATT_HEADS, ATT_HD), lambda b, pt, pk: (b, 0, 0))
    grid_spec = pltpu.PrefetchScalarGridSpec(
        num_scalar_prefetch=2,
        grid=(n,),
        in_specs=[tok, tok, tok, pl.BlockSpec(memory_space=pl.ANY), pl.BlockSpec(memory_space=pl.ANY)],
        out_specs=tok,
        scratch_shapes=[pltpu.VMEM((2, ATT_HEADS, 2 * n_sel, ATT_HD, page_size), F32),
                        pltpu.VMEM((2, ATT_HEADS, 2 * n_sel, ATT_HD, page_size), F32),
                        pltpu.SemaphoreType.DMA((4,))],
    )
    return pl.pallas_call(
        functools.partial(_dec_attend_kernel, layer=layer, n_pages=n_pages, n_sel=n_sel),
        grid_spec=grid_spec,
        out_shape=jax.ShapeDtypeStruct((n, ATT_HEADS, ATT_HD), F32),
        compiler_params=_params(("arbitrary",)),
    )(page_table.reshape(-1), picks, q_hd, k_new, v_new, cache_kt, cache_vt)


def _att_rope_tables(pos):
    half = ROPE_DIM // 2
    inv = ROPE_THETA ** (-2.0 * jnp.arange(half, dtype=F32) / ROPE_DIM)
    ang = pos.astype(F32)[:, None] * inv[None, :]
    cos, sin = jnp.cos(ang), jnp.sin(ang)
    t = pos.shape[0]
    rest = ATT_HD - ROPE_DIM
    ca = jnp.concatenate([cos, cos, jnp.ones((t, rest), F32)], axis=1)
    cm = jnp.concatenate([-sin, jnp.zeros((t, half + rest), F32)], axis=1)
    cp = jnp.concatenate([jnp.zeros((t, half), F32), sin, jnp.zeros((t, rest), F32)], axis=1)
    rep = LANES // ATT_HD
    return tuple(jnp.tile(a, (1, rep)) for a in (ca, cm, cp))


def _ret_rope_tables(pos):
    half = RET_DK // 2
    inv = RET_THETA ** (-2.0 * jnp.arange(half, dtype=F32) / RET_DK)
    ang = pos.astype(F32)[:, None] * inv[None, :]
    cos, sin = jnp.cos(ang), jnp.sin(ang)
    c = jnp.concatenate([cos, cos], axis=1)
    s = jnp.concatenate([-sin, sin], axis=1)
    return jnp.tile(c, (1, RET_HEADS)), jnp.tile(s, (1, RET_HEADS))


def kernel(x_prompt, x_sample, cache_k, cache_v, state_ret, state_conv, page_table, c_prompt, c_sample, w_ada, b_ada,
           w_in, conv_w, w_br, w_o, ln1_g, ln1_b, w_ffn_in, w_ffn_out, ln2_g, ln2_b):
    depth = w_in.shape[0]
    alpha = (2 * depth) ** 0.25
    bsz, seq, d = x_prompt.shape
    nd = x_sample.shape[0]
    assert x_sample.shape[1] == 1, "one new token per decode sequence"
    n_pool, page_size = cache_k.shape[1], cache_k.shape[2]
    past_len = page_table.shape[1] * page_size

    ada = _ada(jnp.concatenate([c_prompt, c_sample], axis=0), w_ada, b_ada)
    w_in_b = w_in[:, :, :N_MIX].astype(BF16)
    w_gate_b = w_in[:, :, N_MIX:].astype(BF16)
    w_br_b, w_o_b = w_br.astype(BF16), w_o.astype(BF16)
    w_fi_b, w_fo_b = w_ffn_in.astype(BF16), w_ffn_out.astype(BF16)
    cache_kt = jnp.transpose(cache_k, (0, 1, 3, 4, 2))
    cache_vt = jnp.transpose(cache_v, (0, 1, 3, 4, 2))

    pos_p = jnp.arange(seq, dtype=jnp.int32)
    pos_s = jnp.full((nd,), past_len, jnp.int32)
    att_tab_p, ret_tab_p = _att_rope_tables(pos_p), _ret_rope_tables(pos_p)
    att_tab_s = _att_rope_tables(pos_s)
    ret_tab_s = tuple(a.reshape(nd, 1, RET_QK_DIM) for a in _ret_rope_tables(pos_s))
    ret_consts = _ret_consts()

    xp = x_prompt
    xs = x_sample.reshape(1, nd, d)
    outs = {k: [] for k in ("kp", "vp", "ks", "vs", "rp", "rs", "cp", "cs")}
    for l in range(depth):
        mods_p = [m.reshape(bsz, 1, d) for m in jnp.split(ada[l, :bsz], 6, axis=-1)]
        mods_s = [m.reshape(1, nd, d) for m in jnp.split(ada[l, bsz:], 6, axis=-1)]
        lg1, lb1 = ln1_g[l].reshape(1, d), ln1_b[l].reshape(1, d)
        lg2, lb2 = ln2_g[l].reshape(1, d), ln2_b[l].reshape(1, d)

        sh1, sc1, g1, sh2, sc2, g2 = mods_p
        z = _inproj(xp, sc1, sh1, w_in_b, l)
        q_t, k_hm, v_t, k_tok, v_tok = _attn_prep(z, att_tab_p, head_major=True)
        kmean = _kmean(k_tok)
        kmean_hm = kmean.reshape(bsz, -1, ATT_HEADS, ATT_HD).transpose(0, 2, 1, 3)
        a_out = _moba(q_t, k_hm, v_t, kmean_hm)
        r_out, r_state = _retention(z, ret_tab_p, ret_consts)
        xp, c_state = _merge(xp, sc1, sh1, g1, a_out, r_out, z, conv_w[l], w_gate_b, w_br_b, w_o_b, lg1, lb1,
                             alpha, l)
        xp = _ffn(xp, sc2, sh2, g2, w_fi_b, w_fo_b, lg2, lb2, alpha, l)
        outs["kp"].append(k_tok.reshape(bsz, seq, ATT_HEADS, ATT_HD))
        outs["vp"].append(v_tok.reshape(bsz, seq, ATT_HEADS, ATT_HD))
        outs["rp"].append(r_state)
        outs["cp"].append(c_state)

        sh1, sc1, g1, sh2, sc2, g2 = mods_s
        z = _inproj(xs, sc1, sh1, w_in_b, l)
        q_tok, k_new, v_new = _attn_prep(z, att_tab_s, head_major=False)
        q_tok, k_new, v_new = (a.reshape(nd, ATT_HEADS, ATT_HD) for a in (q_tok, k_new, v_new))
        picks = _dec_scan(page_table, q_tok, cache_kt, l)
        n_sel = min(MOBA_TOPK, page_table.shape[1] // 2)
        a_out = _dec_attend(page_table, picks[:, :, :n_sel].reshape(-1), q_tok, k_new, v_new, cache_kt, cache_vt, l)
        r_out, r_state = _ret_step(z.reshape(nd, 1, N_MIX), ret_tab_s, state_ret[l])
        prev = (state_conv[l][:, 0, :].reshape(1, nd, CONV_DIM), state_conv[l][:, 1, :].reshape(1, nd, CONV_DIM))
        xs, s_new = _merge(xs, sc1, sh1, g1, a_out.reshape(1, nd, ATT_DIM), r_out.reshape(1, nd, RET_V_DIM), z,
                           conv_w[l], w_gate_b, w_br_b, w_o_b, lg1, lb1, alpha, l, conv_prev=prev)
        xs = _ffn(xs, sc2, sh2, g2, w_fi_b, w_fo_b, lg2, lb2, alpha, l)
        outs["ks"].append(k_new.reshape(nd, 1, ATT_HEADS, ATT_HD))
        outs["vs"].append(v_new.reshape(nd, 1, ATT_HEADS, ATT_HD))
        outs["rs"].append(r_state)
        outs["cs"].append(jnp.stack([state_conv[l][:, 1, :], s_new.reshape(nd, CONV_DIM)], axis=1))

    st = {k: jnp.stack(v) for k, v in outs.items()}
    return (xp, xs.reshape(nd, 1, d), st["kp"], st["vp"], st["ks"], st["vs"], st["rp"], st["rs"], st["cp"], st["cs"])
```

```python
import functools

import jax
import jax.numpy as jnp
from jax import lax
from jax.experimental import pallas as pl
from jax.experimental.pallas import tpu as pltpu

F32 = jnp.float32
BF16 = jnp.bfloat16

D_MODEL = 1024
ATT_HEADS = 8
ATT_HD = 64
ATT_DIM = ATT_HEADS * ATT_HD
ROPE_DIM = ATT_HD // 4
ROPE_THETA = 500000.0
MOBA_BLOCK = 256
MOBA_TOPK = 3
RET_HEADS = 4
RET_DK = 64
RET_DV = 128
RET_QK_DIM = RET_HEADS * RET_DK
RET_V_DIM = RET_HEADS * RET_DV
RET_THETA = 10000.0
RET_CHUNK = 128
CONV_DIM = 512
CONV_W = 3
N_BRANCH = 3
D_FF = 2816
LN_EPS = 1e-5
N_MIX = 3 * ATT_DIM + 2 * RET_QK_DIM + 2 * RET_V_DIM + 3 * CONV_DIM

OFF_AQ = 0
OFF_AK = OFF_AQ + ATT_DIM
OFF_AV = OFF_AK + ATT_DIM
OFF_RQ = OFF_AV + ATT_DIM
OFF_RK = OFF_RQ + RET_QK_DIM
OFF_RV = OFF_RK + RET_QK_DIM
OFF_RG = OFF_RV + RET_V_DIM
OFF_CB = OFF_RG + RET_V_DIM
OFF_CC = OFF_CB + CONV_DIM
OFF_CX = OFF_CC + CONV_DIM

LANES = 128
BF16_SUBLANES = 16
VT_ROWS = ATT_HD + BF16_SUBLANES
LOG2E = 1.4426950408889634
HPS = 4
VMEM_LIMIT = 48 * 1024 * 1024
NEG_INF = float("-inf")


def _params(semantics, vmem=VMEM_LIMIT):
    return pltpu.CompilerParams(dimension_semantics=semantics, vmem_limit_bytes=vmem)


def _ln(y):
    mu = jnp.mean(y, axis=-1, keepdims=True)
    yc = y - mu
    return yc * lax.rsqrt(jnp.mean(yc * yc, axis=-1, keepdims=True) + LN_EPS)


def _sigmoid(x):
    return 1.0 / (1.0 + jnp.exp(-x))


def _silu(x):
    return x * _sigmoid(x)


def _mod_spec(mod, tm, ngrid):
    per_row = mod.shape[1] != 1
    rows = tm if per_row else 1
    if ngrid == 2:
        return pl.BlockSpec((None, rows, mod.shape[2]), lambda b, i: (b, i if per_row else 0, 0))
    return pl.BlockSpec((None, rows, mod.shape[2]), lambda b, i, j: (b, i if per_row else 0, 0))


def _ada_kernel(c_ref, w_ref, b_ref, o_ref):
    cond = _silu(c_ref[...]).astype(BF16)
    o_ref[...] = jnp.dot(cond, w_ref[...].astype(BF16), preferred_element_type=F32) + b_ref[...]


def _ada(c, w_ada, b_ada):
    depth, d, n = w_ada.shape
    rows = c.shape[0]
    tn = 1536
    return pl.pallas_call(
        _ada_kernel,
        grid=(depth, n // tn),
        in_specs=[pl.BlockSpec((rows, d), lambda l, j: (0, 0)),
                  pl.BlockSpec((None, d, tn), lambda l, j: (l, 0, j)),
                  pl.BlockSpec((None, 1, tn), lambda l, j: (l, 0, j))],
        out_specs=pl.BlockSpec((None, rows, tn), lambda l, j: (l, 0, j)),
        out_shape=jax.ShapeDtypeStruct((depth, rows, n), F32),
        compiler_params=_params(("arbitrary", "arbitrary")),
    )(c, w_ada, b_ada.reshape(depth, 1, n))


def _inproj_kernel(x_ref, sc_ref, sh_ref, w_ref, o_ref, u_ref):
    @pl.when(pl.program_id(2) == 0)
    def _():
        u_ref[...] = (x_ref[...] * (1.0 + sc_ref[...]) + sh_ref[...]).astype(BF16)

    o_ref[...] = jnp.dot(u_ref[...], w_ref[...], preferred_element_type=F32)


def _inproj(x, sc, sh, w):
    bsz, t, d = x.shape
    n = w.shape[1]
    tm = min(t, 1024)
    tn = 1536
    return pl.pallas_call(
        _inproj_kernel,
        grid=(bsz, t // tm, n // tn),
        in_specs=[pl.BlockSpec((None, tm, d), lambda b, i, j: (b, i, 0)),
                  _mod_spec(sc, tm, 3), _mod_spec(sh, tm, 3),
                  pl.BlockSpec((d, tn), lambda b, i, j: (0, j))],
        out_specs=pl.BlockSpec((None, tm, tn), lambda b, i, j: (b, i, j)),
        out_shape=jax.ShapeDtypeStruct((bsz, t, n), F32),
        scratch_shapes=[pltpu.VMEM((tm, d), BF16)],
        compiler_params=_params(("arbitrary", "arbitrary", "arbitrary")),
    )(x, sc, sh, w)


def _rope_att(x, ca, cm, cp):
    return x * ca + pltpu.roll(x, LANES - ROPE_DIM // 2, 1) * cm + pltpu.roll(x, ROPE_DIM // 2, 1) * cp


def _attn_prep_kernel(q_ref, k_ref, v_ref, ca_ref, cm_ref, cp_ref, *out_refs, head_major, tm):
    ca, cm, cp = ca_ref[...], cm_ref[...], cp_ref[...]
    scale = ATT_HD ** -0.5
    if head_major:
        qt_ref, kh_ref, vt_ref, ko_ref, vo_ref = out_refs
    else:
        qo_ref, ko_ref, vo_ref = out_refs
    vo_ref[...] = v_ref[...]
    for c in range(ATT_DIM // LANES):
        sl = slice(c * LANES, (c + 1) * LANES)
        q = _rope_att(q_ref[:, sl], ca, cm, cp) * scale
        k = _rope_att(k_ref[:, sl], ca, cm, cp)
        ko_ref[:, sl] = k
        if head_major:
            qt = q.T
            vt = v_ref[:, sl].T.astype(BF16)
            for hh in range(2):
                hs = slice(hh * ATT_HD, (hh + 1) * ATT_HD)
                kh_ref[2 * c + hh] = k[:, hs].astype(BF16)
                for j in range(tm // MOBA_BLOCK):
                    cols = slice(j * MOBA_BLOCK, (j + 1) * MOBA_BLOCK)
                    qt_ref[2 * c + hh, j] = qt[hs, cols]
                    vt_ref[2 * c + hh, j, 0:ATT_HD, :] = vt[hs, cols]
                    vt_ref[2 * c + hh, j, ATT_HD:VT_ROWS, :] = jnp.ones((VT_ROWS - ATT_HD, MOBA_BLOCK), BF16)
        else:
            qo_ref[:, sl] = q


def _attn_prep(z, tabs, head_major):
    bsz, t, _ = z.shape
    tm = min(t, 512)
    cq, ck, cv = OFF_AQ // ATT_DIM, OFF_AK // ATT_DIM, OFF_AV // ATT_DIM
    tok = pl.BlockSpec((None, tm, ATT_DIM), lambda b, i: (b, i, 0))
    tab = pl.BlockSpec((tm, LANES), lambda b, i: (i, 0))
    tok_shape = jax.ShapeDtypeStruct((bsz, t, ATT_DIM), F32)
    if head_major:
        nblk = tm // MOBA_BLOCK
        hm = pl.BlockSpec((None, ATT_HEADS, tm, ATT_HD), lambda b, i: (b, 0, i, 0))
        tr = pl.BlockSpec((None, ATT_HEADS, nblk, ATT_HD, MOBA_BLOCK), lambda b, i: (b, 0, i, 0, 0))
        trv = pl.BlockSpec((None, ATT_HEADS, nblk, VT_ROWS, MOBA_BLOCK), lambda b, i: (b, 0, i, 0, 0))
        nb = t // MOBA_BLOCK
        out_specs = [tr, hm, trv, tok, tok]
        out_shape = [jax.ShapeDtypeStruct((bsz, ATT_HEADS, nb, ATT_HD, MOBA_BLOCK), F32),
                     jax.ShapeDtypeStruct((bsz, ATT_HEADS, t, ATT_HD), BF16),
                     jax.ShapeDtypeStruct((bsz, ATT_HEADS, nb, VT_ROWS, MOBA_BLOCK), BF16), tok_shape, tok_shape]
    else:
        out_specs = [tok, tok, tok]
        out_shape = [tok_shape, tok_shape, tok_shape]
    return pl.pallas_call(
        functools.partial(_attn_prep_kernel, head_major=head_major, tm=tm),
        grid=(bsz, t // tm),
        in_specs=[pl.BlockSpec((None, tm, ATT_DIM), lambda b, i: (b, i, cq)),
                  pl.BlockSpec((None, tm, ATT_DIM), lambda b, i: (b, i, ck)),
                  pl.BlockSpec((None, tm, ATT_DIM), lambda b, i: (b, i, cv)),
                  tab, tab, tab],
        out_specs=out_specs,
        out_shape=out_shape,
        compiler_params=_params(("arbitrary", "arbitrary")),
    )(z, z, z, *tabs)


def _kmean_kernel(k_ref, o_ref, *, nblk):
    for j in range(nblk):
        o_ref[j:j + 1, :] = jnp.mean(k_ref[j * MOBA_BLOCK:(j + 1) * MOBA_BLOCK, :], axis=0, keepdims=True)


def _kmean(k_tok):
    bsz, t, _ = k_tok.shape
    rows = min(t, 8 * MOBA_BLOCK)
    nblk = rows // MOBA_BLOCK
    return pl.pallas_call(
        functools.partial(_kmean_kernel, nblk=nblk),
        grid=(bsz, t // rows),
        in_specs=[pl.BlockSpec((None, rows, ATT_DIM), lambda b, i: (b, i, 0))],
        out_specs=pl.BlockSpec((None, nblk, ATT_DIM), lambda b, i: (b, i, 0)),
        out_shape=jax.ShapeDtypeStruct((bsz, t // MOBA_BLOCK, ATT_DIM), F32),
        compiler_params=_params(("arbitrary", "arbitrary")),
    )(k_tok)


def _moba_kernel(qt_ref, k_ref, vt_ref, km_ref, o_ref, s_scr, p_scr, *, nb):
    i = pl.program_id(2)
    blk = MOBA_BLOCK
    key = lax.broadcasted_iota(jnp.int32, (blk, blk), 0)
    qry = lax.broadcasted_iota(jnp.int32, (blk, blk), 1)
    bid = lax.broadcasted_iota(jnp.int32, (nb, blk), 0)
    qtbs, pickss, init = [], [], []

    def scores(j):
        off = pl.multiple_of(j * blk, blk)
        return [jnp.dot(k_ref[hh, pl.ds(off, blk), :], qtbs[hh], preferred_element_type=F32) for hh in range(HPS)]

    qts = [qt_ref[hh] for hh in range(HPS)]
    qtbs.extend((qt * LOG2E).astype(BF16) for qt in qts)

    for hh, s in enumerate(scores(0)):
        s_scr[0, hh] = s
        p_scr[hh] = jnp.zeros((blk, blk), BF16)
    own = scores(i)

    for hh in range(HPS):
        gate = jnp.dot(km_ref[hh], qts[hh], precision=lax.Precision.HIGHEST, preferred_element_type=F32)
        gate = jnp.where(bid < i, gate, NEG_INF)
        picks = []
        for r in range(MOBA_TOPK):
            mx = jnp.max(gate, axis=0, keepdims=True)
            ix = jnp.min(jnp.where(gate == mx, bid, nb), axis=0, keepdims=True)
            gate = jnp.where(bid == ix, NEG_INF, gate)
            picks.append(jnp.where(r < i, ix, -1))
        pickss.append(picks)

    for hh, s in enumerate(own):
        s = jnp.where(key <= qry, s, NEG_INF)
        m0 = jnp.max(s, axis=0, keepdims=True)
        p = jnp.exp2(s - m0)
        acc0 = jnp.dot(vt_ref[hh, i], p.astype(BF16), preferred_element_type=F32)
        init.append((m0, acc0, jnp.ones_like(m0)))

    def finish(jp, hh, acc, alpha):
        return alpha * acc + jnp.dot(vt_ref[hh, jp], p_scr[hh], preferred_element_type=F32)

    def step(j, carry, cur):
        jp = jnp.maximum(j - 1, 0)
        lagged = [finish(jp, hh, carry[hh][1], carry[hh][2]) for hh in range(HPS)]
        nxt = scores(jnp.minimum(j + 1, nb - 1))
        new = []
        for hh in range(HPS):
            m = carry[hh][0]
            picks = pickss[hh]
            live = (picks[0] == j) | (picks[1] == j) | (picks[2] == j)
            sj = s_scr[cur, hh]
            m_new = jnp.maximum(m, jnp.where(live, jnp.max(sj, axis=0, keepdims=True), NEG_INF))
            alpha = jnp.exp2(m - m_new)
            pj = jnp.exp2(sj - jnp.where(live, m_new, jnp.inf))
            p_scr[hh] = pj.astype(BF16)
            s_scr[1 - cur, hh] = nxt[hh]
            new.append((m_new, lagged[hh], alpha))
        return tuple(new)

    def pair(t, carry):
        return step(2 * t + 1, step(2 * t, carry, 0), 1)

    trips = (i + 1) // 2
    fin = lax.fori_loop(0, trips, pair, tuple(init))
    j_last = jnp.maximum(2 * trips - 1, 0)
    outs = []
    for hh, (_, acc, alpha) in enumerate(fin):
        acc = finish(j_last, hh, acc, alpha)
        outs.append(acc[0:ATT_HD] / acc[ATT_HD:ATT_HD + 1])
    o_ref[...] = jnp.concatenate(outs, axis=0).T


def _moba(qt, k_hm, vt, kmean_hm):
    bsz, nh, nb, hd, blk = qt.shape
    t = nb * blk
    return pl.pallas_call(
        functools.partial(_moba_kernel, nb=nb),
        grid=(bsz, nh // HPS, nb),
        in_specs=[pl.BlockSpec((None, HPS, None, hd, blk), lambda b, h, i: (b, h, i, 0, 0)),
                  pl.BlockSpec((None, HPS, t, hd), lambda b, h, i: (b, h, 0, 0)),
                  pl.BlockSpec((None, HPS, nb, vt.shape[3], blk), lambda b, h, i: (b, h, 0, 0, 0)),
                  pl.BlockSpec((None, HPS, nb, hd), lambda b, h, i: (b, h, 0, 0))],
        out_specs=pl.BlockSpec((None, blk, HPS * hd), lambda b, h, i: (b, i, h)),
        out_shape=jax.ShapeDtypeStruct((bsz, t, nh * hd), F32),
        scratch_shapes=[pltpu.VMEM((2, HPS, blk, blk), F32), pltpu.VMEM((HPS, blk, blk), BF16)],
        compiler_params=_params(("arbitrary", "arbitrary", "arbitrary")),
    )(qt, k_hm, vt, kmean_hm)


def _rope_ret(x, c, s):
    w = x.shape[1]
    d = lax.broadcasted_iota(jnp.int32, x.shape, 1) % RET_DK
    half = RET_DK // 2
    partner = jnp.where(d < half, pltpu.roll(x, w - half, 1), pltpu.roll(x, half, 1))
    return x * c + partner * s


def _ret_decay(h):
    return 1.0 - 2.0 ** (-5.0 - h)


def _retention_kernel(q_ref, k_ref, v_ref, g_ref, c_ref, s_ref, dm_ref, qd_ref, kd_ref, o_ref, st_ref, state, *, tm):
    ch = RET_CHUNK

    @pl.when(pl.program_id(1) == 0)
    def _():
        state[...] = jnp.zeros_like(state)

    q = _rope_ret(q_ref[...], c_ref[...], s_ref[...])
    k = _rope_ret(k_ref[...], c_ref[...], s_ref[...]) * (RET_DK ** -0.5)
    kt = k.T
    for c in range(tm // ch):
        rows = slice(c * ch, (c + 1) * ch)
        for h in range(RET_HEADS):
            qc = q[rows, h * RET_DK:(h + 1) * RET_DK]
            ktc = kt[h * RET_DK:(h + 1) * RET_DK, rows]
            vc = v_ref[rows, h * RET_DV:(h + 1) * RET_DV].astype(BF16)
            s_old = state[h]
            inner = jnp.dot(qc.astype(BF16), ktc.astype(BF16), preferred_element_type=F32) * dm_ref[h]
            o = jnp.dot(inner.astype(BF16), vc, preferred_element_type=F32)
            o = o + jnp.dot((qc * qd_ref[h]).astype(BF16), s_old.astype(BF16), preferred_element_type=F32)
            c_dec = _ret_decay(h) ** ch
            state[h] = c_dec * s_old + jnp.dot((ktc * kd_ref[h]).astype(BF16), vc, preferred_element_type=F32)
            gate = g_ref[rows, h * RET_DV:(h + 1) * RET_DV]
            o_ref[rows, h * RET_DV:(h + 1) * RET_DV] = _ln(o) * _silu(gate)
    st_ref[...] = state[...]


def _retention(z, tabs, consts):
    bsz, t, _ = z.shape
    tm = min(t, 512)
    dmat, qdec, kdec = consts
    const3 = lambda a: pl.BlockSpec(a.shape, lambda b, i: (0, 0, 0))
    tab = pl.BlockSpec((tm, RET_QK_DIM), lambda b, i: (i, 0))
    return pl.pallas_call(
        functools.partial(_retention_kernel, tm=tm),
        grid=(bsz, t // tm),
        in_specs=[pl.BlockSpec((None, tm, RET_QK_DIM), lambda b, i: (b, i, OFF_RQ // RET_QK_DIM)),
                  pl.BlockSpec((None, tm, RET_QK_DIM), lambda b, i: (b, i, OFF_RK // RET_QK_DIM)),
                  pl.BlockSpec((None, tm, RET_V_DIM), lambda b, i: (b, i, OFF_RV // RET_V_DIM)),
                  pl.BlockSpec((None, tm, RET_V_DIM), lambda b, i: (b, i, OFF_RG // RET_V_DIM)),
                  tab, tab, const3(dmat), const3(qdec), const3(kdec)],
        out_specs=[pl.BlockSpec((None, tm, RET_V_DIM), lambda b, i: (b, i, 0)),
                   pl.BlockSpec((None, RET_HEADS, RET_DK, RET_DV), lambda b, i: (b, 0, 0, 0))],
        out_shape=[jax.ShapeDtypeStruct((bsz, t, RET_V_DIM), F32),
                   jax.ShapeDtypeStruct((bsz, RET_HEADS, RET_DK, RET_DV), F32)],
        scratch_shapes=[pltpu.VMEM((RET_HEADS, RET_DK, RET_DV), F32)],
        compiler_params=_params(("arbitrary", "arbitrary")),
    )(z, z, z, z, *tabs, dmat, qdec, kdec)


def _ret_consts():
    ch = RET_CHUNK
    log_g = jnp.log(1.0 - 2.0 ** (-5.0 - jnp.arange(RET_HEADS, dtype=F32)))
    i = jnp.arange(ch, dtype=F32)
    diff = i[:, None] - i[None, :]
    dmat = jnp.where(diff >= 0, jnp.exp(jnp.maximum(diff, 0.0)[None] * log_g[:, None, None]), 0.0)
    qdec = jnp.exp((i + 1.0)[None, :] * log_g[:, None])[:, :, None]
    kdec = jnp.exp((ch - 1.0 - i)[None, :] * log_g[:, None])[:, None, :]
    return dmat, qdec, kdec


def _row_to_col(row):
    n = row.shape[1]
    eye = lax.broadcasted_iota(jnp.int32, (n, n), 0) == lax.broadcasted_iota(jnp.int32, (n, n), 1)
    return jnp.sum(jnp.where(eye, jnp.broadcast_to(row, (n, n)), 0.0), axis=1, keepdims=True)


def _ret_step_kernel(q_ref, k_ref, v_ref, g_ref, c_ref, s_ref, st_ref, o_ref, sn_ref):
    q = _rope_ret(q_ref[...], c_ref[...], s_ref[...])
    k = _rope_ret(k_ref[...], c_ref[...], s_ref[...]) * (RET_DK ** -0.5)
    for h in range(RET_HEADS):
        g = _ret_decay(h)
        qrow = q[:, h * RET_DK:(h + 1) * RET_DK]
        krow = k[:, h * RET_DK:(h + 1) * RET_DK]
        qcol = _row_to_col(qrow)
        kcol = _row_to_col(krow)
        v = v_ref[:, h * RET_DV:(h + 1) * RET_DV]
        s_old = st_ref[h]
        inner = jnp.sum(qrow * krow, axis=1, keepdims=True)
        o = inner * v + jnp.sum((qcol * g) * s_old, axis=0, keepdims=True)
        sn_ref[h] = g * s_old + kcol * v
        o_ref[:, h * RET_DV:(h + 1) * RET_DV] = _ln(o) * _silu(g_ref[:, h * RET_DV:(h + 1) * RET_DV])


def _ret_step(z, tabs, state):
    n = z.shape[0]
    tab = pl.BlockSpec((None, 1, RET_QK_DIM), lambda b: (b, 0, 0))
    st = pl.BlockSpec((None, RET_HEADS, RET_DK, RET_DV), lambda b: (b, 0, 0, 0))
    return pl.pallas_call(
        _ret_step_kernel,
        grid=(n,),
        in_specs=[pl.BlockSpec((None, 1, RET_QK_DIM), lambda b: (b, 0, OFF_RQ // RET_QK_DIM)),
                  pl.BlockSpec((None, 1, RET_QK_DIM), lambda b: (b, 0, OFF_RK // RET_QK_DIM)),
                  pl.BlockSpec((None, 1, RET_V_DIM), lambda b: (b, 0, OFF_RV // RET_V_DIM)),
                  pl.BlockSpec((None, 1, RET_V_DIM), lambda b: (b, 0, OFF_RG // RET_V_DIM)),
                  tab, tab, st],
        out_specs=[pl.BlockSpec((None, 1, RET_V_DIM), lambda b: (b, 0, 0)), st],
        out_shape=[jax.ShapeDtypeStruct((n, 1, RET_V_DIM), F32),
                   jax.ShapeDtypeStruct(state.shape, F32)],
        compiler_params=_params(("arbitrary",)),
    )(z, z, z, z, *tabs, state)


def _merge_kernel(*refs, tm, seq_mode, alpha):
    if seq_mode:
        (x_ref, sc_ref, sh_ref, g1_ref, a_ref, r_ref, cb_ref, cc_ref, cx_ref, cw_ref, wg_ref, wbr_ref, wo_ref,
         lg_ref, lb_ref, o_ref, cs_ref, carry) = refs
    else:
        (x_ref, sc_ref, sh_ref, g1_ref, a_ref, r_ref, cb_ref, cc_ref, cx_ref, cw_ref, wg_ref, wbr_ref, wo_ref,
         lg_ref, lb_ref, p0_ref, p1_ref, o_ref, cs_ref) = refs
    d = x_ref.shape[-1]
    x = x_ref[...]
    u = (x * (1.0 + sc_ref[...]) + sh_ref[...]).astype(BF16)
    s = cc_ref[...] * cx_ref[...]
    if seq_mode:
        @pl.when(pl.program_id(1) == 0)
        def _():
            carry[...] = jnp.zeros_like(carry)

        rid = lax.broadcasted_iota(jnp.int32, s.shape, 0)
        prev0, prev1 = carry[0:1, :], carry[1:2, :]
        s1 = jnp.where(rid == 0, prev1, pltpu.roll(s, 1, 0))
        s2 = jnp.where(rid == 0, prev0, jnp.where(rid == 1, prev1, pltpu.roll(s, 2, 0)))
        carry[...] = s[tm - 2:tm, :]
        cs_ref[...] = s[tm - 2:tm, :]
    else:
        s2, s1 = p0_ref[...], p1_ref[...]
        cs_ref[...] = s
    y = cw_ref[0:1, :] * s2 + cw_ref[1:2, :] * s1 + cw_ref[2:3, :] * s
    c_out = cb_ref[...] * y
    merged = None
    for n, branch in enumerate((a_ref[...], r_ref[...], c_out)):
        gt = jnp.dot(u, wg_ref[:, n * d:(n + 1) * d], preferred_element_type=F32)
        term = _sigmoid(gt) * jnp.dot(branch.astype(BF16), wbr_ref[n], preferred_element_type=F32)
        merged = term if merged is None else merged + term
    m = jnp.dot(merged.astype(BF16), wo_ref[...], preferred_element_type=F32)
    o_ref[...] = _ln(alpha * x + g1_ref[...] * m) * lg_ref[...] + lb_ref[...]


def _merge(x, sc, sh, g1, a_out, r_out, z, conv_w, w_gate, w_br, w_o, ln_g, ln_b, alpha, conv_prev=None):
    bsz, t, d = x.shape
    seq_mode = conv_prev is None
    tm = min(t, 512)
    tok512 = lambda cblk: pl.BlockSpec((None, tm, CONV_DIM), lambda b, i: (b, i, cblk))
    full2 = lambda a: pl.BlockSpec(a.shape, lambda b, i: (0, 0))
    in_specs = [pl.BlockSpec((None, tm, d), lambda b, i: (b, i, 0)),
                _mod_spec(sc, tm, 2), _mod_spec(sh, tm, 2), _mod_spec(g1, tm, 2),
                tok512(0), tok512(0), tok512(OFF_CB // CONV_DIM), tok512(OFF_CC // CONV_DIM),
                tok512(OFF_CX // CONV_DIM),
                full2(conv_w), full2(w_gate), pl.BlockSpec(w_br.shape, lambda b, i: (0, 0, 0)), full2(w_o),
                full2(ln_g), full2(ln_b)]
    args = [x, sc, sh, g1, a_out, r_out, z, z, z, conv_w, w_gate, w_br, w_o, ln_g, ln_b]
    scratch = []
    if seq_mode:
        cs_spec = pl.BlockSpec((None, CONV_W - 1, CONV_DIM), lambda b, i: (b, 0, 0))
        cs_shape = jax.ShapeDtypeStruct((bsz, CONV_W - 1, CONV_DIM), F32)
        scratch = [pltpu.VMEM((CONV_W - 1, CONV_DIM), F32)]
    else:
        in_specs += [tok512(0), tok512(0)]
        args += list(conv_prev)
        cs_spec = tok512(0)
        cs_shape = jax.ShapeDtypeStruct((bsz, t, CONV_DIM), F32)
    return pl.pallas_call(
        functools.partial(_merge_kernel, tm=tm, seq_mode=seq_mode, alpha=alpha),
        grid=(bsz, t // tm),
        in_specs=in_specs,
        out_specs=[pl.BlockSpec((None, tm, d), lambda b, i: (b, i, 0)), cs_spec],
        out_shape=[jax.ShapeDtypeStruct((bsz, t, d), F32), cs_shape],
        scratch_shapes=scratch,
        compiler_params=_params(("arbitrary", "arbitrary")),
    )(*args)


def _ffn_kernel(x_ref, sc_ref, sh_ref, g2_ref, wa_ref, wb_ref, wo_ref, lg_ref, lb_ref, o_ref, u_ref, acc_ref, *,
                alpha):
    f = pl.program_id(2)

    @pl.when(f == 0)
    def _():
        u_ref[...] = (x_ref[...] * (1.0 + sc_ref[...]) + sh_ref[...]).astype(BF16)
        acc_ref[...] = jnp.zeros_like(acc_ref)

    u = u_ref[...]
    a = jnp.dot(u, wa_ref[...], preferred_element_type=F32)
    b = jnp.dot(u, wb_ref[...], preferred_element_type=F32)
    acc_ref[...] += jnp.dot((_silu(a) * b).astype(BF16), wo_ref[...], preferred_element_type=F32)

    @pl.when(f == pl.num_programs(2) - 1)
    def _():
        o_ref[...] = _ln(alpha * x_ref[...] + g2_ref[...] * acc_ref[...]) * lg_ref[...] + lb_ref[...]


def _ffn(x, sc, sh, g2, w_in, w_out, ln_g, ln_b, alpha):
    bsz, t, d = x.shape
    tm = min(t, 512)
    tf = D_FF // 2
    nf = D_FF // tf
    full2 = lambda a: pl.BlockSpec(a.shape, lambda b, i, f: (0, 0))
    return pl.pallas_call(
        functools.partial(_ffn_kernel, alpha=alpha),
        grid=(bsz, t // tm, nf),
        in_specs=[pl.BlockSpec((None, tm, d), lambda b, i, f: (b, i, 0)),
                  _mod_spec(sc, tm, 3), _mod_spec(sh, tm, 3), _mod_spec(g2, tm, 3),
                  pl.BlockSpec((d, tf), lambda b, i, f: (0, f)),
                  pl.BlockSpec((d, tf), lambda b, i, f: (0, nf + f)),
                  pl.BlockSpec((tf, d), lambda b, i, f: (f, 0)),
                  full2(ln_g), full2(ln_b)],
        out_specs=pl.BlockSpec((None, tm, d), lambda b, i, f: (b, i, 0)),
        out_shape=jax.ShapeDtypeStruct((bsz, t, d), F32),
        scratch_shapes=[pltpu.VMEM((tm, d), BF16), pltpu.VMEM((tm, d), F32)],
        compiler_params=_params(("arbitrary", "arbitrary", "arbitrary")),
    )(x, sc, sh, g2, w_in, w_in, w_out, ln_g, ln_b)


def _col_to_row(col):
    n = col.shape[0]
    eye = lax.broadcasted_iota(jnp.int32, (n, n), 0) == lax.broadcasted_iota(jnp.int32, (n, n), 1)
    return jnp.sum(jnp.where(eye, jnp.broadcast_to(col, (n, n)), 0.0), axis=0, keepdims=True)


def _dec_scan_kernel(pt_ref, q_ref, ck_ref, o_ref, buf, sem, *, layer, n_pages, cp):
    b = pl.program_id(0)
    n_seq = pl.num_programs(0)
    n_chunks = n_pages // cp
    cb = cp // 2
    nbp = n_pages // 2

    def copies(w):
        slot = w % 2
        return [pltpu.make_async_copy(ck_ref.at[layer, pt_ref[w * cp + p]], buf.at[slot, p], sem.at[slot])
                for p in range(cp)]

    @pl.when(b == 0)
    def _():
        for cpy in copies(0):
            cpy.start()

    qcols = [_row_to_col(q_ref[h:h + 1, :]) for h in range(ATT_HEADS)]
    head = lax.broadcasted_iota(jnp.int32, (ATT_HEADS, LANES), 0)
    lane = lax.broadcasted_iota(jnp.int32, (ATT_HEADS, LANES), 1)

    def body(c, gates):
        w = b * n_chunks + c
        slot = w % 2

        @pl.when(w + 1 < n_seq * n_chunks)
        def _():
            for cpy in copies(w + 1):
                cpy.start()

        for cpy in copies(w):
            cpy.wait()
        for jb in range(cb):
            per_token = jnp.zeros((ATT_HEADS, LANES), F32)
            for h in range(ATT_HEADS):
                kt = buf[slot, 2 * jb, h] + buf[slot, 2 * jb + 1, h]
                per_token = jnp.where(head == h, jnp.sum(kt * qcols[h], axis=0, keepdims=True), per_token)
            g = jnp.sum(per_token, axis=1, keepdims=True) * (1.0 / MOBA_BLOCK)
            gates = jnp.where(lane == c * cb + jb, g, gates)
        return gates

    gates = lax.fori_loop(0, n_chunks, body, jnp.full((ATT_HEADS, LANES), NEG_INF, F32))
    res = jnp.zeros((ATT_HEADS, LANES), jnp.int32)
    for r in range(min(MOBA_TOPK, nbp)):
        mx = jnp.max(gates, axis=1, keepdims=True)
        ix = jnp.min(jnp.where(gates == mx, lane, LANES), axis=1, keepdims=True)
        gates = jnp.where(lane == ix, NEG_INF, gates)
        res = jnp.where(lane == r, ix, res)
    o_ref[...] = res


def _dec_scan(page_table, q_hd, cache_kt, layer):
    n, n_pages = page_table.shape
    page_size = cache_kt.shape[4]
    assert page_size * 2 == MOBA_BLOCK and page_size == LANES and n_pages // 2 <= LANES
    cp = min(32, n_pages)
    grid_spec = pltpu.PrefetchScalarGridSpec(
        num_scalar_prefetch=1,
        grid=(n,),
        in_specs=[pl.BlockSpec((None, ATT_HEADS, ATT_HD), lambda b, pt: (b, 0, 0)),
                  pl.BlockSpec(memory_space=pl.ANY)],
        out_specs=pl.BlockSpec((None, ATT_HEADS, LANES), lambda b, pt: (b, 0, 0)),
        scratch_shapes=[pltpu.VMEM((2, cp, ATT_HEADS, ATT_HD, page_size), F32),
                        pltpu.SemaphoreType.DMA((2,))],
    )
    return pl.pallas_call(
        functools.partial(_dec_scan_kernel, layer=layer, n_pages=n_pages, cp=cp),
        grid_spec=grid_spec,
        out_shape=jax.ShapeDtypeStruct((n, ATT_HEADS, LANES), jnp.int32),
        compiler_params=_params(("arbitrary",)),
    )(page_table.reshape(-1), q_hd, cache_kt)


def _dec_attend_kernel(pt_ref, pick_ref, q_ref, kn_ref, vn_ref, ck_ref, cv_ref, o_ref, kb, vb, sem, *,
                       layer, n_pages, n_sel):
    b = pl.program_id(0)
    n_tiles = 2 * n_sel

    def copies(seq):
        slot = seq % 2
        out = []
        for h in range(ATT_HEADS):
            for r in range(n_sel):
                blk = pick_ref[(seq * ATT_HEADS + h) * n_sel + r]
                for p in range(2):
                    page = pt_ref[seq * n_pages + 2 * blk + p]
                    out.append(pltpu.make_async_copy(ck_ref.at[layer, page, h], kb.at[slot, h, 2 * r + p],
                                                     sem.at[2 * slot]))
                    out.append(pltpu.make_async_copy(cv_ref.at[layer, page, h], vb.at[slot, h, 2 * r + p],
                                                     sem.at[2 * slot + 1]))
        return out

    @pl.when(b == 0)
    def _():
        for cpy in copies(0):
            cpy.start()

    @pl.when(b + 1 < pl.num_programs(0))
    def _():
        for cpy in copies(b + 1):
            cpy.start()

    for cpy in copies(b):
        cpy.wait()
    slot = b % 2
    for h in range(ATT_HEADS):
        qrow = q_ref[h:h + 1, :]
        qcol = _row_to_col(qrow)
        s_new = jnp.sum(kn_ref[h:h + 1, :] * qrow, axis=1, keepdims=True)
        vcol_new = _row_to_col(vn_ref[h:h + 1, :])
        s = [jnp.sum(kb[slot, h, t] * qcol, axis=0, keepdims=True) for t in range(n_tiles)]
        m = s_new
        for st in s:
            m = jnp.maximum(m, jnp.max(st, axis=1, keepdims=True))
        p_new = jnp.exp(s_new - m)
        den = p_new
        num = p_new * vcol_new
        for t, st in enumerate(s):
            p = jnp.exp(st - m)
            den = den + jnp.sum(p, axis=1, keepdims=True)
            num = num + jnp.sum(vb[slot, h, t] * p, axis=1, keepdims=True)
        o_ref[h:h + 1, :] = _col_to_row(num / den)


def _dec_attend(page_table, picks, q_hd, k_new, v_new, cache_kt, cache_vt, layer):
    n, n_pages = page_table.shape
    page_size = cache_kt.shape[4]
    n_sel = min(MOBA_TOPK, n_pages // 2)
    tok = pl.BlockSpec((None, ATT_HEADS, ATT_HD), lambda b, pt, pk: (b, 0, 0))
    grid_spec = pltpu.PrefetchScalarGridSpec(
        num_scalar_prefetch=2,
        grid=(n,),
        in_specs=[tok, tok, tok, pl.BlockSpec(memory_space=pl.ANY), pl.BlockSpec(memory_space=pl.ANY)],
        out_specs=tok,
        scratch_shapes=[pltpu.VMEM((2, ATT_HEADS, 2 * n_sel, ATT_HD, page_size), F32),
                        pltpu.VMEM((2, ATT_HEADS, 2 * n_sel, ATT_HD, page_size), F32),
                        pltpu.SemaphoreType.DMA((4,))],
    )
    return pl.pallas_call(
        functools.partial(_dec_attend_kernel, layer=layer, n_pages=n_pages, n_sel=n_sel),
        grid_spec=grid_spec,
        out_shape=jax.ShapeDtypeStruct((n, ATT_HEADS, ATT_HD), F32),
        compiler_params=_params(("arbitrary",)),
    )(page_table.reshape(-1), picks, q_hd, k_new, v_new, cache_kt, cache_vt)


def _att_rope_tables(pos):
    half = ROPE_DIM // 2
    inv = ROPE_THETA ** (-2.0 * jnp.arange(half, dtype=F32) / ROPE_DIM)
    ang = pos.astype(F32)[:, None] * inv[None, :]
    cos, sin = jnp.cos(ang), jnp.sin(ang)
    t = pos.shape[0]
    rest = ATT_HD - ROPE_DIM
    ca = jnp.concatenate([cos, cos, jnp.ones((t, rest), F32)], axis=1)
    cm = jnp.concatenate([-sin, jnp.zeros((t, half + rest), F32)], axis=1)
    cp = jnp.concatenate([jnp.zeros((t, half), F32), sin, jnp.zeros((t, rest), F32)], axis=1)
    rep = LANES // ATT_HD
    return tuple(jnp.tile(a, (1, rep)) for a in (ca, cm, cp))


def _ret_rope_tables(pos):
    half = RET_DK // 2
    inv = RET_THETA ** (-2.0 * jnp.arange(half, dtype=F32) / RET_DK)
    ang = pos.astype(F32)[:, None] * inv[None, :]
    cos, sin = jnp.cos(ang), jnp.sin(ang)
    c = jnp.concatenate([cos, cos], axis=1)
    s = jnp.concatenate([-sin, sin], axis=1)
    return jnp.tile(c, (1, RET_HEADS)), jnp.tile(s, (1, RET_HEADS))


def kernel(x_prompt, x_sample, cache_k, cache_v, state_ret, state_conv, page_table, c_prompt, c_sample, w_ada, b_ada,
           w_in, conv_w, w_br, w_o, ln1_g, ln1_b, w_ffn_in, w_ffn_out, ln2_g, ln2_b):
    depth = w_in.shape[0]
    alpha = (2 * depth) ** 0.25
    bsz, seq, d = x_prompt.shape
    nd = x_sample.shape[0]
    assert x_sample.shape[1] == 1, "one new token per decode sequence"
    n_pool, page_size = cache_k.shape[1], cache_k.shape[2]
    past_len = page_table.shape[1] * page_size

    ada = _ada(jnp.concatenate([c_prompt, c_sample], axis=0), w_ada, b_ada)
    w_in_b = w_in[:, :, :N_MIX].astype(BF16)
    w_gate_b = w_in[:, :, N_MIX:].astype(BF16)
    w_br_b, w_o_b = w_br.astype(BF16), w_o.astype(BF16)
    w_fi_b, w_fo_b = w_ffn_in.astype(BF16), w_ffn_out.astype(BF16)
    cache_kt = jnp.transpose(cache_k, (0, 1, 3, 4, 2))
    cache_vt = jnp.transpose(cache_v, (0, 1, 3, 4, 2))

    pos_p = jnp.arange(seq, dtype=jnp.int32)
    pos_s = jnp.full((nd,), past_len, jnp.int32)
    att_tab_p, ret_tab_p = _att_rope_tables(pos_p), _ret_rope_tables(pos_p)
    att_tab_s = _att_rope_tables(pos_s)
    ret_tab_s = tuple(a.reshape(nd, 1, RET_QK_DIM) for a in _ret_rope_tables(pos_s))
    ret_consts = _ret_consts()

    xp = x_prompt
    xs = x_sample.reshape(1, nd, d)
    outs = {k: [] for k in ("kp", "vp", "ks", "vs", "rp", "rs", "cp", "cs")}
    for l in range(depth):
        mods_p = [m.reshape(bsz, 1, d) for m in jnp.split(ada[l, :bsz], 6, axis=-1)]
        mods_s = [m.reshape(1, nd, d) for m in jnp.split(ada[l, bsz:], 6, axis=-1)]
        lg1, lb1 = ln1_g[l].reshape(1, d), ln1_b[l].reshape(1, d)
        lg2, lb2 = ln2_g[l].reshape(1, d), ln2_b[l].reshape(1, d)

        sh1, sc1, g1, sh2, sc2, g2 = mods_p
        z = _inproj(xp, sc1, sh1, w_in_b[l])
        q_t, k_hm, v_t, k_tok, v_tok = _attn_prep(z, att_tab_p, head_major=True)
        kmean = _kmean(k_tok)
        kmean_hm = kmean.reshape(bsz, -1, ATT_HEADS, ATT_HD).transpose(0, 2, 1, 3)
        a_out = _moba(q_t, k_hm, v_t, kmean_hm)
        r_out, r_state = _retention(z, ret_tab_p, ret_consts)
        xp, c_state = _merge(xp, sc1, sh1, g1, a_out, r_out, z, conv_w[l], w_gate_b[l], w_br_b[l], w_o_b[l], lg1, lb1,
                             alpha)
        xp = _ffn(xp, sc2, sh2, g2, w_fi_b[l], w_fo_b[l], lg2, lb2, alpha)
        outs["kp"].append(k_tok.reshape(bsz, seq, ATT_HEADS, ATT_HD))
        outs["vp"].append(v_tok.reshape(bsz, seq, ATT_HEADS, ATT_HD))
        outs["rp"].append(r_state)
        outs["cp"].append(c_state)

        sh1, sc1, g1, sh2, sc2, g2 = mods_s
        z = _inproj(xs, sc1, sh1, w_in_b[l])
        q_tok, k_new, v_new = _attn_prep(z, att_tab_s, head_major=False)
        q_tok, k_new, v_new = (a.reshape(nd, ATT_HEADS, ATT_HD) for a in (q_tok, k_new, v_new))
        picks = _dec_scan(page_table, q_tok, cache_kt, l)
        n_sel = min(MOBA_TOPK, page_table.shape[1] // 2)
        a_out = _dec_attend(page_table, picks[:, :, :n_sel].reshape(-1), q_tok, k_new, v_new, cache_kt, cache_vt, l)
        r_out, r_state = _ret_step(z.reshape(nd, 1, N_MIX), ret_tab_s, state_ret[l])
        prev = (state_conv[l][:, 0, :].reshape(1, nd, CONV_DIM), state_conv[l][:, 1, :].reshape(1, nd, CONV_DIM))
        xs, s_new = _merge(xs, sc1, sh1, g1, a_out.reshape(1, nd, ATT_DIM), r_out.reshape(1, nd, RET_V_DIM), z,
                           conv_w[l], w_gate_b[l], w_br_b[l], w_o_b[l], lg1, lb1, alpha, conv_prev=prev)
        xs = _ffn(xs, sc2, sh2, g2, w_fi_b[l], w_fo_b[l], lg2, lb2, alpha)
        outs["ks"].append(k_new.reshape(nd, 1, ATT_HEADS, ATT_HD))
        outs["vs"].append(v_new.reshape(nd, 1, ATT_HEADS, ATT_HD))
        outs["rs"].append(r_state)
        outs["cs"].append(jnp.stack([state_conv[l][:, 1, :], s_new.reshape(nd, CONV_DIM)], axis=1))

    st = {k: jnp.stack(v) for k, v in outs.items()}
    return (xp, xs.reshape(nd, 1, d), st["kp"], st["vp"], st["ks"], st["vs"], st["rp"], st["rs"], st["cp"], st["cs"])
```

```python
import functools

import jax
import jax.numpy as jnp
from jax import lax
from jax.experimental import pallas as pl
from jax.experimental.pallas import tpu as pltpu

F32 = jnp.float32
BF16 = jnp.bfloat16

D_MODEL = 1024
ATT_HEADS = 8
ATT_HD = 64
ATT_DIM = ATT_HEADS * ATT_HD
ROPE_DIM = ATT_HD // 4
ROPE_THETA = 500000.0
MOBA_BLOCK = 256
MOBA_TOPK = 3
RET_HEADS = 4
RET_DK = 64
RET_DV = 128
RET_QK_DIM = RET_HEADS * RET_DK
RET_V_DIM = RET_HEADS * RET_DV
RET_THETA = 10000.0
RET_CHUNK = 128
CONV_DIM = 512
CONV_W = 3
N_BRANCH = 3
D_FF = 2816
LN_EPS = 1e-5
N_MIX = 3 * ATT_DIM + 2 * RET_QK_DIM + 2 * RET_V_DIM + 3 * CONV_DIM

OFF_AQ = 0
OFF_AK = OFF_AQ + ATT_DIM
OFF_AV = OFF_AK + ATT_DIM
OFF_RQ = OFF_AV + ATT_DIM
OFF_RK = OFF_RQ + RET_QK_DIM
OFF_RV = OFF_RK + RET_QK_DIM
OFF_RG = OFF_RV + RET_V_DIM
OFF_CB = OFF_RG + RET_V_DIM
OFF_CC = OFF_CB + CONV_DIM
OFF_CX = OFF_CC + CONV_DIM

LANES = 128
BF16_SUBLANES = 16
VT_ROWS = ATT_HD + BF16_SUBLANES
LOG2E = 1.4426950408889634
HPS = 4
VMEM_LIMIT = 48 * 1024 * 1024
NEG_INF = float("-inf")


def _params(semantics, vmem=VMEM_LIMIT):
    return pltpu.CompilerParams(dimension_semantics=semantics, vmem_limit_bytes=vmem)


def _ln(y):
    mu = jnp.mean(y, axis=-1, keepdims=True)
    yc = y - mu
    return yc * lax.rsqrt(jnp.mean(yc * yc, axis=-1, keepdims=True) + LN_EPS)


def _sigmoid(x):
    return 1.0 / (1.0 + jnp.exp(-x))


def _silu(x):
    return x * _sigmoid(x)


def _mod_spec(mod, tm, ngrid):
    per_row = mod.shape[1] != 1
    rows = tm if per_row else 1
    if ngrid == 2:
        return pl.BlockSpec((None, rows, mod.shape[2]), lambda b, i: (b, i if per_row else 0, 0))
    return pl.BlockSpec((None, rows, mod.shape[2]), lambda b, i, j: (b, i if per_row else 0, 0))


def _ada_kernel(c_ref, w_ref, b_ref, o_ref):
    cond = _silu(c_ref[...]).astype(BF16)
    o_ref[...] = jnp.dot(cond, w_ref[...].astype(BF16), preferred_element_type=F32) + b_ref[...]


def _ada(c, w_ada, b_ada):
    depth, d, n = w_ada.shape
    rows = c.shape[0]
    tn = 1536
    return pl.pallas_call(
        _ada_kernel,
        grid=(depth, n // tn),
        in_specs=[pl.BlockSpec((rows, d), lambda l, j: (0, 0)),
                  pl.BlockSpec((None, d, tn), lambda l, j: (l, 0, j)),
                  pl.BlockSpec((None, 1, tn), lambda l, j: (l, 0, j))],
        out_specs=pl.BlockSpec((None, rows, tn), lambda l, j: (l, 0, j)),
        out_shape=jax.ShapeDtypeStruct((depth, rows, n), F32),
        compiler_params=_params(("arbitrary", "arbitrary")),
    )(c, w_ada, b_ada.reshape(depth, 1, n))


def _inproj_kernel(x_ref, sc_ref, sh_ref, w_ref, o_ref, u_ref):
    @pl.when(pl.program_id(2) == 0)
    def _():
        u_ref[...] = (x_ref[...] * (1.0 + sc_ref[...]) + sh_ref[...]).astype(BF16)

    o_ref[...] = jnp.dot(u_ref[...], w_ref[...], preferred_element_type=F32)


def _inproj(x, sc, sh, w):
    bsz, t, d = x.shape
    n = w.shape[1]
    tm = min(t, 1024)
    tn = 1536
    return pl.pallas_call(
        _inproj_kernel,
        grid=(bsz, t // tm, n // tn),
        in_specs=[pl.BlockSpec((None, tm, d), lambda b, i, j: (b, i, 0)),
                  _mod_spec(sc, tm, 3), _mod_spec(sh, tm, 3),
                  pl.BlockSpec((d, tn), lambda b, i, j: (0, j))],
        out_specs=pl.BlockSpec((None, tm, tn), lambda b, i, j: (b, i, j)),
        out_shape=jax.ShapeDtypeStruct((bsz, t, n), F32),
        scratch_shapes=[pltpu.VMEM((tm, d), BF16)],
        compiler_params=_params(("arbitrary", "arbitrary", "arbitrary")),
    )(x, sc, sh, w)


def _rope_att(x, ca, cm, cp):
    return x * ca + pltpu.roll(x, LANES - ROPE_DIM // 2, 1) * cm + pltpu.roll(x, ROPE_DIM // 2, 1) * cp


def _attn_prep_kernel(q_ref, k_ref, v_ref, ca_ref, cm_ref, cp_ref, *out_refs, head_major, tm):
    ca, cm, cp = ca_ref[...], cm_ref[...], cp_ref[...]
    scale = ATT_HD ** -0.5
    if head_major:
        qt_ref, kh_ref, vt_ref, ko_ref, vo_ref = out_refs
    else:
        qo_ref, ko_ref, vo_ref = out_refs
    vo_ref[...] = v_ref[...]
    for c in range(ATT_DIM // LANES):
        sl = slice(c * LANES, (c + 1) * LANES)
        q = _rope_att(q_ref[:, sl], ca, cm, cp) * scale
        k = _rope_att(k_ref[:, sl], ca, cm, cp)
        ko_ref[:, sl] = k
        if head_major:
            qt = q.T
            vt = v_ref[:, sl].T.astype(BF16)
            for hh in range(2):
                hs = slice(hh * ATT_HD, (hh + 1) * ATT_HD)
                kh_ref[2 * c + hh] = k[:, hs].astype(BF16)
                for j in range(tm // MOBA_BLOCK):
                    cols = slice(j * MOBA_BLOCK, (j + 1) * MOBA_BLOCK)
                    qt_ref[2 * c + hh, j] = qt[hs, cols]
                    vt_ref[2 * c + hh, j, 0:ATT_HD, :] = vt[hs, cols]
                    vt_ref[2 * c + hh, j, ATT_HD:VT_ROWS, :] = jnp.ones((VT_ROWS - ATT_HD, MOBA_BLOCK), BF16)
        else:
            qo_ref[:, sl] = q


def _attn_prep(z, tabs, head_major):
    bsz, t, _ = z.shape
    tm = min(t, 512)
    cq, ck, cv = OFF_AQ // ATT_DIM, OFF_AK // ATT_DIM, OFF_AV // ATT_DIM
    tok = pl.BlockSpec((None, tm, ATT_DIM), lambda b, i: (b, i, 0))
    tab = pl.BlockSpec((tm, LANES), lambda b, i: (i, 0))
    tok_shape = jax.ShapeDtypeStruct((bsz, t, ATT_DIM), F32)
    if head_major:
        nblk = tm // MOBA_BLOCK
        hm = pl.BlockSpec((None, ATT_HEADS, tm, ATT_HD), lambda b, i: (b, 0, i, 0))
        tr = pl.BlockSpec((None, ATT_HEADS, nblk, ATT_HD, MOBA_BLOCK), lambda b, i: (b, 0, i, 0, 0))
        trv = pl.BlockSpec((None, ATT_HEADS, nblk, VT_ROWS, MOBA_BLOCK), lambda b, i: (b, 0, i, 0, 0))
        nb = t // MOBA_BLOCK
        out_specs = [tr, hm, trv, tok, tok]
        out_shape = [jax.ShapeDtypeStruct((bsz, ATT_HEADS, nb, ATT_HD, MOBA_BLOCK), F32),
                     jax.ShapeDtypeStruct((bsz, ATT_HEADS, t, ATT_HD), BF16),
                     jax.ShapeDtypeStruct((bsz, ATT_HEADS, nb, VT_ROWS, MOBA_BLOCK), BF16), tok_shape, tok_shape]
    else:
        out_specs = [tok, tok, tok]
        out_shape = [tok_shape, tok_shape, tok_shape]
    return pl.pallas_call(
        functools.partial(_attn_prep_kernel, head_major=head_major, tm=tm),
        grid=(bsz, t // tm),
        in_specs=[pl.BlockSpec((None, tm, ATT_DIM), lambda b, i: (b, i, cq)),
                  pl.BlockSpec((None, tm, ATT_DIM), lambda b, i: (b, i, ck)),
                  pl.BlockSpec((None, tm, ATT_DIM), lambda b, i: (b, i, cv)),
                  tab, tab, tab],
        out_specs=out_specs,
        out_shape=out_shape,
        compiler_params=_params(("arbitrary", "arbitrary")),
    )(z, z, z, *tabs)


def _kmean_kernel(k_ref, o_ref, *, nblk):
    for j in range(nblk):
        o_ref[j:j + 1, :] = jnp.mean(k_ref[j * MOBA_BLOCK:(j + 1) * MOBA_BLOCK, :], axis=0, keepdims=True)


def _kmean(k_tok):
    bsz, t, _ = k_tok.shape
    rows = min(t, 8 * MOBA_BLOCK)
    nblk = rows // MOBA_BLOCK
    return pl.pallas_call(
        functools.partial(_kmean_kernel, nblk=nblk),
        grid=(bsz, t // rows),
        in_specs=[pl.BlockSpec((None, rows, ATT_DIM), lambda b, i: (b, i, 0))],
        out_specs=pl.BlockSpec((None, nblk, ATT_DIM), lambda b, i: (b, i, 0)),
        out_shape=jax.ShapeDtypeStruct((bsz, t // MOBA_BLOCK, ATT_DIM), F32),
        compiler_params=_params(("arbitrary", "arbitrary")),
    )(k_tok)


def _moba_kernel(qt_ref, k_ref, vt_ref, km_ref, o_ref, s_scr, p_scr, acc_scr, *, nb):
    i = pl.program_id(2)
    blk = MOBA_BLOCK
    key = lax.broadcasted_iota(jnp.int32, (blk, blk), 0)
    qry = lax.broadcasted_iota(jnp.int32, (blk, blk), 1)
    bid = lax.broadcasted_iota(jnp.int32, (nb, blk), 0)
    qtbs, pickss, init = [], [], []

    def scores(j):
        off = pl.multiple_of(j * blk, blk)
        return [jnp.dot(k_ref[hh, pl.ds(off, blk), :], qtbs[hh], preferred_element_type=F32) for hh in range(HPS)]

    qts = [qt_ref[hh] for hh in range(HPS)]
    qtbs.extend((qt * LOG2E).astype(BF16) for qt in qts)

    for hh, s in enumerate(scores(0)):
        s_scr[0, hh] = s
        p_scr[hh] = jnp.zeros((blk, blk), BF16)
    own = scores(i)

    for hh in range(HPS):
        gate = jnp.dot(km_ref[hh], qts[hh], precision=lax.Precision.HIGHEST, preferred_element_type=F32)
        gate = jnp.where(bid < i, gate, NEG_INF)
        picks = []
        for r in range(MOBA_TOPK):
            mx = jnp.max(gate, axis=0, keepdims=True)
            ix = jnp.min(jnp.where(gate == mx, bid, nb), axis=0, keepdims=True)
            gate = jnp.where(bid == ix, NEG_INF, gate)
            picks.append(jnp.where(r < i, ix, -1))
        pickss.append(picks)

    for hh, s in enumerate(own):
        s = jnp.where(key <= qry, s, NEG_INF)
        m0 = jnp.max(s, axis=0, keepdims=True)
        p = jnp.exp2(s - m0)
        acc_scr[hh] = jnp.dot(vt_ref[hh, i], p.astype(BF16), preferred_element_type=F32)
        init.append((m0, jnp.ones_like(m0)))

    def finish(jp, hh, alpha):
        acc_scr[hh] = alpha * acc_scr[hh] + jnp.dot(vt_ref[hh, jp], p_scr[hh], preferred_element_type=F32)

    def step(j, carry, cur):
        jp = jnp.maximum(j - 1, 0)
        for hh in range(HPS):
            finish(jp, hh, carry[hh][1])
        nxt = scores(jnp.minimum(j + 1, nb - 1))
        new = []
        for hh in range(HPS):
            m = carry[hh][0]
            picks = pickss[hh]
            live = (picks[0] == j) | (picks[1] == j) | (picks[2] == j)
            sj = s_scr[cur, hh]
            m_new = jnp.maximum(m, jnp.where(live, jnp.max(sj, axis=0, keepdims=True), NEG_INF))
            alpha = jnp.exp2(m - m_new)
            pj = jnp.exp2(sj - jnp.where(live, m_new, jnp.inf))
            p_scr[hh] = pj.astype(BF16)
            s_scr[1 - cur, hh] = nxt[hh]
            new.append((m_new, alpha))
        return tuple(new)

    def pair(t, carry):
        return step(2 * t + 1, step(2 * t, carry, 0), 1)

    trips = (i + 1) // 2
    fin = lax.fori_loop(0, trips, pair, tuple(init))
    j_last = jnp.maximum(2 * trips - 1, 0)
    outs = []
    for hh, (_, alpha) in enumerate(fin):
        finish(j_last, hh, alpha)
        acc = acc_scr[hh]
        outs.append(acc[0:ATT_HD] / acc[ATT_HD:ATT_HD + 1])
    o_ref[...] = jnp.concatenate(outs, axis=0).T


def _moba(qt, k_hm, vt, kmean_hm):
    bsz, nh, nb, hd, blk = qt.shape
    t = nb * blk
    return pl.pallas_call(
        functools.partial(_moba_kernel, nb=nb),
        grid=(bsz, nh // HPS, nb),
        in_specs=[pl.BlockSpec((None, HPS, None, hd, blk), lambda b, h, i: (b, h, i, 0, 0)),
                  pl.BlockSpec((None, HPS, t, hd), lambda b, h, i: (b, h, 0, 0)),
                  pl.BlockSpec((None, HPS, nb, vt.shape[3], blk), lambda b, h, i: (b, h, 0, 0, 0)),
                  pl.BlockSpec((None, HPS, nb, hd), lambda b, h, i: (b, h, 0, 0))],
        out_specs=pl.BlockSpec((None, blk, HPS * hd), lambda b, h, i: (b, i, h)),
        out_shape=jax.ShapeDtypeStruct((bsz, t, nh * hd), F32),
        scratch_shapes=[pltpu.VMEM((2, HPS, blk, blk), F32), pltpu.VMEM((HPS, blk, blk), BF16),
                        pltpu.VMEM((HPS, vt.shape[3], blk), F32)],
        compiler_params=_params(("arbitrary", "arbitrary", "arbitrary")),
    )(qt, k_hm, vt, kmean_hm)


def _rope_ret(x, c, s):
    w = x.shape[1]
    d = lax.broadcasted_iota(jnp.int32, x.shape, 1) % RET_DK
    half = RET_DK // 2
    partner = jnp.where(d < half, pltpu.roll(x, w - half, 1), pltpu.roll(x, half, 1))
    return x * c + partner * s


def _ret_decay(h):
    return 1.0 - 2.0 ** (-5.0 - h)


def _retention_kernel(q_ref, k_ref, v_ref, g_ref, c_ref, s_ref, dm_ref, qd_ref, kd_ref, o_ref, st_ref, state, *, tm):
    ch = RET_CHUNK

    @pl.when(pl.program_id(1) == 0)
    def _():
        state[...] = jnp.zeros_like(state)

    q = _rope_ret(q_ref[...], c_ref[...], s_ref[...])
    k = _rope_ret(k_ref[...], c_ref[...], s_ref[...]) * (RET_DK ** -0.5)
    kt = k.T
    for c in range(tm // ch):
        rows = slice(c * ch, (c + 1) * ch)
        for h in range(RET_HEADS):
            qc = q[rows, h * RET_DK:(h + 1) * RET_DK]
            ktc = kt[h * RET_DK:(h + 1) * RET_DK, rows]
            vc = v_ref[rows, h * RET_DV:(h + 1) * RET_DV].astype(BF16)
            s_old = state[h]
            inner = jnp.dot(qc.astype(BF16), ktc.astype(BF16), preferred_element_type=F32) * dm_ref[h]
            o = jnp.dot(inner.astype(BF16), vc, preferred_element_type=F32)
            o = o + jnp.dot((qc * qd_ref[h]).astype(BF16), s_old.astype(BF16), preferred_element_type=F32)
            c_dec = _ret_decay(h) ** ch
            state[h] = c_dec * s_old + jnp.dot((ktc * kd_ref[h]).astype(BF16), vc, preferred_element_type=F32)
            gate = g_ref[rows, h * RET_DV:(h + 1) * RET_DV]
            o_ref[rows, h * RET_DV:(h + 1) * RET_DV] = _ln(o) * _silu(gate)
    st_ref[...] = state[...]


def _retention(z, tabs, consts):
    bsz, t, _ = z.shape
    tm = min(t, 512)
    dmat, qdec, kdec = consts
    const3 = lambda a: pl.BlockSpec(a.shape, lambda b, i: (0, 0, 0))
    tab = pl.BlockSpec((tm, RET_QK_DIM), lambda b, i: (i, 0))
    return pl.pallas_call(
        functools.partial(_retention_kernel, tm=tm),
        grid=(bsz, t // tm),
        in_specs=[pl.BlockSpec((None, tm, RET_QK_DIM), lambda b, i: (b, i, OFF_RQ // RET_QK_DIM)),
                  pl.BlockSpec((None, tm, RET_QK_DIM), lambda b, i: (b, i, OFF_RK // RET_QK_DIM)),
                  pl.BlockSpec((None, tm, RET_V_DIM), lambda b, i: (b, i, OFF_RV // RET_V_DIM)),
                  pl.BlockSpec((None, tm, RET_V_DIM), lambda b, i: (b, i, OFF_RG // RET_V_DIM)),
                  tab, tab, const3(dmat), const3(qdec), const3(kdec)],
        out_specs=[pl.BlockSpec((None, tm, RET_V_DIM), lambda b, i: (b, i, 0)),
                   pl.BlockSpec((None, RET_HEADS, RET_DK, RET_DV), lambda b, i: (b, 0, 0, 0))],
        out_shape=[jax.ShapeDtypeStruct((bsz, t, RET_V_DIM), F32),
                   jax.ShapeDtypeStruct((bsz, RET_HEADS, RET_DK, RET_DV), F32)],
        scratch_shapes=[pltpu.VMEM((RET_HEADS, RET_DK, RET_DV), F32)],
        compiler_params=_params(("arbitrary", "arbitrary")),
    )(z, z, z, z, *tabs, dmat, qdec, kdec)


def _ret_consts():
    ch = RET_CHUNK
    log_g = jnp.log(1.0 - 2.0 ** (-5.0 - jnp.arange(RET_HEADS, dtype=F32)))
    i = jnp.arange(ch, dtype=F32)
    diff = i[:, None] - i[None, :]
    dmat = jnp.where(diff >= 0, jnp.exp(jnp.maximum(diff, 0.0)[None] * log_g[:, None, None]), 0.0)
    qdec = jnp.exp((i + 1.0)[None, :] * log_g[:, None])[:, :, None]
    kdec = jnp.exp((ch - 1.0 - i)[None, :] * log_g[:, None])[:, None, :]
    return dmat, qdec, kdec


def _row_to_col(row):
    n = row.shape[1]
    eye = lax.broadcasted_iota(jnp.int32, (n, n), 0) == lax.broadcasted_iota(jnp.int32, (n, n), 1)
    return jnp.sum(jnp.where(eye, jnp.broadcast_to(row, (n, n)), 0.0), axis=1, keepdims=True)


def _ret_step_kernel(q_ref, k_ref, v_ref, g_ref, c_ref, s_ref, st_ref, o_ref, sn_ref):
    q = _rope_ret(q_ref[...], c_ref[...], s_ref[...])
    k = _rope_ret(k_ref[...], c_ref[...], s_ref[...]) * (RET_DK ** -0.5)
    for h in range(RET_HEADS):
        g = _ret_decay(h)
        qrow = q[:, h * RET_DK:(h + 1) * RET_DK]
        krow = k[:, h * RET_DK:(h + 1) * RET_DK]
        qcol = _row_to_col(qrow)
        kcol = _row_to_col(krow)
        v = v_ref[:, h * RET_DV:(h + 1) * RET_DV]
        s_old = st_ref[h]
        inner = jnp.sum(qrow * krow, axis=1, keepdims=True)
        o = inner * v + jnp.sum((qcol * g) * s_old, axis=0, keepdims=True)
        sn_ref[h] = g * s_old + kcol * v
        o_ref[:, h * RET_DV:(h + 1) * RET_DV] = _ln(o) * _silu(g_ref[:, h * RET_DV:(h + 1) * RET_DV])


def _ret_step(z, tabs, state):
    n = z.shape[0]
    tab = pl.BlockSpec((None, 1, RET_QK_DIM), lambda b: (b, 0, 0))
    st = pl.BlockSpec((None, RET_HEADS, RET_DK, RET_DV), lambda b: (b, 0, 0, 0))
    return pl.pallas_call(
        _ret_step_kernel,
        grid=(n,),
        in_specs=[pl.BlockSpec((None, 1, RET_QK_DIM), lambda b: (b, 0, OFF_RQ // RET_QK_DIM)),
                  pl.BlockSpec((None, 1, RET_QK_DIM), lambda b: (b, 0, OFF_RK // RET_QK_DIM)),
                  pl.BlockSpec((None, 1, RET_V_DIM), lambda b: (b, 0, OFF_RV // RET_V_DIM)),
                  pl.BlockSpec((None, 1, RET_V_DIM), lambda b: (b, 0, OFF_RG // RET_V_DIM)),
                  tab, tab, st],
        out_specs=[pl.BlockSpec((None, 1, RET_V_DIM), lambda b: (b, 0, 0)), st],
        out_shape=[jax.ShapeDtypeStruct((n, 1, RET_V_DIM), F32),
                   jax.ShapeDtypeStruct(state.shape, F32)],
        compiler_params=_params(("arbitrary",)),
    )(z, z, z, z, *tabs, state)


def _merge_kernel(*refs, tm, seq_mode, alpha):
    if seq_mode:
        (x_ref, sc_ref, sh_ref, g1_ref, a_ref, r_ref, cb_ref, cc_ref, cx_ref, cw_ref, wg_ref, wbr_ref, wo_ref,
         lg_ref, lb_ref, o_ref, cs_ref, carry) = refs
    else:
        (x_ref, sc_ref, sh_ref, g1_ref, a_ref, r_ref, cb_ref, cc_ref, cx_ref, cw_ref, wg_ref, wbr_ref, wo_ref,
         lg_ref, lb_ref, p0_ref, p1_ref, o_ref, cs_ref) = refs
    d = x_ref.shape[-1]
    x = x_ref[...]
    u = (x * (1.0 + sc_ref[...]) + sh_ref[...]).astype(BF16)
    s = cc_ref[...] * cx_ref[...]
    if seq_mode:
        @pl.when(pl.program_id(1) == 0)
        def _():
            carry[...] = jnp.zeros_like(carry)

        rid = lax.broadcasted_iota(jnp.int32, s.shape, 0)
        prev0, prev1 = carry[0:1, :], carry[1:2, :]
        s1 = jnp.where(rid == 0, prev1, pltpu.roll(s, 1, 0))
        s2 = jnp.where(rid == 0, prev0, jnp.where(rid == 1, prev1, pltpu.roll(s, 2, 0)))
        carry[...] = s[tm - 2:tm, :]
        cs_ref[...] = s[tm - 2:tm, :]
    else:
        s2, s1 = p0_ref[...], p1_ref[...]
        cs_ref[...] = s
    y = cw_ref[0:1, :] * s2 + cw_ref[1:2, :] * s1 + cw_ref[2:3, :] * s
    c_out = cb_ref[...] * y
    merged = None
    for n, branch in enumerate((a_ref[...], r_ref[...], c_out)):
        gt = jnp.dot(u, wg_ref[:, n * d:(n + 1) * d], preferred_element_type=F32)
        term = _sigmoid(gt) * jnp.dot(branch.astype(BF16), wbr_ref[n], preferred_element_type=F32)
        merged = term if merged is None else merged + term
    m = jnp.dot(merged.astype(BF16), wo_ref[...], preferred_element_type=F32)
    o_ref[...] = _ln(alpha * x + g1_ref[...] * m) * lg_ref[...] + lb_ref[...]


def _merge(x, sc, sh, g1, a_out, r_out, z, conv_w, w_gate, w_br, w_o, ln_g, ln_b, alpha, conv_prev=None):
    bsz, t, d = x.shape
    seq_mode = conv_prev is None
    tm = min(t, 512)
    tok512 = lambda cblk: pl.BlockSpec((None, tm, CONV_DIM), lambda b, i: (b, i, cblk))
    full2 = lambda a: pl.BlockSpec(a.shape, lambda b, i: (0, 0))
    in_specs = [pl.BlockSpec((None, tm, d), lambda b, i: (b, i, 0)),
                _mod_spec(sc, tm, 2), _mod_spec(sh, tm, 2), _mod_spec(g1, tm, 2),
                tok512(0), tok512(0), tok512(OFF_CB // CONV_DIM), tok512(OFF_CC // CONV_DIM),
                tok512(OFF_CX // CONV_DIM),
                full2(conv_w), full2(w_gate), pl.BlockSpec(w_br.shape, lambda b, i: (0, 0, 0)), full2(w_o),
                full2(ln_g), full2(ln_b)]
    args = [x, sc, sh, g1, a_out, r_out, z, z, z, conv_w, w_gate, w_br, w_o, ln_g, ln_b]
    scratch = []
    if seq_mode:
        cs_spec = pl.BlockSpec((None, CONV_W - 1, CONV_DIM), lambda b, i: (b, 0, 0))
        cs_shape = jax.ShapeDtypeStruct((bsz, CONV_W - 1, CONV_DIM), F32)
        scratch = [pltpu.VMEM((CONV_W - 1, CONV_DIM), F32)]
    else:
        in_specs += [tok512(0), tok512(0)]
        args += list(conv_prev)
        cs_spec = tok512(0)
        cs_shape = jax.ShapeDtypeStruct((bsz, t, CONV_DIM), F32)
    return pl.pallas_call(
        functools.partial(_merge_kernel, tm=tm, seq_mode=seq_mode, alpha=alpha),
        grid=(bsz, t // tm),
        in_specs=in_specs,
        out_specs=[pl.BlockSpec((None, tm, d), lambda b, i: (b, i, 0)), cs_spec],
        out_shape=[jax.ShapeDtypeStruct((bsz, t, d), F32), cs_shape],
        scratch_shapes=scratch,
        compiler_params=_params(("arbitrary", "arbitrary")),
    )(*args)


def _ffn_kernel(x_ref, sc_ref, sh_ref, g2_ref, wa_ref, wb_ref, wo_ref, lg_ref, lb_ref, o_ref, u_ref, acc_ref, *,
                alpha):
    f = pl.program_id(2)

    @pl.when(f == 0)
    def _():
        u_ref[...] = (x_ref[...] * (1.0 + sc_ref[...]) + sh_ref[...]).astype(BF16)
        acc_ref[...] = jnp.zeros_like(acc_ref)

    u = u_ref[...]
    a = jnp.dot(u, wa_ref[...], preferred_element_type=F32)
    b = jnp.dot(u, wb_ref[...], preferred_element_type=F32)
    acc_ref[...] += jnp.dot((_silu(a) * b).astype(BF16), wo_ref[...], preferred_element_type=F32)

    @pl.when(f == pl.num_programs(2) - 1)
    def _():
        o_ref[...] = _ln(alpha * x_ref[...] + g2_ref[...] * acc_ref[...]) * lg_ref[...] + lb_ref[...]


def _ffn(x, sc, sh, g2, w_in, w_out, ln_g, ln_b, alpha):
    bsz, t, d = x.shape
    tm = min(t, 512)
    tf = D_FF // 2
    nf = D_FF // tf
    full2 = lambda a: pl.BlockSpec(a.shape, lambda b, i, f: (0, 0))
    return pl.pallas_call(
        functools.partial(_ffn_kernel, alpha=alpha),
        grid=(bsz, t // tm, nf),
        in_specs=[pl.BlockSpec((None, tm, d), lambda b, i, f: (b, i, 0)),
                  _mod_spec(sc, tm, 3), _mod_spec(sh, tm, 3), _mod_spec(g2, tm, 3),
                  pl.BlockSpec((d, tf), lambda b, i, f: (0, f)),
                  pl.BlockSpec((d, tf), lambda b, i, f: (0, nf + f)),
                  pl.BlockSpec((tf, d), lambda b, i, f: (f, 0)),
                  full2(ln_g), full2(ln_b)],
        out_specs=pl.BlockSpec((None, tm, d), lambda b, i, f: (b, i, 0)),
        out_shape=jax.ShapeDtypeStruct((bsz, t, d), F32),
        scratch_shapes=[pltpu.VMEM((tm, d), BF16), pltpu.VMEM((tm, d), F32)],
        compiler_params=_params(("arbitrary", "arbitrary", "arbitrary")),
    )(x, sc, sh, g2, w_in, w_in, w_out, ln_g, ln_b)


def _col_to_row(col):
    n = col.shape[0]
    eye = lax.broadcasted_iota(jnp.int32, (n, n), 0) == lax.broadcasted_iota(jnp.int32, (n, n), 1)
    return jnp.sum(jnp.where(eye, jnp.broadcast_to(col, (n, n)), 0.0), axis=0, keepdims=True)


def _dec_scan_kernel(pt_ref, q_ref, ck_ref, o_ref, buf, sem, *, layer, n_pages, cp):
    b = pl.program_id(0)
    n_seq = pl.num_programs(0)
    n_chunks = n_pages // cp
    cb = cp // 2
    nbp = n_pages // 2

    def copies(w):
        slot = w % 2
        return [pltpu.make_async_copy(ck_ref.at[layer, pt_ref[w * cp + p]], buf.at[slot, p], sem.at[slot])
                for p in range(cp)]

    @pl.when(b == 0)
    def _():
        for cpy in copies(0):
            cpy.start()

    qcols = [_row_to_col(q_ref[h:h + 1, :]) for h in range(ATT_HEADS)]
    head = lax.broadcasted_iota(jnp.int32, (ATT_HEADS, LANES), 0)
    lane = lax.broadcasted_iota(jnp.int32, (ATT_HEADS, LANES), 1)

    def body(c, gates):
        w = b * n_chunks + c
        slot = w % 2

        @pl.when(w + 1 < n_seq * n_chunks)
        def _():
            for cpy in copies(w + 1):
                cpy.start()

        for cpy in copies(w):
            cpy.wait()
        for jb in range(cb):
            per_token = jnp.zeros((ATT_HEADS, LANES), F32)
            for h in range(ATT_HEADS):
                kt = buf[slot, 2 * jb, h] + buf[slot, 2 * jb + 1, h]
                per_token = jnp.where(head == h, jnp.sum(kt * qcols[h], axis=0, keepdims=True), per_token)
            g = jnp.sum(per_token, axis=1, keepdims=True) * (1.0 / MOBA_BLOCK)
            gates = jnp.where(lane == c * cb + jb, g, gates)
        return gates

    gates = lax.fori_loop(0, n_chunks, body, jnp.full((ATT_HEADS, LANES), NEG_INF, F32))
    res = jnp.zeros((ATT_HEADS, LANES), jnp.int32)
    for r in range(min(MOBA_TOPK, nbp)):
        mx = jnp.max(gates, axis=1, keepdims=True)
        ix = jnp.min(jnp.where(gates == mx, lane, LANES), axis=1, keepdims=True)
        gates = jnp.where(lane == ix, NEG_INF, gates)
        res = jnp.where(lane == r, ix, res)
    o_ref[...] = res


def _dec_scan(page_table, q_hd, cache_kt, layer):
    n, n_pages = page_table.shape
    page_size = cache_kt.shape[4]
    assert page_size * 2 == MOBA_BLOCK and page_size == LANES and n_pages // 2 <= LANES
    cp = min(32, n_pages)
    grid_spec = pltpu.PrefetchScalarGridSpec(
        num_scalar_prefetch=1,
        grid=(n,),
        in_specs=[pl.BlockSpec((None, ATT_HEADS, ATT_HD), lambda b, pt: (b, 0, 0)),
                  pl.BlockSpec(memory_space=pl.ANY)],
        out_specs=pl.BlockSpec((None, ATT_HEADS, LANES), lambda b, pt: (b, 0, 0)),
        scratch_shapes=[pltpu.VMEM((2, cp, ATT_HEADS, ATT_HD, page_size), F32),
                        pltpu.SemaphoreType.DMA((2,))],
    )
    return pl.pallas_call(
        functools.partial(_dec_scan_kernel, layer=layer, n_pages=n_pages, cp=cp),
        grid_spec=grid_spec,
        out_shape=jax.ShapeDtypeStruct((n, ATT_HEADS, LANES), jnp.int32),
        compiler_params=_params(("arbitrary",)),
    )(page_table.reshape(-1), q_hd, cache_kt)


def _dec_attend_kernel(pt_ref, pick_ref, q_ref, kn_ref, vn_ref, ck_ref, cv_ref, o_ref, kb, vb, sem, *,
                       layer, n_pages, n_sel):
    b = pl.program_id(0)
    n_tiles = 2 * n_sel

    def copies(seq):
        slot = seq % 2
        out = []
        for h in range(ATT_HEADS):
            for r in range(n_sel):
                blk = pick_ref[(seq * ATT_HEADS + h) * n_sel + r]
                for p in range(2):
                    page = pt_ref[seq * n_pages + 2 * blk + p]
                    out.append(pltpu.make_async_copy(ck_ref.at[layer, page, h], kb.at[slot, h, 2 * r + p],
                                                     sem.at[2 * slot]))
                    out.append(pltpu.make_async_copy(cv_ref.at[layer, page, h], vb.at[slot, h, 2 * r + p],
                                                     sem.at[2 * slot + 1]))
        return out

    @pl.when(b == 0)
    def _():
        for cpy in copies(0):
            cpy.start()

    @pl.when(b + 1 < pl.num_programs(0))
    def _():
        for cpy in copies(b + 1):
            cpy.start()

    for cpy in copies(b):
        cpy.wait()
    slot = b % 2
    for h in range(ATT_HEADS):
        qrow = q_ref[h:h + 1, :]
        qcol = _row_to_col(qrow)
        s_new = jnp.sum(kn_ref[h:h + 1, :] * qrow, axis=1, keepdims=True)
        vcol_new = _row_to_col(vn_ref[h:h + 1, :])
        s = [jnp.sum(kb[slot, h, t] * qcol, axis=0, keepdims=True) for t in range(n_tiles)]
        m = s_new
        for st in s:
            m = jnp.maximum(m, jnp.max(st, axis=1, keepdims=True))
        p_new = jnp.exp(s_new - m)
        den = p_new
        num = p_new * vcol_new
        for t, st in enumerate(s):
            p = jnp.exp(st - m)
            den = den + jnp.sum(p, axis=1, keepdims=True)
            num = num + jnp.sum(vb[slot, h, t] * p, axis=1, keepdims=True)
        o_ref[h:h + 1, :] = _col_to_row(num / den)


def _dec_attend(page_table, picks, q_hd, k_new, v_new, cache_kt, cache_vt, layer):
    n, n_pages = page_table.shape
    page_size = cache_kt.shape[4]
    n_sel = min(MOBA_TOPK, n_pages // 2)
    tok = pl.BlockSpec((None, ATT_HEADS, ATT_HD), lambda b, pt, pk: (b, 0, 0))
    grid_spec = pltpu.PrefetchScalarGridSpec(
        num_scalar_prefetch=2,
        grid=(n,),
        in_specs=[tok, tok, tok, pl.BlockSpec(memory_space=pl.ANY), pl.BlockSpec(memory_space=pl.ANY)],
        out_specs=tok,
        scratch_shapes=[pltpu.VMEM((2, ATT_HEADS, 2 * n_sel, ATT_HD, page_size), F32),
                        pltpu.VMEM((2, ATT_HEADS, 2 * n_sel, ATT_HD, page_size), F32),
                        pltpu.SemaphoreType.DMA((4,))],
    )
    return pl.pallas_call(
        functools.partial(_dec_attend_kernel, layer=layer, n_pages=n_pages, n_sel=n_sel),
        grid_spec=grid_spec,
        out_shape=jax.ShapeDtypeStruct((n, ATT_HEADS, ATT_HD), F32),
        compiler_params=_params(("arbitrary",)),
    )(page_table.reshape(-1), picks, q_hd, k_new, v_new, cache_kt, cache_vt)


def _att_rope_tables(pos):
    half = ROPE_DIM // 2
    inv = ROPE_THETA ** (-2.0 * jnp.arange(half, dtype=F32) / ROPE_DIM)
    ang = pos.astype(F32)[:, None] * inv[None, :]
    cos, sin = jnp.cos(ang), jnp.sin(ang)
    t = pos.shape[0]
    rest = ATT_HD - ROPE_DIM
    ca = jnp.concatenate([cos, cos, jnp.ones((t, rest), F32)], axis=1)
    cm = jnp.concatenate([-sin, jnp.zeros((t, half + rest), F32)], axis=1)
    cp = jnp.concatenate([jnp.zeros((t, half), F32), sin, jnp.zeros((t, rest), F32)], axis=1)
    rep = LANES // ATT_HD
    return tuple(jnp.tile(a, (1, rep)) for a in (ca, cm, cp))


def _ret_rope_tables(pos):
    half = RET_DK // 2
    inv = RET_THETA ** (-2.0 * jnp.arange(half, dtype=F32) / RET_DK)
    ang = pos.astype(F32)[:, None] * inv[None, :]
    cos, sin = jnp.cos(ang), jnp.sin(ang)
    c = jnp.concatenate([cos, cos], axis=1)
    s = jnp.concatenate([-sin, sin], axis=1)
    return jnp.tile(c, (1, RET_HEADS)), jnp.tile(s, (1, RET_HEADS))


def kernel(x_prompt, x_sample, cache_k, cache_v, state_ret, state_conv, page_table, c_prompt, c_sample, w_ada, b_ada,
           w_in, conv_w, w_br, w_o, ln1_g, ln1_b, w_ffn_in, w_ffn_out, ln2_g, ln2_b):
    depth = w_in.shape[0]
    alpha = (2 * depth) ** 0.25
    bsz, seq, d = x_prompt.shape
    nd = x_sample.shape[0]
    assert x_sample.shape[1] == 1, "one new token per decode sequence"
    n_pool, page_size = cache_k.shape[1], cache_k.shape[2]
    past_len = page_table.shape[1] * page_size

    ada = _ada(jnp.concatenate([c_prompt, c_sample], axis=0), w_ada, b_ada)
    w_in_b = w_in[:, :, :N_MIX].astype(BF16)
    w_gate_b = w_in[:, :, N_MIX:].astype(BF16)
    w_br_b, w_o_b = w_br.astype(BF16), w_o.astype(BF16)
    w_fi_b, w_fo_b = w_ffn_in.astype(BF16), w_ffn_out.astype(BF16)
    cache_kt = jnp.transpose(cache_k, (0, 1, 3, 4, 2))
    cache_vt = jnp.transpose(cache_v, (0, 1, 3, 4, 2))

    pos_p = jnp.arange(seq, dtype=jnp.int32)
    pos_s = jnp.full((nd,), past_len, jnp.int32)
    att_tab_p, ret_tab_p = _att_rope_tables(pos_p), _ret_rope_tables(pos_p)
    att_tab_s = _att_rope_tables(pos_s)
    ret_tab_s = tuple(a.reshape(nd, 1, RET_QK_DIM) for a in _ret_rope_tables(pos_s))
    ret_consts = _ret_consts()

    xp = x_prompt
    xs = x_sample.reshape(1, nd, d)
    outs = {k: [] for k in ("kp", "vp", "ks", "vs", "rp", "rs", "cp", "cs")}
    for l in range(depth):
        mods_p = [m.reshape(bsz, 1, d) for m in jnp.split(ada[l, :bsz], 6, axis=-1)]
        mods_s = [m.reshape(1, nd, d) for m in jnp.split(ada[l, bsz:], 6, axis=-1)]
        lg1, lb1 = ln1_g[l].reshape(1, d), ln1_b[l].reshape(1, d)
        lg2, lb2 = ln2_g[l].reshape(1, d), ln2_b[l].reshape(1, d)

        sh1, sc1, g1, sh2, sc2, g2 = mods_p
        z = _inproj(xp, sc1, sh1, w_in_b[l])
        q_t, k_hm, v_t, k_tok, v_tok = _attn_prep(z, att_tab_p, head_major=True)
        kmean = _kmean(k_tok)
        kmean_hm = kmean.reshape(bsz, -1, ATT_HEADS, ATT_HD).transpose(0, 2, 1, 3)
        a_out = _moba(q_t, k_hm, v_t, kmean_hm)
        r_out, r_state = _retention(z, ret_tab_p, ret_consts)
        xp, c_state = _merge(xp, sc1, sh1, g1, a_out, r_out, z, conv_w[l], w_gate_b[l], w_br_b[l], w_o_b[l], lg1, lb1,
                             alpha)
        xp = _ffn(xp, sc2, sh2, g2, w_fi_b[l], w_fo_b[l], lg2, lb2, alpha)
        outs["kp"].append(k_tok.reshape(bsz, seq, ATT_HEADS, ATT_HD))
        outs["vp"].append(v_tok.reshape(bsz, seq, ATT_HEADS, ATT_HD))
        outs["rp"].append(r_state)
        outs["cp"].append(c_state)

        sh1, sc1, g1, sh2, sc2, g2 = mods_s
        z = _inproj(xs, sc1, sh1, w_in_b[l])
        q_tok, k_new, v_new = _attn_prep(z, att_tab_s, head_major=False)
        q_tok, k_new, v_new = (a.reshape(nd, ATT_HEADS, ATT_HD) for a in (q_tok, k_new, v_new))
        picks = _dec_scan(page_table, q_tok, cache_kt, l)
        n_sel = min(MOBA_TOPK, page_table.shape[1] // 2)
        a_out = _dec_attend(page_table, picks[:, :, :n_sel].reshape(-1), q_tok, k_new, v_new, cache_kt, cache_vt, l)
        r_out, r_state = _ret_step(z.reshape(nd, 1, N_MIX), ret_tab_s, state_ret[l])
        prev = (state_conv[l][:, 0, :].reshape(1, nd, CONV_DIM), state_conv[l][:, 1, :].reshape(1, nd, CONV_DIM))
        xs, s_new = _merge(xs, sc1, sh1, g1, a_out.reshape(1, nd, ATT_DIM), r_out.reshape(1, nd, RET_V_DIM), z,
                           conv_w[l], w_gate_b[l], w_br_b[l], w_o_b[l], lg1, lb1, alpha, conv_prev=prev)
        xs = _ffn(xs, sc2, sh2, g2, w_fi_b[l], w_fo_b[l], lg2, lb2, alpha)
        outs["ks"].append(k_new.reshape(nd, 1, ATT_HEADS, ATT_HD))
        outs["vs"].append(v_new.reshape(nd, 1, ATT_HEADS, ATT_HD))
        outs["rs"].append(r_state)
        outs["cs"].append(jnp.stack([state_conv[l][:, 1, :], s_new.reshape(nd, CONV_DIM)], axis=1))

    st = {k: jnp.stack(v) for k, v in outs.items()}
    return (xp, xs.reshape(nd, 1, d), st["kp"], st["vp"], st["ks"], st["vs"], st["rp"], st["rs"], st["cp"], st["cs"])
```
